```python
import math
import jax, jax.numpy as jnp
from jax import lax
import numpy as np

D_MODEL = 1024
BATCH = 8
SEQ = 4096
DEPTH = 1

GRID_W = 64
CTX_LEN = 256
D_HYENA = 512
HYENA_ORDER = 2
FILTER_EMB = 33
FILTER_WIDTH = 64
FILTER_TARGET = 1e-2
FAST_DECAY_PCT = 0.3
SLOW_DECAY_PCT = 1.5
N_HEADS = 4
HEAD_DIM = 64
D_ATTN = N_HEADS * 2 * HEAD_DIM
ROPE_BASE = 10000.0
Q_BLOCK = 128
D_FF = ((8 * D_MODEL + 3 * 256 - 1) // (3 * 256)) * 256
EPS = 1e-6
OFF_Q = (HYENA_ORDER + 1) * D_HYENA
OFF_K = OFF_Q + D_ATTN
OFF_V = OFF_K + D_ATTN
OFF_G = OFF_V + D_ATTN
N_COLS = OFF_G + 2 * D_MODEL

kernel_name = 'hybrid_hyena_diffattn_dit_block'


def rmsnorm(x, g):
    xf = x.astype(jnp.float32)
    y = xf * lax.rsqrt(jnp.mean(xf * xf, axis=-1, keepdims=True) + EPS)
    return (y * g.astype(jnp.float32)).astype(x.dtype)


def modulate(h, shift, scale):
    return h * (1.0 + scale) + shift


def short_conv(u, w, b):
    L = u.shape[1]
    up = jnp.pad(u, ((0, 0), (1, 1), (0, 0)))
    return up[:, :L] * w[0] + up[:, 1:L + 1] * w[1] + up[:, 2:] * w[2] + b


def hyena_filters(L, fw1, fb1, fw2, fb2, fw3, fb3, ffreq):
    f32 = jnp.float32
    bands = (FILTER_EMB - 1) // 2
    t = jnp.linspace(0.0, 1.0, L, dtype=f32)[:, None]
    w = (2.0 * math.pi / L) * jnp.arange(L, dtype=f32)[:, None]
    fr = jnp.linspace(1e-4, bands - 1, bands, dtype=f32)[None, :]
    z = jnp.concatenate([t, jnp.cos(fr * w), -jnp.sin(fr * w)], axis=-1)
    freq = ffreq.astype(f32)
    hdn = jnp.sin(freq * (z @ fw1.astype(f32) + fb1.astype(f32)))
    hdn = jnp.sin(freq * (hdn @ fw2.astype(f32) + fb2.astype(f32)))
    h = (hdn @ fw3.astype(f32) + fb3.astype(f32)).reshape(L, HYENA_ORDER, 2, D_HYENA)
    deltas = jnp.abs(jnp.linspace(math.log(FILTER_TARGET) / SLOW_DECAY_PCT,
                                  math.log(FILTER_TARGET) / FAST_DECAY_PCT, D_HYENA, dtype=f32))
    h = h * jnp.exp(-t * deltas)[:, None, None, :]
    h_fwd, h_bwd = h[:, :, 0], h[:, :, 1]
    circ = jnp.concatenate([h_fwd, jnp.zeros((1, HYENA_ORDER, D_HYENA), f32), h_bwd[:0:-1]], axis=0)
    return jnp.fft.rfft(circ, axis=0)


def fftconv(u, hk, dbias):
    L = u.shape[1]
    uf = u.astype(jnp.float32)
    y = jnp.fft.irfft(jnp.fft.rfft(uf, n=2 * L, axis=1) * hk[None], n=2 * L, axis=1)[:, :L]
    return (y + uf * dbias.astype(jnp.float32)).astype(u.dtype)


def hyena_branch(u, conv_w, conv_b, fw1, fb1, fw2, fb2, fw3, fb3, ffreq, dbias):
    L = u.shape[1]
    u = short_conv(u, conv_w, conv_b)
    parts = jnp.split(u, HYENA_ORDER + 1, axis=-1)
    hk = hyena_filters(L, fw1, fb1, fw2, fb2, fw3, fb3, ffreq)
    z = parts[0]
    for n in range(HYENA_ORDER):
        z = parts[n + 1] * fftconv(z, hk[:, n], dbias[n])
    return z


def _rotate(x, pos):
    half = x.shape[-1] // 2
    inv = ROPE_BASE ** (-jnp.arange(half, dtype=jnp.float32) / half)
    ang = pos.astype(jnp.float32)[:, None] * inv[None, :]
    cos = jnp.cos(ang)[None, :, None, None, :].astype(x.dtype)
    sin = jnp.sin(ang)[None, :, None, None, :].astype(x.dtype)
    x1, x2 = x[..., :half], x[..., half:]
    return jnp.concatenate([x1 * cos - x2 * sin, x1 * sin + x2 * cos], axis=-1)


def rope_2d(x, row, col):
    h = x.shape[-1] // 2
    return jnp.concatenate([_rotate(x[..., :h], row), _rotate(x[..., h:], col)], axis=-1)


def diff_attention(q, k, v, lam):
    B, H, _, Lq, d = q.shape
    nb = Lq // Q_BLOCK
    qb = jnp.moveaxis(q.reshape(B, H, 2, nb, Q_BLOCK, d), 3, 0)
    scale = HEAD_DIM ** -0.5

    def block(qi):
        s = jnp.einsum('bhiqd,bhikd->bhiqk', qi, k).astype(jnp.float32) * scale
        p = jax.nn.softmax(s, axis=-1)
        a = (p[:, :, 0] - lam * p[:, :, 1]).astype(v.dtype)
        return jnp.einsum('bhqk,bhkv->bhqv', a, v)

    o = lax.map(block, qb)
    return jnp.moveaxis(o, 0, 2).reshape(B, H, Lq, 2 * d)


def heads_out(o, subln_g, lam_init):
    B, H, L, dv = o.shape
    o = rmsnorm(o, subln_g) * (1.0 - lam_init)
    return o.transpose(0, 2, 1, 3).reshape(B, L, H * dv)


def swiglu(h, w_gate, w_up, w_down):
    return (jax.nn.silu(h @ w_gate) * (h @ w_up)) @ w_down


def setup_inputs(seed: int = 0) -> dict:
    key = jax.random.key(seed)
    ks = jax.random.split(key, 40)
    f32 = jnp.float32

    def nrm(i, shape, s):
        return jax.random.normal(ks[i], shape, f32) * s

    return {
        'x': nrm(0, (BATCH, SEQ, D_MODEL), 1.0),
        'c': nrm(1, (BATCH, D_MODEL), 1.0),
        'ctx': nrm(2, (BATCH, CTX_LEN, D_MODEL), 1.0),
        'c_ctx': nrm(3, (D_MODEL,), 1.0),
        'w_ada': nrm(4, (DEPTH, D_MODEL, 6 * D_MODEL), 0.5 * D_MODEL ** -0.5),
        'b_ada': nrm(5, (DEPTH, 6 * D_MODEL), 0.01),
        'g_mix_pre': 1.0 + nrm(6, (DEPTH, D_MODEL), 0.05),
        'g_mix_post': 1.0 + nrm(7, (DEPTH, D_MODEL), 0.05),
        'g_ffn_pre': 1.0 + nrm(8, (DEPTH, D_MODEL), 0.05),
        'g_ffn_post': 1.0 + nrm(9, (DEPTH, D_MODEL), 0.05),
        'w_in': nrm(10, (DEPTH, D_MODEL, N_COLS), D_MODEL ** -0.5),
        'hy_conv_w': nrm(11, (DEPTH, 3, (HYENA_ORDER + 1) * D_HYENA), 0.5),
        'hy_conv_b': nrm(12, (DEPTH, (HYENA_ORDER + 1) * D_HYENA), 0.01),
        'hy_f_w1': nrm(13, (DEPTH, FILTER_EMB, FILTER_WIDTH), FILTER_EMB ** -0.5),
        'hy_f_b1': nrm(14, (DEPTH, FILTER_WIDTH), 0.1),
        'hy_f_w2': nrm(15, (DEPTH, FILTER_WIDTH, FILTER_WIDTH), FILTER_WIDTH ** -0.5),
        'hy_f_b2': nrm(16, (DEPTH, FILTER_WIDTH), 0.1),
        'hy_f_w3': nrm(17, (DEPTH, FILTER_WIDTH, HYENA_ORDER * 2 * D_HYENA), 0.03 * FILTER_WIDTH ** -0.5),
        'hy_f_b3': nrm(18, (DEPTH, HYENA_ORDER * 2 * D_HYENA), 0.003),
        'hy_f_freq': 1.0 + nrm(19, (DEPTH, FILTER_WIDTH), 0.05),
        'hy_bias': nrm(20, (DEPTH, HYENA_ORDER, D_HYENA), 1.0),
        'lambda_q1': nrm(21, (DEPTH, HEAD_DIM), 0.1),
        'lambda_k1': nrm(22, (DEPTH, HEAD_DIM), 0.1),
        'lambda_q2': nrm(23, (DEPTH, HEAD_DIM), 0.1),
        'lambda_k2': nrm(24, (DEPTH, HEAD_DIM), 0.1),
        'att_subln_g': 1.0 + nrm(25, (DEPTH, 2 * HEAD_DIM), 0.05),
        'w_hy_up': nrm(26, (DEPTH, D_HYENA, D_MODEL), D_HYENA ** -0.5),
        'w_att_up': nrm(27, (DEPTH, D_ATTN, D_MODEL), D_ATTN ** -0.5),
        'w_out': nrm(28, (DEPTH, D_MODEL, D_MODEL), D_MODEL ** -0.5),
        'w_ffn_gate': nrm(29, (DEPTH, D_MODEL, D_FF), D_MODEL ** -0.5),
        'w_ffn_up': nrm(30, (DEPTH, D_MODEL, D_FF), D_MODEL ** -0.5),
        'w_ffn_down': nrm(31, (DEPTH, D_FF, D_MODEL), D_FF ** -0.5),
    }


def reference(x, c, ctx, c_ctx, w_ada, b_ada, g_mix_pre, g_mix_post, g_ffn_pre, g_ffn_post,
              w_in, hy_conv_w, hy_conv_b, hy_f_w1, hy_f_b1, hy_f_w2, hy_f_b2, hy_f_w3, hy_f_b3,
              hy_f_freq, hy_bias, lambda_q1, lambda_k1, lambda_q2, lambda_k2, att_subln_g,
              w_hy_up, w_att_up, w_out, w_ffn_gate, w_ffn_up, w_ffn_down):
    B, L, D = x.shape
    C = ctx.shape[1]
    rows = L // GRID_W
    row = jnp.repeat(jnp.arange(rows, dtype=jnp.int32), GRID_W)
    col = jnp.tile(jnp.arange(GRID_W, dtype=jnp.int32), rows)
    f32 = jnp.float32

    for layer in range(DEPTH):
        last = layer == DEPTH - 1
        hy_p = (hy_conv_w[layer], hy_conv_b[layer], hy_f_w1[layer], hy_f_b1[layer], hy_f_w2[layer],
                hy_f_b2[layer], hy_f_w3[layer], hy_f_b3[layer], hy_f_freq[layer], hy_bias[layer])
        ada = jax.nn.silu(c) @ w_ada[layer] + b_ada[layer]
        sh1, sc1, g1, sh2, sc2, g2 = jnp.split(ada[:, None, :], 6, axis=-1)
        ada_c = jax.nn.silu(c_ctx) @ w_ada[layer] + b_ada[layer]
        csh1, csc1, cg1, csh2, csc2, cg2 = jnp.split(ada_c, 6, axis=-1)

        lam_init = 0.8 - 0.6 * math.exp(-0.3 * layer)
        lam = (jnp.exp(jnp.sum(lambda_q1[layer].astype(f32) * lambda_k1[layer].astype(f32)))
               - jnp.exp(jnp.sum(lambda_q2[layer].astype(f32) * lambda_k2[layer].astype(f32))) + lam_init)

        hc = modulate(rmsnorm(ctx, g_mix_pre[layer]), csh1, csc1)
        if last:
            kv_c = hc @ w_in[layer][:, OFF_K:OFF_G]
        else:
            proj_c = hc @ w_in[layer]
            kv_c = proj_c[..., OFF_K:OFF_G]
        k_c = kv_c[..., :D_ATTN].reshape(B, C, N_HEADS, 2, HEAD_DIM)
        v_c = kv_c[..., D_ATTN:].reshape(B, C, N_HEADS, 2 * HEAD_DIM)

        h = modulate(rmsnorm(x, g_mix_pre[layer]), sh1, sc1)
        proj = h @ w_in[layer]
        y_hy = hyena_branch(proj[..., :OFF_Q], *hy_p) @ w_hy_up[layer]
        q = rope_2d(proj[..., OFF_Q:OFF_K].reshape(B, L, N_HEADS, 2, HEAD_DIM), row, col)
        k = rope_2d(proj[..., OFF_K:OFF_V].reshape(B, L, N_HEADS, 2, HEAD_DIM), row, col)
        v = proj[..., OFF_V:OFF_G].reshape(B, L, N_HEADS, 2 * HEAD_DIM)
        k_all = jnp.concatenate([k, k_c], axis=1).transpose(0, 2, 3, 1, 4)
        v_all = jnp.concatenate([v, v_c], axis=1).transpose(0, 2, 1, 3)
        o = diff_attention(q.transpose(0, 2, 3, 1, 4), k_all, v_all, lam)
        y_att = heads_out(o, att_subln_g[layer], lam_init) @ w_att_up[layer]
        g_hy, g_att = jnp.split(jax.nn.sigmoid(proj[..., OFF_G:]), 2, axis=-1)
        mixed = (g_hy * y_hy + g_att * y_att) @ w_out[layer]
        x = x + g1 * rmsnorm(mixed, g_mix_post[layer])
        hf = modulate(rmsnorm(x, g_ffn_pre[layer]), sh2, sc2)
        f = swiglu(hf, w_ffn_gate[layer], w_ffn_up[layer], w_ffn_down[layer])
        x = x + g2 * rmsnorm(f, g_ffn_post[layer])

        if not last:
            qc = proj_c[..., OFF_Q:OFF_K].reshape(B, C, N_HEADS, 2, HEAD_DIM).transpose(0, 2, 3, 1, 4)
            oc = diff_attention(qc, k_c.transpose(0, 2, 3, 1, 4), v_c.transpose(0, 2, 1, 3), lam)
            yc_att = heads_out(oc, att_subln_g[layer], lam_init) @ w_att_up[layer]
            yc_hy = hyena_branch(proj_c[..., :OFF_Q], *hy_p) @ w_hy_up[layer]
            gc_hy, gc_att = jnp.split(jax.nn.sigmoid(proj_c[..., OFF_G:]), 2, axis=-1)
            mixed_c = (gc_hy * yc_hy + gc_att * yc_att) @ w_out[layer]
            ctx = ctx + cg1 * rmsnorm(mixed_c, g_mix_post[layer])
            hfc = modulate(rmsnorm(ctx, g_ffn_pre[layer]), csh2, csc2)
            fc = swiglu(hfc, w_ffn_gate[layer], w_ffn_up[layer], w_ffn_down[layer])
            ctx = ctx + cg2 * rmsnorm(fc, g_ffn_post[layer])
    return x
```

```python
import functools
import math

import numpy as np
import jax
import jax.numpy as jnp
from jax import lax
from jax.experimental import pallas as pl
from jax.experimental.pallas import tpu as pltpu

F32 = jnp.float32
BF16 = jnp.bfloat16
HIGHEST = lax.Precision.HIGHEST

EPS = 1e-6
GRID_W = 64
HEAD_DIM = 64
N_HEADS = 4
D_HYENA = 512
D_ATTN = N_HEADS * 2 * HEAD_DIM
ROPE_BASE = 10000.0
FILTER_TARGET = 1e-2
FAST_DECAY_PCT = 0.3
SLOW_DECAY_PCT = 1.5
LAM_INIT = 0.8 - 0.6 * math.exp(0.0)

LANES = 128
MXU_DIM = 256
VMEM_LIMIT_BYTES = 56 * 1024 * 1024

FFT_N2 = 256
FFT_GROUP = 16
HY_CB = 256


def _dot(a, b):
    return jnp.dot(a, b, preferred_element_type=F32)


def _dot_hi(a, b):
    return jnp.dot(a, b, preferred_element_type=F32, precision=HIGHEST)


def _dot_nt(a, b):
    return lax.dot_general(a, b, (((1,), (1,)), ((), ())), preferred_element_type=F32)


def _rms(x, g):
    return x * lax.rsqrt(jnp.mean(x * x, axis=-1, keepdims=True) + EPS) * g


def _const_spec(shape):
    zeros = (0,) * len(shape)
    return pl.BlockSpec(shape, lambda *_: zeros, pipeline_mode=pl.Buffered(1))


def _params(*semantics):
    return pltpu.CompilerParams(dimension_semantics=semantics, vmem_limit_bytes=VMEM_LIMIT_BYTES)


def _ada_body(c_ref, w_ref, b_ref, o_ref):
    c = c_ref[...]
    o_ref[...] = _dot_hi(c * jax.nn.sigmoid(c), w_ref[...]) + b_ref[...]


def _ada(cc, w_ada, b_ada):
    rows, d = cc.shape
    n = w_ada.shape[1]
    tn = 1024
    return pl.pallas_call(
        _ada_body,
        grid=(n // tn,),
        in_specs=[
            pl.BlockSpec((rows, d), lambda j: (0, 0)),
            pl.BlockSpec((d, tn), lambda j: (0, j)),
            pl.BlockSpec((1, tn), lambda j: (0, j)),
        ],
        out_specs=pl.BlockSpec((rows, tn), lambda j: (0, j)),
        out_shape=jax.ShapeDtypeStruct((rows, n), F32),
        compiler_params=_params("arbitrary"),
        name="ada",
    )(cc, w_ada, b_ada)


def _ctx_kv_body(x_ref, g_ref, sh_ref, sc_ref, wk_ref, wv_ref, k_ref, v_ref):
    h = _rms(x_ref[...], g_ref[...]) * (1.0 + sc_ref[...]) + sh_ref[...]
    hb = h.astype(BF16)
    k_ref[...] = _dot(hb, wk_ref[...]).astype(BF16)
    v_ref[...] = _dot(hb, wv_ref[...]).astype(BF16)


def _ctx_kv(ctx2d, g, sh, sc, wk, wv):
    rows, d = ctx2d.shape
    n = wk.shape[1]
    tm = 512
    row = lambda i: (i, 0)
    return pl.pallas_call(
        _ctx_kv_body,
        grid=(rows // tm,),
        in_specs=[
            pl.BlockSpec((tm, d), row),
            _const_spec((1, d)), _const_spec((1, d)), _const_spec((1, d)),
            _const_spec((d, n)), _const_spec((d, n)),
        ],
        out_specs=[pl.BlockSpec((tm, n), row), pl.BlockSpec((tm, n), row)],
        out_shape=[jax.ShapeDtypeStruct((rows, n), BF16)] * 2,
        compiler_params=_params("arbitrary"),
        name="ctx_kv",
    )(ctx2d, g, sh, sc, wk, wv)


def _proj_body(x_ref, g_ref, sh_ref, sc_ref, cos_ref, sin_ref,
               wu_ref, wq_ref, wqs_ref, wk_ref, wks_ref, wv_ref, wg_ref,
               u_ref, q_ref, k_ref, v_ref, gate_ref):
    h = _rms(x_ref[0], g_ref[...]) * (1.0 + sc_ref[0]) + sh_ref[0]
    hb = h.astype(BF16)
    u_ref[0] = _dot(hb, wu_ref[...]).astype(BF16)
    cos = cos_ref[...]
    sin = sin_ref[...]
    q_ref[0] = (_dot(hb, wq_ref[...]) * cos + _dot(hb, wqs_ref[...]) * sin).astype(BF16)
    k_ref[0] = (_dot(hb, wk_ref[...]) * cos + _dot(hb, wks_ref[...]) * sin).astype(BF16)
    v_ref[0] = _dot(hb, wv_ref[...]).astype(BF16)
    gate_ref[0] = jax.nn.sigmoid(_dot(hb, wg_ref[...])).astype(BF16)


def _proj(x, g, sh, sc, cos, sin, wu, wq, wqs, wk, wks, wv, wg):
    b, l, d = x.shape
    tm = 512
    tile = lambda i, bi: (bi, i, 0)
    per_b = lambda i, bi: (bi, 0, 0)
    table = lambda i, bi: (i, 0)
    widths = (wu.shape[1], wq.shape[1], wk.shape[1], wv.shape[1], wg.shape[1])
    return pl.pallas_call(
        _proj_body,
        grid=(l // tm, b),
        in_specs=[
            pl.BlockSpec((1, tm, d), tile),
            _const_spec((1, d)),
            pl.BlockSpec((1, 1, d), per_b), pl.BlockSpec((1, 1, d), per_b),
            pl.BlockSpec((tm, D_ATTN), table), pl.BlockSpec((tm, D_ATTN), table),
        ] + [_const_spec(w.shape) for w in (wu, wq, wqs, wk, wks, wv, wg)],
        out_specs=[pl.BlockSpec((1, tm, n), tile) for n in widths],
        out_shape=[jax.ShapeDtypeStruct((b, l, n), BF16) for n in widths],
        compiler_params=_params("arbitrary", "arbitrary"),
        name="proj",
    )(x, g, sh, sc, cos, sin, wu, wq, wqs, wk, wks, wv, wg)


def _fft_dims(l):
    n = 2 * l
    n1 = n // FFT_N2
    h1 = n1 // 2
    k1 = h1 + 1
    ng = FFT_N2 // FFT_GROUP
    return n, n1, h1, k1, ng


@functools.lru_cache(maxsize=None)
def _fft_constants(l):
    n, n1, h1, k1n, ng = _fft_dims(l)
    g = FFT_GROUP
    ma = np.zeros((ng, k1n, 2, g, h1, g), np.float64)
    for gi in range(ng):
        for j in range(g):
            n2 = gi * g + j
            nn = FFT_N2 * np.arange(h1)[None, :] + n2
            theta = 2.0 * np.pi * ((np.arange(k1n)[:, None] * nn) % n) / n
            ma[gi, :, 0, j, :, j] = np.cos(theta)
            ma[gi, :, 1, j, :, j] = -np.sin(theta)
    weight = np.full((k1n,), 2.0)
    weight[0] = weight[-1] = 1.0
    ma_fwd = ma.reshape(ng, k1n * 2 * g, h1 * g)
    ma_inv = (ma * (weight / n)[None, :, None, None, None, None]).reshape(ng, k1n * 2 * g, h1 * g)
    ma_inv = np.transpose(ma_inv, (0, 2, 1))
    kpad = -(-ma_inv.shape[2] // LANES) * LANES
    ma_inv = np.pad(ma_inv, ((0, 0), (0, 0), (0, kpad - ma_inv.shape[2])))
    ang = 2.0 * np.pi * ((np.arange(FFT_N2)[:, None] * np.arange(FFT_N2)[None, :]) % FFT_N2) / FFT_N2
    fr, fi = np.cos(ang), -np.sin(ang)
    mb_fwd = np.block([[fr, -fi], [fi, fr]])
    mb_inv = np.block([[fr, fi], [-fi, fr]])
    return tuple(np.asarray(a, np.float32) for a in (ma_fwd, ma_inv, mb_fwd, mb_inv))


def _fft_stage_a(src_ref, ma_ref, spec_ref, l):
    _, _, h1, k1n, ng = _fft_dims(l)
    g = FFT_GROUP

    def group(gi, carry):
        off = gi * g
        xg = jnp.concatenate(
            [src_ref[pl.ds(pl.multiple_of(FFT_N2 * i + off, g), g), :] for i in range(h1)], axis=0)
        bg = _dot(ma_ref[gi], xg.astype(BF16)).astype(BF16)
        for c in range(2 * k1n):
            spec_ref[pl.ds(pl.multiple_of(c * FFT_N2 + off, g), g), :] = bg[c * g:(c + 1) * g]
        return carry

    lax.fori_loop(0, ng, group, 0)


def _fft_stage_a_inv(spec_ref, ma_inv_ref, dst_ref, l):
    _, _, h1, k1n, ng = _fft_dims(l)
    g = FFT_GROUP
    kpad = ma_inv_ref.shape[2] - 2 * k1n * g

    def group(gi, carry):
        off = gi * g
        chunks = [spec_ref[pl.ds(pl.multiple_of(c * FFT_N2 + off, g), g), :] for c in range(2 * k1n)]
        if kpad:
            chunks.append(jnp.zeros((kpad, spec_ref.shape[1]), BF16))
        yg = _dot(ma_inv_ref[gi], jnp.concatenate(chunks, axis=0))
        for i in range(h1):
            dst_ref[pl.ds(pl.multiple_of(FFT_N2 * i + off, g), g), :] = yg[i * g:(i + 1) * g]
        return carry

    lax.fori_loop(0, ng, group, 0)


def _filt_body(z_ref, w1_ref, b1_ref, w2_ref, b2_ref, fq_ref, w3f_ref, b3f_ref, w3b_ref, b3b_ref,
               t_ref, dl_ref, ma_ref, mb_ref, hf_ref, sig_ref, spec_ref, acc_ref, *, l):
    _, _, _, k1n, _ = _fft_dims(l)
    fq = fq_ref[...]
    hdn = jnp.sin(fq * (_dot_hi(z_ref[...], w1_ref[...]) + b1_ref[...]))
    hdn = jnp.sin(fq * (_dot_hi(hdn, w2_ref[...]) + b2_ref[...]))
    window = jnp.exp(-t_ref[...] * dl_ref[...])
    rows = lax.broadcasted_iota(jnp.int32, window.shape, 0)
    blk = 2 * FFT_N2

    sig_ref[...] = (_dot_hi(hdn, w3f_ref[...]) + b3f_ref[...]) * window
    _fft_stage_a(sig_ref, ma_ref, spec_ref, l)

    def fwd_k1(i, carry):
        rows_i = pl.ds(pl.multiple_of(i * blk, blk), blk)
        acc_ref[rows_i, :] = _dot(mb_ref[...], spec_ref[rows_i, :])
        return carry

    lax.fori_loop(0, k1n, fwd_k1, 0)

    sig_ref[...] = jnp.where(rows > 0, (_dot_hi(hdn, w3b_ref[...]) + b3b_ref[...]) * window, 0.0)
    _fft_stage_a(sig_ref, ma_ref, spec_ref, l)

    def bwd_k1(i, carry):
        rows_i = pl.ds(pl.multiple_of(i * blk, blk), blk)
        s = _dot(mb_ref[...], spec_ref[rows_i, :])
        a = acc_ref[rows_i, :]
        hf_ref[0, rows_i, :] = jnp.concatenate(
            [a[:FFT_N2] + s[:FFT_N2], a[FFT_N2:] - s[FFT_N2:]], axis=0).astype(BF16)
        return carry

    lax.fori_loop(0, k1n, bwd_k1, 0)


def _filt(zfeat, w1, b1, w2, b2, fq, w3, b3, tcol, deltas, ma_fwd, mb_fwd, order, l):
    _, _, _, k1n, _ = _fft_dims(l)
    cb = HY_CB
    nblk = D_HYENA // cb
    srows = k1n * 2 * FFT_N2
    fcol = lambda o, c: (0, (2 * o) * nblk + c)
    bcol = lambda o, c: (0, (2 * o + 1) * nblk + c)
    return pl.pallas_call(
        functools.partial(_filt_body, l=l),
        grid=(order, nblk),
        in_specs=[
            _const_spec(zfeat.shape), _const_spec(w1.shape), _const_spec(b1.shape),
            _const_spec(w2.shape), _const_spec(b2.shape), _const_spec(fq.shape),
            pl.BlockSpec((w3.shape[0], cb), fcol), pl.BlockSpec((1, cb), fcol),
            pl.BlockSpec((w3.shape[0], cb), bcol), pl.BlockSpec((1, cb), bcol),
            _const_spec(tcol.shape), pl.BlockSpec((1, cb), lambda o, c: (0, c)),
            _const_spec(ma_fwd.shape), _const_spec(mb_fwd.shape),
        ],
        out_specs=pl.BlockSpec((1, srows, cb), lambda o, c: (o, 0, c)),
        out_shape=jax.ShapeDtypeStruct((order, srows, D_HYENA), BF16),
        scratch_shapes=[
            pltpu.VMEM((l, cb), F32),
            pltpu.VMEM((srows, cb), BF16),
            pltpu.VMEM((srows, cb), F32),
        ],
        compiler_params=_params("arbitrary", "arbitrary"),
        name="filt",
    )(zfeat, w1, b1, w2, b2, fq, w3, b3, w3, b3, tcol, deltas, ma_fwd, mb_fwd)


CONV_ROWS = 512
HALO = 16


def _short_conv_chunk(u_ref, w_ref, b_ref, r0, l):
    c = u_ref.shape[2]
    zeros = jnp.zeros((HALO, c), F32)
    before = zeros if r0 == 0 else u_ref[0, r0 - HALO:r0, :].astype(F32)
    after = zeros if r0 + CONV_ROWS == l else u_ref[0, r0 + CONV_ROWS:r0 + CONV_ROWS + HALO, :].astype(F32)
    ext = jnp.concatenate([before, u_ref[0, r0:r0 + CONV_ROWS, :].astype(F32), after], axis=0)
    w = w_ref[...]
    return (ext[HALO - 1:HALO - 1 + CONV_ROWS] * w[0:1]
            + ext[HALO:HALO + CONV_ROWS] * w[1:2]
            + ext[HALO + 1:HALO + 1 + CONV_ROWS] * w[2:3]
            + b_ref[...])


def _hyena_body(u0_ref, u1_ref, u2_ref, w0_ref, w1_ref, w2_ref, b0_ref, b1_ref, b2_ref, db_ref, hf_ref,
                ma_ref, mai_ref, mb_ref, mbi_ref, o_ref, z_ref, y_ref, spec_ref, *, l, order):
    _, _, _, k1n, _ = _fft_dims(l)
    blk = 2 * FFT_N2
    gates = ((u1_ref, w1_ref, b1_ref), (u2_ref, w2_ref, b2_ref))

    for r0 in range(0, l, CONV_ROWS):
        z_ref[r0:r0 + CONV_ROWS, :] = _short_conv_chunk(u0_ref, w0_ref, b0_ref, r0, l)

    for o in range(order):
        _fft_stage_a(z_ref, ma_ref, spec_ref, l)

        def per_k1(i, carry, o=o):
            rows_i = pl.ds(pl.multiple_of(i * blk, blk), blk)
            s = _dot(mb_ref[...], spec_ref[rows_i, :])
            h = hf_ref[o, rows_i, :].astype(F32)
            sr, si, hr, hi = s[:FFT_N2], s[FFT_N2:], h[:FFT_N2], h[FFT_N2:]
            y = jnp.concatenate([sr * hr - si * hi, sr * hi + si * hr], axis=0).astype(BF16)
            spec_ref[rows_i, :] = _dot(mbi_ref[...], y).astype(BF16)
            return carry

        lax.fori_loop(0, k1n, per_k1, 0)
        _fft_stage_a_inv(spec_ref, mai_ref, y_ref, l)

        u_ref, w_ref, b_ref = gates[o]
        dbias = db_ref[o:o + 1, :]
        for r0 in range(0, l, CONV_ROWS):
            rows = slice(r0, r0 + CONV_ROWS)
            znew = _short_conv_chunk(u_ref, w_ref, b_ref, r0, l) * (y_ref[rows, :] + z_ref[rows, :] * dbias)
            if o + 1 < order:
                z_ref[rows, :] = znew
            else:
                o_ref[0, rows, :] = znew.astype(BF16)


def _hyena(u, conv_w, conv_b, dbias, hf, consts, order):
    b, l, _ = u.shape
    _, _, _, k1n, _ = _fft_dims(l)
    cb = HY_CB
    nblk = D_HYENA // cb
    srows = k1n * 2 * FFT_N2
    ma_fwd, ma_inv, mb_fwd, mb_inv = consts
    upart = lambda p: pl.BlockSpec((1, l, cb), lambda c, bi: (bi, 0, p * nblk + c))
    wpart = lambda p: pl.BlockSpec((3, cb), lambda c, bi: (0, p * nblk + c))
    bpart = lambda p: pl.BlockSpec((1, cb), lambda c, bi: (0, p * nblk + c))
    return pl.pallas_call(
        functools.partial(_hyena_body, l=l, order=order),
        grid=(nblk, b),
        in_specs=[
            upart(0), upart(1), upart(2), wpart(0), wpart(1), wpart(2), bpart(0), bpart(1), bpart(2),
            pl.BlockSpec((order, cb), lambda c, bi: (0, c)),
            pl.BlockSpec((order, srows, cb), lambda c, bi: (0, 0, c), pipeline_mode=pl.Buffered(1)),
            _const_spec(ma_fwd.shape), _const_spec(ma_inv.shape),
            _const_spec(mb_fwd.shape), _const_spec(mb_inv.shape),
        ],
        out_specs=pl.BlockSpec((1, l, cb), lambda c, bi: (bi, 0, c)),
        out_shape=jax.ShapeDtypeStruct((b, l, D_HYENA), BF16),
        scratch_shapes=[
            pltpu.VMEM((l, cb), F32),
            pltpu.VMEM((l, cb), F32),
            pltpu.VMEM((srows, cb), BF16),
        ],
        compiler_params=_params("arbitrary", "arbitrary"),
        name="hyena",
    )(u, u, u, conv_w, conv_w, conv_w, conv_b, conv_b, conv_b, dbias, hf, ma_fwd, ma_inv, mb_fwd, mb_inv)


def _attn_body(q_ref, k_ref, v_ref, kc_ref, vc_ref, lam_ref, sg_ref, o_ref):
    tq = q_ref.shape[1]
    lv = lam_ref[...]
    lam = (jnp.exp(jnp.sum(lv[0:1] * lv[1:2], axis=-1, keepdims=True))
           - jnp.exp(jnp.sum(lv[2:3] * lv[3:4], axis=-1, keepdims=True)) + LAM_INIT)
    q = q_ref[0]
    lane = lax.broadcasted_iota(jnp.int32, q.shape, 1)
    qq = jnp.concatenate([jnp.where(lane < HEAD_DIM, q, jnp.zeros_like(q)),
                          jnp.where(lane >= HEAD_DIM, q, jnp.zeros_like(q))], axis=0)
    s_l = _dot_nt(qq, k_ref[0])
    s_c = _dot_nt(qq, kc_ref[0])
    m = jnp.maximum(jnp.max(s_l, axis=-1, keepdims=True), jnp.max(s_c, axis=-1, keepdims=True))
    e_l = jnp.exp(s_l - m)
    e_c = jnp.exp(s_c - m)
    r = 1.0 / (jnp.sum(e_l, axis=-1, keepdims=True) + jnp.sum(e_c, axis=-1, keepdims=True))
    r1 = r[:tq]
    r2 = r[tq:] * lam
    a_l = (e_l[:tq] * r1 - e_l[tq:] * r2).astype(BF16)
    a_c = (e_c[:tq] * r1 - e_c[tq:] * r2).astype(BF16)
    o = _dot(a_l, v_ref[0]) + _dot(a_c, vc_ref[0])
    o_ref[0] = (_rms(o, sg_ref[...]) * (1.0 - LAM_INIT)).astype(BF16)


def _attn(q, k, v, kc, vc, lamv, subln_g):
    b, l, _ = q.shape
    c = kc.shape[1]
    dv = 2 * HEAD_DIM
    tq = 256
    return pl.pallas_call(
        _attn_body,
        grid=(b, N_HEADS, l // tq),
        in_specs=[
            pl.BlockSpec((1, tq, dv), lambda bi, h, i: (bi, i, h)),
            pl.BlockSpec((1, l, dv), lambda bi, h, i: (bi, 0, h)),
            pl.BlockSpec((1, l, dv), lambda bi, h, i: (bi, 0, h)),
            pl.BlockSpec((1, c, dv), lambda bi, h, i: (bi, 0, h)),
            pl.BlockSpec((1, c, dv), lambda bi, h, i: (bi, 0, h)),
            _const_spec(lamv.shape), _const_spec(subln_g.shape),
        ],
        out_specs=pl.BlockSpec((1, tq, dv), lambda bi, h, i: (bi, i, h)),
        out_shape=jax.ShapeDtypeStruct((b, l, N_HEADS * dv), BF16),
        compiler_params=_params("arbitrary", "arbitrary", "arbitrary"),
        name="attn",
    )(q, k, v, kc, vc, lamv, subln_g)


def _post_body(x_ref, zh_ref, oa_ref, gate_ref, g1_ref, sh2_ref, sc2_ref, g2_ref,
               gpost_ref, gfpre_ref, gfpost_ref,
               whu_ref, wau_ref, wo_ref, wg_ref, wu_ref, wd_ref, o_ref, *, ff_chunks):
    d = x_ref.shape[2]
    y_hy = _dot(zh_ref[0], whu_ref[...])
    y_att = _dot(oa_ref[0], wau_ref[...])
    mixed = gate_ref[0, :, :d].astype(F32) * y_hy + gate_ref[0, :, d:].astype(F32) * y_att
    mixed = _dot(mixed.astype(BF16), wo_ref[...])
    x1 = x_ref[0] + g1_ref[0] * _rms(mixed, gpost_ref[...])
    hf = (_rms(x1, gfpre_ref[...]) * (1.0 + sc2_ref[0]) + sh2_ref[0]).astype(BF16)
    f = None
    for c0, cw in ff_chunks:
        a = _dot(hf, wg_ref[:, c0:c0 + cw])
        t = (a * jax.nn.sigmoid(a) * _dot(hf, wu_ref[:, c0:c0 + cw])).astype(BF16)
        part = _dot(t, wd_ref[c0:c0 + cw, :])
        f = part if f is None else f + part
    o_ref[0] = x1 + g2_ref[0] * _rms(f, gfpost_ref[...])


def _post(x, zh, oa, gate, g1, sh2, sc2, g2, gpost, gfpre, gfpost, whu, wau, wo, wg, wu, wd):
    b, l, d = x.shape
    dff = wg.shape[1]
    tm = 512
    ff_chunk = 4 * MXU_DIM
    ff_chunks = tuple((c0, min(ff_chunk, dff - c0)) for c0 in range(0, dff, ff_chunk))
    tile = lambda bi, i: (bi, i, 0)
    per_b = lambda bi, i: (bi, 0, 0)
    return pl.pallas_call(
        functools.partial(_post_body, ff_chunks=ff_chunks),
        grid=(b, l // tm),
        in_specs=[
            pl.BlockSpec((1, tm, d), tile),
            pl.BlockSpec((1, tm, zh.shape[2]), tile),
            pl.BlockSpec((1, tm, oa.shape[2]), tile),
            pl.BlockSpec((1, tm, gate.shape[2]), tile),
            pl.BlockSpec((1, 1, d), per_b), pl.BlockSpec((1, 1, d), per_b),
            pl.BlockSpec((1, 1, d), per_b), pl.BlockSpec((1, 1, d), per_b),
            _const_spec((1, d)), _const_spec((1, d)), _const_spec((1, d)),
        ] + [_const_spec(w.shape) for w in (whu, wau, wo, wg, wu, wd)],
        out_specs=pl.BlockSpec((1, tm, d), tile),
        out_shape=jax.ShapeDtypeStruct((b, l, d), F32),
        compiler_params=_params("arbitrary", "arbitrary"),
        name="post",
    )(x, zh, oa, gate, g1, sh2, sc2, g2, gpost, gfpre, gfpost, whu, wau, wo, wg, wu, wd)


@functools.lru_cache(maxsize=None)
def _rope_tables(l):
    quarter = HEAD_DIM // 4
    inv = ROPE_BASE ** (-np.arange(quarter, dtype=np.float64) / quarter)
    t = np.arange(l)
    ang_r = (t // GRID_W)[:, None] * inv[None, :]
    ang_c = (t % GRID_W)[:, None] * inv[None, :]
    ang = np.concatenate([ang_r, ang_r, ang_c, ang_c], axis=1)
    reps = D_ATTN // HEAD_DIM
    cos = np.tile(np.cos(ang), (1, reps)).astype(np.float32)
    sin = np.tile(np.sin(ang), (1, reps)).astype(np.float32)
    j = np.arange(D_ATTN)
    second = (j % (2 * quarter)) >= quarter
    perm = np.where(second, j - quarter, j + quarter)
    sign = np.where(second, 1.0, -1.0).astype(np.float32)
    return cos, sin, perm, sign


@functools.lru_cache(maxsize=None)
def _filter_features(l, emb, kpad):
    bands = (emb - 1) // 2
    t = np.linspace(0.0, 1.0, l, dtype=np.float32).astype(np.float64)[:, None]
    w = (2.0 * math.pi / l) * np.arange(l, dtype=np.float64)[:, None]
    fr = np.linspace(1e-4, bands - 1, bands, dtype=np.float32).astype(np.float64)[None, :]
    z = np.concatenate([t, np.cos(fr * w), -np.sin(fr * w)], axis=-1)
    z = np.pad(z, ((0, 0), (0, kpad - emb))).astype(np.float32)
    deltas = np.abs(np.linspace(math.log(FILTER_TARGET) / SLOW_DECAY_PCT,
                                math.log(FILTER_TARGET) / FAST_DECAY_PCT, D_HYENA, dtype=np.float32))
    return z, t.astype(np.float32), deltas[None, :].astype(np.float32)


def _pad_to(a, rows, cols, value=0.0):
    return jnp.pad(a, ((0, rows - a.shape[0]), (0, cols - a.shape[1])), constant_values=value)


def kernel(x, c, ctx, c_ctx, w_ada, b_ada, g_mix_pre, g_mix_post, g_ffn_pre, g_ffn_post, w_in, hy_conv_w, hy_conv_b, hy_f_w1, hy_f_b1, hy_f_w2, hy_f_b2, hy_f_w3, hy_f_b3, hy_f_freq, hy_bias, lambda_q1, lambda_k1, lambda_q2, lambda_k2, att_subln_g, w_hy_up, w_att_up, w_out, w_ffn_gate, w_ffn_up, w_ffn_down):
    b, l, d = x.shape
    n_ctx = ctx.shape[1]
    order = hy_bias.shape[1]
    off_q = (order + 1) * D_HYENA
    off_k = off_q + D_ATTN
    off_v = off_k + D_ATTN
    off_g = off_v + D_ATTN
    assert w_in.shape[0] == 1, "single-layer block"
    assert l % CONV_ROWS == 0 and l % 512 == 0 and (b * n_ctx) % 512 == 0

    pad_rows = -(-(b + 1) // 8) * 8
    cc = jnp.concatenate([c, c_ctx[None, :], jnp.zeros((pad_rows - b - 1, d), F32)], axis=0)
    ada = _ada(cc, w_ada[0], b_ada[0][None, :])
    sh1, sc1, g1, sh2, sc2, g2 = (ada[:b, i * d:(i + 1) * d][:, None, :] for i in range(6))
    csh1, csc1 = ada[b:b + 1, 0:d], ada[b:b + 1, d:2 * d]

    row = lambda v: v[0][None, :]
    w = w_in[0].astype(BF16)
    cos, sin, perm, sign = _rope_tables(l)
    scale = HEAD_DIM ** -0.5
    wq = w[:, off_q:off_k]
    wk = w[:, off_k:off_v]
    wv = w[:, off_v:off_g]
    wqs = wq[:, perm] * sign.astype(BF16)
    wks = wk[:, perm] * sign.astype(BF16)

    kc, vc = _ctx_kv(ctx.reshape(b * n_ctx, d), row(g_mix_pre), csh1, csc1, wk, wv)
    kc = kc.reshape(b, n_ctx, D_ATTN)
    vc = vc.reshape(b, n_ctx, D_ATTN)

    u, q, k, v, gate = _proj(x, row(g_mix_pre), sh1, sc1, jnp.asarray(cos), jnp.asarray(sin),
                             w[:, :off_q], wq * scale, wqs * scale, wk, wks, wv, w[:, off_g:])

    consts = tuple(jnp.asarray(a).astype(BF16) for a in _fft_constants(l))
    emb, width = hy_f_w1.shape[1], hy_f_w1.shape[2]
    zfeat, tcol, deltas = _filter_features(l, emb, LANES)
    hf = _filt(jnp.asarray(zfeat),
               _pad_to(hy_f_w1[0], LANES, LANES), _pad_to(row(hy_f_b1), 1, LANES),
               _pad_to(hy_f_w2[0], LANES, LANES), _pad_to(row(hy_f_b2), 1, LANES),
               _pad_to(row(hy_f_freq), 1, LANES, 1.0),
               _pad_to(hy_f_w3[0], LANES, hy_f_w3.shape[2]), row(hy_f_b3),
               jnp.asarray(tcol), jnp.asarray(deltas), consts[0], consts[2], order, l)

    zh = _hyena(u, hy_conv_w[0], row(hy_conv_b), hy_bias[0], hf, consts, order)

    lamv = jnp.concatenate([lambda_q1, lambda_k1, lambda_q2, lambda_k2], axis=0)
    oa = _attn(q, k, v, kc, vc, lamv, row(att_subln_g))

    return _post(x, zh, oa, gate, g1, sh2, sc2, g2, row(g_mix_post), row(g_ffn_pre), row(g_ffn_post),
                 w_hy_up[0].astype(BF16), w_att_up[0].astype(BF16), w_out[0].astype(BF16),
                 w_ffn_gate[0].astype(BF16), w_ffn_up[0].astype(BF16), w_ffn_down[0].astype(BF16))
```

```python
import functools
import math

import numpy as np
import jax
import jax.numpy as jnp
from jax import lax
from jax.experimental import pallas as pl
from jax.experimental.pallas import tpu as pltpu

F32 = jnp.float32
BF16 = jnp.bfloat16
HIGHEST = lax.Precision.HIGHEST

EPS = 1e-6
GRID_W = 64
HEAD_DIM = 64
N_HEADS = 4
D_HYENA = 512
D_ATTN = N_HEADS * 2 * HEAD_DIM
ROPE_BASE = 10000.0
FILTER_TARGET = 1e-2
FAST_DECAY_PCT = 0.3
SLOW_DECAY_PCT = 1.5
LAM_INIT = 0.8 - 0.6 * math.exp(0.0)

LANES = 128
MXU_DIM = 256
VMEM_LIMIT_BYTES = 56 * 1024 * 1024

FFT_N2 = 256
FFT_GROUP = 16
HY_CB = 256


def _dot(a, b):
    return jnp.dot(a, b, preferred_element_type=F32)


def _dot_hi(a, b):
    return jnp.dot(a, b, preferred_element_type=F32, precision=HIGHEST)


def _dot_nt(a, b):
    return lax.dot_general(a, b, (((1,), (1,)), ((), ())), preferred_element_type=F32)


def _rms(x, g):
    return x * lax.rsqrt(jnp.mean(x * x, axis=-1, keepdims=True) + EPS) * g


def _const_spec(shape):
    zeros = (0,) * len(shape)
    return pl.BlockSpec(shape, lambda *_: zeros, pipeline_mode=pl.Buffered(1))


def _params(*semantics):
    return pltpu.CompilerParams(dimension_semantics=semantics, vmem_limit_bytes=VMEM_LIMIT_BYTES)


def _ada_body(c_ref, w_ref, b_ref, o_ref):
    c = c_ref[...]
    o_ref[...] = _dot_hi(c * jax.nn.sigmoid(c), w_ref[...]) + b_ref[...]


def _ada(cc, w_ada, b_ada):
    rows, d = cc.shape
    n = w_ada.shape[1]
    tn = 1024
    return pl.pallas_call(
        _ada_body,
        grid=(n // tn,),
        in_specs=[
            pl.BlockSpec((rows, d), lambda j: (0, 0)),
            pl.BlockSpec((d, tn), lambda j: (0, j)),
            pl.BlockSpec((1, tn), lambda j: (0, j)),
        ],
        out_specs=pl.BlockSpec((rows, tn), lambda j: (0, j)),
        out_shape=jax.ShapeDtypeStruct((rows, n), F32),
        compiler_params=_params("arbitrary"),
        name="ada",
    )(cc, w_ada, b_ada)


def _ctx_kv_body(x_ref, g_ref, sh_ref, sc_ref, wk_ref, wvt_ref, k_ref, vt_ref):
    h = _rms(x_ref[0], g_ref[...]) * (1.0 + sc_ref[...]) + sh_ref[...]
    hb = h.astype(BF16)
    k_ref[0] = _dot(hb, wk_ref[...]).astype(BF16)
    vt_ref[0] = _dot_nt(wvt_ref[...], hb).astype(BF16)


def _ctx_kv(ctx, g, sh, sc, wk, wvt):
    b, n_ctx, d = ctx.shape
    n = wk.shape[1]
    return pl.pallas_call(
        _ctx_kv_body,
        grid=(b,),
        in_specs=[
            pl.BlockSpec((1, n_ctx, d), lambda i: (i, 0, 0)),
            _const_spec((1, d)), _const_spec((1, d)), _const_spec((1, d)),
            _const_spec((d, n)), _const_spec((n, d)),
        ],
        out_specs=[pl.BlockSpec((1, n_ctx, n), lambda i: (i, 0, 0)),
                   pl.BlockSpec((1, n, n_ctx), lambda i: (i, 0, 0))],
        out_shape=[jax.ShapeDtypeStruct((b, n_ctx, n), BF16), jax.ShapeDtypeStruct((b, n, n_ctx), BF16)],
        compiler_params=_params("arbitrary"),
        name="ctx_kv",
    )(ctx, g, sh, sc, wk, wvt)


def _proj_body(x_ref, g_ref, sh_ref, sc_ref, cos_ref, sin_ref, cost_ref, sint_ref,
               wu_ref, wqt_ref, wqst_ref, wk_ref, wks_ref, wvt_ref, wg_ref,
               u_ref, qt_ref, k_ref, vt_ref, gate_ref):
    h = _rms(x_ref[0], g_ref[...]) * (1.0 + sc_ref[0]) + sh_ref[0]
    hb = h.astype(BF16)
    u_ref[0] = _dot(hb, wu_ref[...]).astype(BF16)
    qt_ref[0] = (_dot_nt(wqt_ref[...], hb) * cost_ref[...]
                 + _dot_nt(wqst_ref[...], hb) * sint_ref[...]).astype(BF16)
    k_ref[0] = (_dot(hb, wk_ref[...]) * cos_ref[...] + _dot(hb, wks_ref[...]) * sin_ref[...]).astype(BF16)
    vt_ref[0] = _dot_nt(wvt_ref[...], hb).astype(BF16)
    gate_ref[0] = jax.nn.sigmoid(_dot(hb, wg_ref[...])).astype(BF16)


def _proj(x, g, sh, sc, cos, sin, cost, sint, wu, wqt, wqst, wk, wks, wvt, wg):
    b, l, d = x.shape
    tm = 512
    tile = lambda i, bi: (bi, i, 0)
    tile_t = lambda i, bi: (bi, 0, i)
    per_b = lambda i, bi: (bi, 0, 0)
    bf = lambda shape: jax.ShapeDtypeStruct(shape, BF16)
    n_u, n_g = wu.shape[1], wg.shape[1]
    return pl.pallas_call(
        _proj_body,
        grid=(l // tm, b),
        in_specs=[
            pl.BlockSpec((1, tm, d), tile),
            _const_spec((1, d)),
            pl.BlockSpec((1, 1, d), per_b), pl.BlockSpec((1, 1, d), per_b),
            pl.BlockSpec((tm, D_ATTN), lambda i, bi: (i, 0)), pl.BlockSpec((tm, D_ATTN), lambda i, bi: (i, 0)),
            pl.BlockSpec((D_ATTN, tm), lambda i, bi: (0, i)), pl.BlockSpec((D_ATTN, tm), lambda i, bi: (0, i)),
        ] + [_const_spec(w.shape) for w in (wu, wqt, wqst, wk, wks, wvt, wg)],
        out_specs=[pl.BlockSpec((1, tm, n_u), tile), pl.BlockSpec((1, D_ATTN, tm), tile_t),
                   pl.BlockSpec((1, tm, D_ATTN), tile), pl.BlockSpec((1, D_ATTN, tm), tile_t),
                   pl.BlockSpec((1, tm, n_g), tile)],
        out_shape=[bf((b, l, n_u)), bf((b, D_ATTN, l)), bf((b, l, D_ATTN)), bf((b, D_ATTN, l)), bf((b, l, n_g))],
        compiler_params=_params("arbitrary", "arbitrary"),
        name="proj",
    )(x, g, sh, sc, cos, sin, cost, sint, wu, wqt, wqst, wk, wks, wvt, wg)


def _fft_dims(l):
    n = 2 * l
    n1 = n // FFT_N2
    h1 = n1 // 2
    k1 = h1 + 1
    ng = FFT_N2 // FFT_GROUP
    return n, n1, h1, k1, ng


@functools.lru_cache(maxsize=None)
def _fft_constants(l):
    n, n1, h1, k1n, ng = _fft_dims(l)
    g = FFT_GROUP
    ma = np.zeros((ng, k1n, 2, g, h1, g), np.float64)
    for gi in range(ng):
        for j in range(g):
            n2 = gi * g + j
            nn = FFT_N2 * np.arange(h1)[None, :] + n2
            theta = 2.0 * np.pi * ((np.arange(k1n)[:, None] * nn) % n) / n
            ma[gi, :, 0, j, :, j] = np.cos(theta)
            ma[gi, :, 1, j, :, j] = -np.sin(theta)
    weight = np.full((k1n,), 2.0)
    weight[0] = weight[-1] = 1.0
    ma_fwd = ma.reshape(ng, k1n * 2 * g, h1 * g)
    ma_inv = (ma * (weight / n)[None, :, None, None, None, None]).reshape(ng, k1n * 2 * g, h1 * g)
    ma_inv = np.transpose(ma_inv, (0, 2, 1))
    kpad = -(-ma_inv.shape[2] // LANES) * LANES
    ma_inv = np.pad(ma_inv, ((0, 0), (0, 0), (0, kpad - ma_inv.shape[2])))
    ang = 2.0 * np.pi * ((np.arange(FFT_N2)[:, None] * np.arange(FFT_N2)[None, :]) % FFT_N2) / FFT_N2
    fr, fi = np.cos(ang), -np.sin(ang)
    mb_fwd = np.block([[fr, -fi], [fi, fr]])
    mb_inv = np.block([[fr, fi], [-fi, fr]])
    mb_filt = np.block([[fr, -fi, fr, -fi], [fi, fr, -fi, -fr]])
    return tuple(np.asarray(a, np.float32) for a in (ma_fwd, ma_inv, mb_fwd, mb_inv, mb_filt))


def _fft_stage_a(src_ref, ma_ref, spec_ref, l):
    _, _, h1, k1n, ng = _fft_dims(l)
    g = FFT_GROUP

    def group(gi, carry):
        off = gi * g
        xg = jnp.concatenate(
            [src_ref[pl.ds(pl.multiple_of(FFT_N2 * i + off, g), g), :] for i in range(h1)], axis=0)
        bg = _dot(ma_ref[gi], xg.astype(BF16)).astype(BF16)
        for c in range(2 * k1n):
            spec_ref[pl.ds(pl.multiple_of(c * FFT_N2 + off, g), g), :] = bg[c * g:(c + 1) * g]
        return carry

    lax.fori_loop(0, ng, group, 0)


def _fft_stage_a_inv(spec_ref, ma_inv_ref, dst_ref, l):
    _, _, h1, k1n, ng = _fft_dims(l)
    g = FFT_GROUP
    kpad = ma_inv_ref.shape[2] - 2 * k1n * g

    def group(gi, carry):
        off = gi * g
        chunks = [spec_ref[pl.ds(pl.multiple_of(c * FFT_N2 + off, g), g), :] for c in range(2 * k1n)]
        if kpad:
            chunks.append(jnp.zeros((kpad, spec_ref.shape[1]), BF16))
        yg = _dot(ma_inv_ref[gi], jnp.concatenate(chunks, axis=0))
        for i in range(h1):
            dst_ref[pl.ds(pl.multiple_of(FFT_N2 * i + off, g), g), :] = yg[i * g:(i + 1) * g]
        return carry

    lax.fori_loop(0, ng, group, 0)


def _filt_body(z_ref, w1_ref, b1_ref, w2_ref, b2_ref, fq_ref, w3f_ref, b3f_ref, w3b_ref, b3b_ref,
               dl_ref, ma_ref, mb_ref, hf_ref, hdn_ref, sig_ref, specf_ref, specb_ref, *, l):
    _, _, _, k1n, _ = _fft_dims(l)

    @pl.when((pl.program_id(0) == 0) & (pl.program_id(1) == 0))
    def _():
        fq = fq_ref[...]
        h1 = jnp.sin(fq * (_dot_hi(z_ref[...], w1_ref[...]) + b1_ref[...]))
        hdn_ref[...] = jnp.sin(fq * (_dot_hi(h1, w2_ref[...]) + b2_ref[...]))

    hdn = hdn_ref[...]
    cb = specf_ref.shape[1]
    blk = 2 * FFT_N2
    rows = lax.broadcasted_iota(jnp.int32, (l, cb), 0)
    window = jnp.exp(rows.astype(F32) * (-1.0 / (l - 1)) * dl_ref[...])

    sig_ref[...] = (_dot_hi(hdn, w3f_ref[...]) + b3f_ref[...]) * window
    _fft_stage_a(sig_ref, ma_ref, specf_ref, l)
    sig_ref[...] = jnp.where(rows > 0, (_dot_hi(hdn, w3b_ref[...]) + b3b_ref[...]) * window, 0.0)
    _fft_stage_a(sig_ref, ma_ref, specb_ref, l)

    def blocks(starts):
        x = jnp.concatenate(
            [jnp.concatenate([specf_ref[pl.ds(st, blk), :], specb_ref[pl.ds(st, blk), :]], axis=0)
             for st in starts], axis=1)
        s = _dot(mb_ref[...], x).astype(BF16)
        for n, st in enumerate(starts):
            hf_ref[0, pl.ds(st, blk), :] = s[:, n * cb:(n + 1) * cb]

    def per_pair(i, carry):
        st = pl.multiple_of(i * 2 * blk, blk)
        blocks([st, st + blk])
        return carry

    lax.fori_loop(0, k1n // 2, per_pair, 0)
    if k1n % 2:
        blocks([(k1n - 1) * blk])


def _filt(zfeat, w1, b1, w2, b2, fq, w3, b3, deltas, ma_fwd, mb_filt, order, l):
    _, _, _, k1n, _ = _fft_dims(l)
    cb = HY_CB
    nblk = D_HYENA // cb
    srows = k1n * 2 * FFT_N2
    fcol = lambda o, c: (0, (2 * o) * nblk + c)
    bcol = lambda o, c: (0, (2 * o + 1) * nblk + c)
    return pl.pallas_call(
        functools.partial(_filt_body, l=l),
        grid=(order, nblk),
        in_specs=[
            _const_spec(zfeat.shape), _const_spec(w1.shape), _const_spec(b1.shape),
            _const_spec(w2.shape), _const_spec(b2.shape), _const_spec(fq.shape),
            pl.BlockSpec((w3.shape[0], cb), fcol), pl.BlockSpec((1, cb), fcol),
            pl.BlockSpec((w3.shape[0], cb), bcol), pl.BlockSpec((1, cb), bcol),
            pl.BlockSpec((1, cb), lambda o, c: (0, c)),
            _const_spec(ma_fwd.shape), _const_spec(mb_filt.shape),
        ],
        out_specs=pl.BlockSpec((1, srows, cb), lambda o, c: (o, 0, c)),
        out_shape=jax.ShapeDtypeStruct((order, srows, D_HYENA), BF16),
        scratch_shapes=[
            pltpu.VMEM((l, w2.shape[1]), F32),
            pltpu.VMEM((l, cb), F32),
            pltpu.VMEM((srows, cb), BF16),
            pltpu.VMEM((srows, cb), BF16),
        ],
        compiler_params=_params("arbitrary", "arbitrary"),
        name="filt",
    )(zfeat, w1, b1, w2, b2, fq, w3, b3, w3, b3, deltas, ma_fwd, mb_filt)


CONV_ROWS = 512
HALO = 16


def _short_conv_chunk(u_ref, w_ref, b_ref, r0, l):
    c = u_ref.shape[2]
    zeros = jnp.zeros((HALO, c), F32)
    before = zeros if r0 == 0 else u_ref[0, r0 - HALO:r0, :].astype(F32)
    after = zeros if r0 + CONV_ROWS == l else u_ref[0, r0 + CONV_ROWS:r0 + CONV_ROWS + HALO, :].astype(F32)
    ext = jnp.concatenate([before, u_ref[0, r0:r0 + CONV_ROWS, :].astype(F32), after], axis=0)
    w = w_ref[...]
    return (ext[HALO - 1:HALO - 1 + CONV_ROWS] * w[0:1]
            + ext[HALO:HALO + CONV_ROWS] * w[1:2]
            + ext[HALO + 1:HALO + 1 + CONV_ROWS] * w[2:3]
            + b_ref[...])


def _hyena_body(u0_ref, u1_ref, u2_ref, w0_ref, w1_ref, w2_ref, b0_ref, b1_ref, b2_ref, db_ref, hf_ref,
                ma_ref, mai_ref, mb_ref, mbi_ref, o_ref, z_ref, y_ref, spec_ref, *, l, order):
    _, _, _, k1n, _ = _fft_dims(l)
    blk = 2 * FFT_N2
    gates = ((u1_ref, w1_ref, b1_ref), (u2_ref, w2_ref, b2_ref))

    for r0 in range(0, l, CONV_ROWS):
        z_ref[r0:r0 + CONV_ROWS, :] = _short_conv_chunk(u0_ref, w0_ref, b0_ref, r0, l)

    for o in range(order):
        _fft_stage_a(z_ref, ma_ref, spec_ref, l)

        def filter_blocks(starts, o=o):
            cb = spec_ref.shape[1]
            x = jnp.concatenate([spec_ref[pl.ds(st, blk), :] for st in starts], axis=1)
            s = _dot(mb_ref[...], x)
            h = jnp.concatenate([hf_ref[o, pl.ds(st, blk), :] for st in starts], axis=1).astype(F32)
            sr, si, hr, hi = s[:FFT_N2], s[FFT_N2:], h[:FFT_N2], h[FFT_N2:]
            y = jnp.concatenate([sr * hr - si * hi, sr * hi + si * hr], axis=0).astype(BF16)
            c = _dot(mbi_ref[...], y).astype(BF16)
            for n, st in enumerate(starts):
                spec_ref[pl.ds(st, blk), :] = c[:, n * cb:(n + 1) * cb]

        def per_pair(i, carry, filter_blocks=filter_blocks):
            st = pl.multiple_of(i * 2 * blk, blk)
            filter_blocks([st, st + blk])
            return carry

        lax.fori_loop(0, k1n // 2, per_pair, 0)
        if k1n % 2:
            filter_blocks([(k1n - 1) * blk])
        _fft_stage_a_inv(spec_ref, mai_ref, y_ref, l)

        u_ref, w_ref, b_ref = gates[o]
        dbias = db_ref[o:o + 1, :]
        for r0 in range(0, l, CONV_ROWS):
            rows = slice(r0, r0 + CONV_ROWS)
            znew = _short_conv_chunk(u_ref, w_ref, b_ref, r0, l) * (y_ref[rows, :] + z_ref[rows, :] * dbias)
            if o + 1 < order:
                z_ref[rows, :] = znew
            else:
                o_ref[0, rows, :] = znew.astype(BF16)


def _hyena(u, conv_w, conv_b, dbias, hf, consts, order):
    b, l, _ = u.shape
    _, _, _, k1n, _ = _fft_dims(l)
    cb = HY_CB
    nblk = D_HYENA // cb
    srows = k1n * 2 * FFT_N2
    ma_fwd, ma_inv, mb_fwd, mb_inv = consts
    upart = lambda p: pl.BlockSpec((1, l, cb), lambda c, bi: (bi, 0, p * nblk + c))
    wpart = lambda p: pl.BlockSpec((3, cb), lambda c, bi: (0, p * nblk + c))
    bpart = lambda p: pl.BlockSpec((1, cb), lambda c, bi: (0, p * nblk + c))
    return pl.pallas_call(
        functools.partial(_hyena_body, l=l, order=order),
        grid=(nblk, b),
        in_specs=[
            upart(0), upart(1), upart(2), wpart(0), wpart(1), wpart(2), bpart(0), bpart(1), bpart(2),
            pl.BlockSpec((order, cb), lambda c, bi: (0, c)),
            pl.BlockSpec((order, srows, cb), lambda c, bi: (0, 0, c), pipeline_mode=pl.Buffered(1)),
            _const_spec(ma_fwd.shape), _const_spec(ma_inv.shape),
            _const_spec(mb_fwd.shape), _const_spec(mb_inv.shape),
        ],
        out_specs=pl.BlockSpec((1, l, cb), lambda c, bi: (bi, 0, c)),
        out_shape=jax.ShapeDtypeStruct((b, l, D_HYENA), BF16),
        scratch_shapes=[
            pltpu.VMEM((l, cb), F32),
            pltpu.VMEM((l, cb), F32),
            pltpu.VMEM((srows, cb), BF16),
        ],
        compiler_params=_params("arbitrary", "arbitrary"),
        name="hyena",
    )(u, u, u, conv_w, conv_w, conv_w, conv_b, conv_b, conv_b, dbias, hf, ma_fwd, ma_inv, mb_fwd, mb_inv)


ATT_TQ = MXU_DIM


def _attn_body(qt_ref, k_ref, vt_ref, kc_ref, vct_ref, lam_ref, sg_ref, o_ref):
    tq = qt_ref.shape[2]
    lv = lam_ref[...]
    lam = (jnp.exp(jnp.sum(lv[0:1] * lv[1:2], axis=-1, keepdims=True))
           - jnp.exp(jnp.sum(lv[2:3] * lv[3:4], axis=-1, keepdims=True)) + LAM_INIT)
    qt = qt_ref[0]
    feat = lax.broadcasted_iota(jnp.int32, qt.shape, 0)
    qz = jnp.concatenate([jnp.where(feat < HEAD_DIM, qt, jnp.zeros_like(qt)),
                          jnp.where(feat >= HEAD_DIM, qt, jnp.zeros_like(qt))], axis=1)
    s_l = _dot(k_ref[0], qz)
    s_c = _dot(kc_ref[0], qz)
    m = jnp.maximum(jnp.max(s_l, axis=0, keepdims=True), jnp.max(s_c, axis=0, keepdims=True))
    p_l = jnp.exp2(s_l - m)
    p_c = jnp.exp2(s_c - m)
    r = 1.0 / (jnp.sum(p_l, axis=0, keepdims=True) + jnp.sum(p_c, axis=0, keepdims=True))
    acc = _dot(vt_ref[0], p_l.astype(BF16)) + _dot(vct_ref[0], p_c.astype(BF16))
    ot = acc[:, :tq] * r[:, :tq] - acc[:, tq:] * (lam * r[:, tq:])
    o_ref[0] = (_rms(ot.T, sg_ref[...]) * (1.0 - LAM_INIT)).astype(BF16)


def _attn(qt, k, vt, kc, vct, lamv, subln_g):
    b, l, _ = k.shape
    c = kc.shape[1]
    dv = 2 * HEAD_DIM
    tq = ATT_TQ
    assert l % tq == 0
    return pl.pallas_call(
        _attn_body,
        grid=(b, N_HEADS, l // tq),
        in_specs=[
            pl.BlockSpec((1, dv, tq), lambda bi, h, i: (bi, h, i)),
            pl.BlockSpec((1, l, dv), lambda bi, h, i: (bi, 0, h)),
            pl.BlockSpec((1, dv, l), lambda bi, h, i: (bi, h, 0)),
            pl.BlockSpec((1, c, dv), lambda bi, h, i: (bi, 0, h)),
            pl.BlockSpec((1, dv, c), lambda bi, h, i: (bi, h, 0)),
            _const_spec(lamv.shape), _const_spec(subln_g.shape),
        ],
        out_specs=pl.BlockSpec((1, tq, dv), lambda bi, h, i: (bi, i, h)),
        out_shape=jax.ShapeDtypeStruct((b, l, N_HEADS * dv), BF16),
        compiler_params=_params("arbitrary", "arbitrary", "arbitrary"),
        name="attn",
    )(qt, k, vt, kc, vct, lamv, subln_g)


def _post_body(x_ref, zh_ref, oa_ref, gate_ref, g1_ref, sh2_ref, sc2_ref, g2_ref,
               gpost_ref, gfpre_ref, gfpost_ref,
               whu_ref, wau_ref, wo_ref, wg_ref, wu_ref, wd_ref, o_ref, *, ff_chunks):
    d = x_ref.shape[2]
    y_hy = _dot(zh_ref[0], whu_ref[...])
    y_att = _dot(oa_ref[0], wau_ref[...])
    mixed = gate_ref[0, :, :d].astype(F32) * y_hy + gate_ref[0, :, d:].astype(F32) * y_att
    mixed = _dot(mixed.astype(BF16), wo_ref[...])
    x1 = x_ref[0] + g1_ref[0] * _rms(mixed, gpost_ref[...])
    hf = (_rms(x1, gfpre_ref[...]) * (1.0 + sc2_ref[0]) + sh2_ref[0]).astype(BF16)
    f = None
    for c0, cw in ff_chunks:
        a = _dot(hf, wg_ref[:, c0:c0 + cw])
        t = (a * jax.nn.sigmoid(a) * _dot(hf, wu_ref[:, c0:c0 + cw])).astype(BF16)
        part = _dot(t, wd_ref[c0:c0 + cw, :])
        f = part if f is None else f + part
    o_ref[0] = x1 + g2_ref[0] * _rms(f, gfpost_ref[...])


def _post(x, zh, oa, gate, g1, sh2, sc2, g2, gpost, gfpre, gfpost, whu, wau, wo, wg, wu, wd):
    b, l, d = x.shape
    dff = wg.shape[1]
    tm = 512
    ff_chunk = 4 * MXU_DIM
    ff_chunks = tuple((c0, min(ff_chunk, dff - c0)) for c0 in range(0, dff, ff_chunk))
    tile = lambda bi, i: (bi, i, 0)
    per_b = lambda bi, i: (bi, 0, 0)
    return pl.pallas_call(
        functools.partial(_post_body, ff_chunks=ff_chunks),
        grid=(b, l // tm),
        in_specs=[
            pl.BlockSpec((1, tm, d), tile),
            pl.BlockSpec((1, tm, zh.shape[2]), tile),
            pl.BlockSpec((1, tm, oa.shape[2]), tile),
            pl.BlockSpec((1, tm, gate.shape[2]), tile),
            pl.BlockSpec((1, 1, d), per_b), pl.BlockSpec((1, 1, d), per_b),
            pl.BlockSpec((1, 1, d), per_b), pl.BlockSpec((1, 1, d), per_b),
            _const_spec((1, d)), _const_spec((1, d)), _const_spec((1, d)),
        ] + [_const_spec(w.shape) for w in (whu, wau, wo, wg, wu, wd)],
        out_specs=pl.BlockSpec((1, tm, d), tile),
        out_shape=jax.ShapeDtypeStruct((b, l, d), F32),
        compiler_params=_params("arbitrary", "arbitrary"),
        name="post",
    )(x, zh, oa, gate, g1, sh2, sc2, g2, gpost, gfpre, gfpost, whu, wau, wo, wg, wu, wd)


@functools.lru_cache(maxsize=None)
def _rope_tables(l):
    quarter = HEAD_DIM // 4
    inv = ROPE_BASE ** (-np.arange(quarter, dtype=np.float64) / quarter)
    t = np.arange(l)
    ang_r = (t // GRID_W)[:, None] * inv[None, :]
    ang_c = (t % GRID_W)[:, None] * inv[None, :]
    ang = np.concatenate([ang_r, ang_r, ang_c, ang_c], axis=1)
    reps = D_ATTN // HEAD_DIM
    cos = np.tile(np.cos(ang), (1, reps)).astype(np.float32)
    sin = np.tile(np.sin(ang), (1, reps)).astype(np.float32)
    j = np.arange(D_ATTN)
    second = (j % (2 * quarter)) >= quarter
    perm = np.where(second, j - quarter, j + quarter)
    sign = np.where(second, 1.0, -1.0).astype(np.float32)
    return cos, sin, perm, sign


@functools.lru_cache(maxsize=None)
def _filter_features(l, emb, kpad):
    bands = (emb - 1) // 2
    t = np.linspace(0.0, 1.0, l, dtype=np.float32).astype(np.float64)[:, None]
    w = (2.0 * math.pi / l) * np.arange(l, dtype=np.float64)[:, None]
    fr = np.linspace(1e-4, bands - 1, bands, dtype=np.float32).astype(np.float64)[None, :]
    z = np.concatenate([t, np.cos(fr * w), -np.sin(fr * w)], axis=-1)
    z = np.pad(z, ((0, 0), (0, kpad - emb))).astype(np.float32)
    deltas = np.abs(np.linspace(math.log(FILTER_TARGET) / SLOW_DECAY_PCT,
                                math.log(FILTER_TARGET) / FAST_DECAY_PCT, D_HYENA, dtype=np.float32))
    return z, deltas[None, :].astype(np.float32)


def _pad_to(a, rows, cols, value=0.0):
    return jnp.pad(a, ((0, rows - a.shape[0]), (0, cols - a.shape[1])), constant_values=value)


def kernel(x, c, ctx, c_ctx, w_ada, b_ada, g_mix_pre, g_mix_post, g_ffn_pre, g_ffn_post, w_in, hy_conv_w, hy_conv_b, hy_f_w1, hy_f_b1, hy_f_w2, hy_f_b2, hy_f_w3, hy_f_b3, hy_f_freq, hy_bias, lambda_q1, lambda_k1, lambda_q2, lambda_k2, att_subln_g, w_hy_up, w_att_up, w_out, w_ffn_gate, w_ffn_up, w_ffn_down):
    b, l, d = x.shape
    n_ctx = ctx.shape[1]
    order = hy_bias.shape[1]
    off_q = (order + 1) * D_HYENA
    off_k = off_q + D_ATTN
    off_v = off_k + D_ATTN
    off_g = off_v + D_ATTN
    assert w_in.shape[0] == 1, "single-layer block"
    assert l % CONV_ROWS == 0 and l % 512 == 0

    pad_rows = -(-(b + 1) // 8) * 8
    cc = jnp.concatenate([c, c_ctx[None, :], jnp.zeros((pad_rows - b - 1, d), F32)], axis=0)
    ada = _ada(cc, w_ada[0], b_ada[0][None, :])
    sh1, sc1, g1, sh2, sc2, g2 = (ada[:b, i * d:(i + 1) * d][:, None, :] for i in range(6))
    csh1, csc1 = ada[b:b + 1, 0:d], ada[b:b + 1, d:2 * d]

    row = lambda v: v[0][None, :]
    w = w_in[0].astype(BF16)
    cos, sin, perm, sign = _rope_tables(l)
    q_scale = HEAD_DIM ** -0.5 * math.log2(math.e)
    wq = (w_in[0][:, off_q:off_k] * q_scale).astype(BF16)
    wk = w[:, off_k:off_v]
    wvt = w[:, off_v:off_g].T
    wqst = (wq[:, perm] * sign.astype(BF16)).T
    wks = wk[:, perm] * sign.astype(BF16)

    kc, vct = _ctx_kv(ctx, row(g_mix_pre), csh1, csc1, wk, wvt)

    u, qt, k, vt, gate = _proj(x, row(g_mix_pre), sh1, sc1, jnp.asarray(cos), jnp.asarray(sin),
                               jnp.asarray(cos.T), jnp.asarray(sin.T),
                               w[:, :off_q], wq.T, wqst, wk, wks, wvt, w[:, off_g:])

    consts = tuple(jnp.asarray(a).astype(BF16) for a in _fft_constants(l))
    zfeat, deltas = _filter_features(l, hy_f_w1.shape[1], LANES)
    hf = _filt(jnp.asarray(zfeat),
               _pad_to(hy_f_w1[0], LANES, LANES), _pad_to(row(hy_f_b1), 1, LANES),
               _pad_to(hy_f_w2[0], LANES, LANES), _pad_to(row(hy_f_b2), 1, LANES),
               _pad_to(row(hy_f_freq), 1, LANES, 1.0),
               _pad_to(hy_f_w3[0], LANES, hy_f_w3.shape[2]), row(hy_f_b3),
               jnp.asarray(deltas), consts[0], consts[4], order, l)

    zh = _hyena(u, hy_conv_w[0], row(hy_conv_b), hy_bias[0], hf, consts[:4], order)

    lamv = jnp.concatenate([lambda_q1, lambda_k1, lambda_q2, lambda_k2], axis=0)
    oa = _attn(qt, k, vt, kc, vct, lamv, row(att_subln_g))

    return _post(x, zh, oa, gate, g1, sh2, sc2, g2, row(g_mix_post), row(g_ffn_pre), row(g_ffn_post),
                 w_hy_up[0].astype(BF16), w_att_up[0].astype(BF16), w_out[0].astype(BF16),
                 w_ffn_gate[0].astype(BF16), w_ffn_up[0].astype(BF16), w_ffn_down[0].astype(BF16))
```

```python
import functools
import math

import numpy as np
import jax
import jax.numpy as jnp
from jax import lax
from jax.experimental import pallas as pl
from jax.experimental.pallas import tpu as pltpu

F32 = jnp.float32
BF16 = jnp.bfloat16
HIGHEST = lax.Precision.HIGHEST

EPS = 1e-6
GRID_W = 64
HEAD_DIM = 64
N_HEADS = 4
D_HYENA = 512
D_ATTN = N_HEADS * 2 * HEAD_DIM
ROPE_BASE = 10000.0
FILTER_TARGET = 1e-2
FAST_DECAY_PCT = 0.3
SLOW_DECAY_PCT = 1.5
LAM_INIT = 0.8 - 0.6 * math.exp(0.0)

LANES = 128
MXU_DIM = 256
VMEM_LIMIT_BYTES = 56 * 1024 * 1024

FFT_N2 = 256
FFT_GROUP = 16
HY_CB = 256


def _dot(a, b):
    return jnp.dot(a, b, preferred_element_type=F32)


def _dot_hi(a, b):
    return jnp.dot(a, b, preferred_element_type=F32, precision=HIGHEST)


def _dot_nt(a, b):
    return lax.dot_general(a, b, (((1,), (1,)), ((), ())), preferred_element_type=F32)


def _rms(x, g):
    return x * lax.rsqrt(jnp.mean(x * x, axis=-1, keepdims=True) + EPS) * g


def _const_spec(shape):
    zeros = (0,) * len(shape)
    return pl.BlockSpec(shape, lambda *_: zeros, pipeline_mode=pl.Buffered(1))


def _params(*semantics):
    return pltpu.CompilerParams(dimension_semantics=semantics, vmem_limit_bytes=VMEM_LIMIT_BYTES)


def _ada_body(c_ref, w_ref, b_ref, o_ref):
    c = c_ref[...]
    o_ref[...] = _dot_hi(c * jax.nn.sigmoid(c), w_ref[...]) + b_ref[...]


def _ada(cc, w_ada, b_ada):
    rows, d = cc.shape
    n = w_ada.shape[1]
    tn = 1024
    return pl.pallas_call(
        _ada_body,
        grid=(n // tn,),
        in_specs=[
            pl.BlockSpec((rows, d), lambda j: (0, 0)),
            pl.BlockSpec((d, tn), lambda j: (0, j)),
            pl.BlockSpec((1, tn), lambda j: (0, j)),
        ],
        out_specs=pl.BlockSpec((rows, tn), lambda j: (0, j)),
        out_shape=jax.ShapeDtypeStruct((rows, n), F32),
        compiler_params=_params("arbitrary"),
        name="ada",
    )(cc, w_ada, b_ada)


def _ctx_kv_body(x_ref, g_ref, sh_ref, sc_ref, wk_ref, wvt_ref, k_ref, vt_ref):
    h = _rms(x_ref[0], g_ref[...]) * (1.0 + sc_ref[...]) + sh_ref[...]
    hb = h.astype(BF16)
    k_ref[0] = _dot(hb, wk_ref[...]).astype(BF16)
    vt_ref[0] = _dot_nt(wvt_ref[...], hb).astype(BF16)


def _ctx_kv(ctx, g, sh, sc, wk, wvt):
    b, n_ctx, d = ctx.shape
    n = wk.shape[1]
    return pl.pallas_call(
        _ctx_kv_body,
        grid=(b,),
        in_specs=[
            pl.BlockSpec((1, n_ctx, d), lambda i: (i, 0, 0)),
            _const_spec((1, d)), _const_spec((1, d)), _const_spec((1, d)),
            _const_spec((d, n)), _const_spec((n, d)),
        ],
        out_specs=[pl.BlockSpec((1, n_ctx, n), lambda i: (i, 0, 0)),
                   pl.BlockSpec((1, n, n_ctx), lambda i: (i, 0, 0))],
        out_shape=[jax.ShapeDtypeStruct((b, n_ctx, n), BF16), jax.ShapeDtypeStruct((b, n, n_ctx), BF16)],
        compiler_params=_params("arbitrary"),
        name="ctx_kv",
    )(ctx, g, sh, sc, wk, wvt)


def _proj_body(x_ref, g_ref, sh_ref, sc_ref, cos_ref, sin_ref, cost_ref, sint_ref,
               wu_ref, wqt_ref, wqst_ref, wk_ref, wks_ref, wvt_ref, wg_ref,
               u_ref, qt_ref, k_ref, vt_ref, gate_ref):
    h = _rms(x_ref[0], g_ref[...]) * (1.0 + sc_ref[0]) + sh_ref[0]
    hb = h.astype(BF16)
    u_ref[0] = _dot(hb, wu_ref[...]).astype(BF16)
    qt_ref[0] = (_dot_nt(wqt_ref[...], hb) * cost_ref[...]
                 + _dot_nt(wqst_ref[...], hb) * sint_ref[...]).astype(BF16)
    k_ref[0] = (_dot(hb, wk_ref[...]) * cos_ref[...] + _dot(hb, wks_ref[...]) * sin_ref[...]).astype(BF16)
    vt_ref[0] = _dot_nt(wvt_ref[...], hb).astype(BF16)
    gate_ref[0] = jax.nn.sigmoid(_dot(hb, wg_ref[...])).astype(BF16)


def _proj(x, g, sh, sc, cos, sin, cost, sint, wu, wqt, wqst, wk, wks, wvt, wg):
    b, l, d = x.shape
    tm = 512
    tile = lambda i, bi: (bi, i, 0)
    tile_t = lambda i, bi: (bi, 0, i)
    per_b = lambda i, bi: (bi, 0, 0)
    bf = lambda shape: jax.ShapeDtypeStruct(shape, BF16)
    n_u, n_g = wu.shape[1], wg.shape[1]
    return pl.pallas_call(
        _proj_body,
        grid=(l // tm, b),
        in_specs=[
            pl.BlockSpec((1, tm, d), tile),
            _const_spec((1, d)),
            pl.BlockSpec((1, 1, d), per_b), pl.BlockSpec((1, 1, d), per_b),
            pl.BlockSpec((tm, D_ATTN), lambda i, bi: (i, 0)), pl.BlockSpec((tm, D_ATTN), lambda i, bi: (i, 0)),
            pl.BlockSpec((D_ATTN, tm), lambda i, bi: (0, i)), pl.BlockSpec((D_ATTN, tm), lambda i, bi: (0, i)),
        ] + [_const_spec(w.shape) for w in (wu, wqt, wqst, wk, wks, wvt, wg)],
        out_specs=[pl.BlockSpec((1, tm, n_u), tile), pl.BlockSpec((1, D_ATTN, tm), tile_t),
                   pl.BlockSpec((1, tm, D_ATTN), tile), pl.BlockSpec((1, D_ATTN, tm), tile_t),
                   pl.BlockSpec((1, tm, n_g), tile)],
        out_shape=[bf((b, l, n_u)), bf((b, D_ATTN, l)), bf((b, l, D_ATTN)), bf((b, D_ATTN, l)), bf((b, l, n_g))],
        compiler_params=_params("arbitrary", "arbitrary"),
        name="proj",
    )(x, g, sh, sc, cos, sin, cost, sint, wu, wqt, wqst, wk, wks, wvt, wg)


def _fft_dims(l):
    n = 2 * l
    n1 = n // FFT_N2
    h1 = n1 // 2
    k1 = h1 + 1
    ng = FFT_N2 // FFT_GROUP
    return n, n1, h1, k1, ng


@functools.lru_cache(maxsize=None)
def _fft_constants(l):
    n, n1, h1, k1n, ng = _fft_dims(l)
    g = FFT_GROUP
    ma = np.zeros((ng, k1n, 2, g, h1, g), np.float64)
    for gi in range(ng):
        for j in range(g):
            n2 = gi * g + j
            nn = FFT_N2 * np.arange(h1)[None, :] + n2
            theta = 2.0 * np.pi * ((np.arange(k1n)[:, None] * nn) % n) / n
            ma[gi, :, 0, j, :, j] = np.cos(theta)
            ma[gi, :, 1, j, :, j] = -np.sin(theta)
    weight = np.full((k1n,), 2.0)
    weight[0] = weight[-1] = 1.0
    ma_fwd = ma.reshape(ng, k1n * 2 * g, h1 * g)
    ma_inv = (ma * (weight / n)[None, :, None, None, None, None]).reshape(ng, k1n * 2 * g, h1 * g)
    ma_inv = np.transpose(ma_inv, (0, 2, 1))
    kpad = -(-ma_inv.shape[2] // LANES) * LANES
    ma_inv = np.pad(ma_inv, ((0, 0), (0, 0), (0, kpad - ma_inv.shape[2])))
    ang = 2.0 * np.pi * ((np.arange(FFT_N2)[:, None] * np.arange(FFT_N2)[None, :]) % FFT_N2) / FFT_N2
    fr, fi = np.cos(ang), -np.sin(ang)
    mb_fwd = np.block([[fr, -fi], [fi, fr]])
    mb_inv = np.block([[fr, fi], [-fi, fr]])
    mb_filt = np.block([[fr, -fi, fr, -fi], [fi, fr, -fi, -fr]])
    return tuple(np.asarray(a, np.float32) for a in (ma_fwd, ma_inv, mb_fwd, mb_inv, mb_filt))


def _fft_stage_a(src_ref, ma_ref, spec_ref, l):
    _, _, h1, k1n, ng = _fft_dims(l)
    g = FFT_GROUP

    def group(gi, carry):
        off = gi * g
        xg = jnp.concatenate(
            [src_ref[pl.ds(pl.multiple_of(FFT_N2 * i + off, g), g), :] for i in range(h1)], axis=0)
        bg = _dot(ma_ref[gi], xg.astype(BF16)).astype(BF16)
        for c in range(2 * k1n):
            spec_ref[pl.ds(pl.multiple_of(c * FFT_N2 + off, g), g), :] = bg[c * g:(c + 1) * g]
        return carry

    lax.fori_loop(0, ng, group, 0)


def _fft_stage_a_inv(spec_ref, ma_inv_ref, dst_ref, l):
    _, _, h1, k1n, ng = _fft_dims(l)
    g = FFT_GROUP
    kpad = ma_inv_ref.shape[2] - 2 * k1n * g

    def group(gi, carry):
        off = gi * g
        chunks = [spec_ref[pl.ds(pl.multiple_of(c * FFT_N2 + off, g), g), :] for c in range(2 * k1n)]
        if kpad:
            chunks.append(jnp.zeros((kpad, spec_ref.shape[1]), BF16))
        yg = _dot(ma_inv_ref[gi], jnp.concatenate(chunks, axis=0))
        for i in range(h1):
            dst_ref[pl.ds(pl.multiple_of(FFT_N2 * i + off, g), g), :] = yg[i * g:(i + 1) * g]
        return carry

    lax.fori_loop(0, ng, group, 0)


def _filt_body(z_ref, w1_ref, b1_ref, w2_ref, b2_ref, fq_ref, w3f_ref, b3f_ref, w3b_ref, b3b_ref,
               dl_ref, ma_ref, mb_ref, hf_ref, hdn_ref, sig_ref, specf_ref, specb_ref, *, l):
    _, _, _, k1n, _ = _fft_dims(l)

    @pl.when((pl.program_id(0) == 0) & (pl.program_id(1) == 0))
    def _():
        fq = fq_ref[...]
        h1 = jnp.sin(fq * (_dot_hi(z_ref[...], w1_ref[...]) + b1_ref[...]))
        hdn_ref[...] = jnp.sin(fq * (_dot_hi(h1, w2_ref[...]) + b2_ref[...]))

    hdn = hdn_ref[...]
    cb = specf_ref.shape[1]
    blk = 2 * FFT_N2
    rows = lax.broadcasted_iota(jnp.int32, (l, cb), 0)
    window = jnp.exp(rows.astype(F32) * (-1.0 / (l - 1)) * dl_ref[...])

    sig_ref[...] = (_dot_hi(hdn, w3f_ref[...]) + b3f_ref[...]) * window
    _fft_stage_a(sig_ref, ma_ref, specf_ref, l)
    sig_ref[...] = jnp.where(rows > 0, (_dot_hi(hdn, w3b_ref[...]) + b3b_ref[...]) * window, 0.0)
    _fft_stage_a(sig_ref, ma_ref, specb_ref, l)

    def blocks(starts):
        x = jnp.concatenate(
            [jnp.concatenate([specf_ref[pl.ds(st, blk), :], specb_ref[pl.ds(st, blk), :]], axis=0)
             for st in starts], axis=1)
        s = _dot(mb_ref[...], x).astype(BF16)
        for n, st in enumerate(starts):
            hf_ref[0, pl.ds(st, blk), :] = s[:, n * cb:(n + 1) * cb]

    def per_pair(i, carry):
        st = pl.multiple_of(i * 2 * blk, blk)
        blocks([st, st + blk])
        return carry

    lax.fori_loop(0, k1n // 2, per_pair, 0)
    if k1n % 2:
        blocks([(k1n - 1) * blk])


def _filt(zfeat, w1, b1, w2, b2, fq, w3, b3, deltas, ma_fwd, mb_filt, order, l):
    _, _, _, k1n, _ = _fft_dims(l)
    cb = HY_CB
    nblk = D_HYENA // cb
    srows = k1n * 2 * FFT_N2
    fcol = lambda o, c: (0, (2 * o) * nblk + c)
    bcol = lambda o, c: (0, (2 * o + 1) * nblk + c)
    return pl.pallas_call(
        functools.partial(_filt_body, l=l),
        grid=(order, nblk),
        in_specs=[
            _const_spec(zfeat.shape), _const_spec(w1.shape), _const_spec(b1.shape),
            _const_spec(w2.shape), _const_spec(b2.shape), _const_spec(fq.shape),
            pl.BlockSpec((w3.shape[0], cb), fcol), pl.BlockSpec((1, cb), fcol),
            pl.BlockSpec((w3.shape[0], cb), bcol), pl.BlockSpec((1, cb), bcol),
            pl.BlockSpec((1, cb), lambda o, c: (0, c)),
            _const_spec(ma_fwd.shape), _const_spec(mb_filt.shape),
        ],
        out_specs=pl.BlockSpec((1, srows, cb), lambda o, c: (o, 0, c)),
        out_shape=jax.ShapeDtypeStruct((order, srows, D_HYENA), BF16),
        scratch_shapes=[
            pltpu.VMEM((l, w2.shape[1]), F32),
            pltpu.VMEM((l, cb), F32),
            pltpu.VMEM((srows, cb), BF16),
            pltpu.VMEM((srows, cb), BF16),
        ],
        compiler_params=_params("arbitrary", "arbitrary"),
        name="filt",
    )(zfeat, w1, b1, w2, b2, fq, w3, b3, w3, b3, deltas, ma_fwd, mb_filt)


CONV_ROWS = 512
HALO = 16


def _short_conv_chunk(u_ref, w_ref, b_ref, r0, l):
    c = u_ref.shape[2]
    zeros = jnp.zeros((HALO, c), F32)
    before = zeros if r0 == 0 else u_ref[0, r0 - HALO:r0, :].astype(F32)
    after = zeros if r0 + CONV_ROWS == l else u_ref[0, r0 + CONV_ROWS:r0 + CONV_ROWS + HALO, :].astype(F32)
    ext = jnp.concatenate([before, u_ref[0, r0:r0 + CONV_ROWS, :].astype(F32), after], axis=0)
    w = w_ref[...]
    return (ext[HALO - 1:HALO - 1 + CONV_ROWS] * w[0:1]
            + ext[HALO:HALO + CONV_ROWS] * w[1:2]
            + ext[HALO + 1:HALO + 1 + CONV_ROWS] * w[2:3]
            + b_ref[...])


def _hyena_body(u0_ref, u1_ref, u2_ref, w0_ref, w1_ref, w2_ref, b0_ref, b1_ref, b2_ref, db_ref, hf_ref,
                ma_ref, mai_ref, mb_ref, mbi_ref, o_ref, z_ref, y_ref, spec_ref, *, l, order):
    _, _, _, k1n, _ = _fft_dims(l)
    blk = 2 * FFT_N2
    gates = ((u1_ref, w1_ref, b1_ref), (u2_ref, w2_ref, b2_ref))

    for r0 in range(0, l, CONV_ROWS):
        z_ref[r0:r0 + CONV_ROWS, :] = _short_conv_chunk(u0_ref, w0_ref, b0_ref, r0, l)

    for o in range(order):
        _fft_stage_a(z_ref, ma_ref, spec_ref, l)

        def filter_blocks(starts, o=o):
            cb = spec_ref.shape[1]
            x = jnp.concatenate([spec_ref[pl.ds(st, blk), :] for st in starts], axis=1)
            s = _dot(mb_ref[...], x)
            h = jnp.concatenate([hf_ref[o, pl.ds(st, blk), :] for st in starts], axis=1).astype(F32)
            sr, si, hr, hi = s[:FFT_N2], s[FFT_N2:], h[:FFT_N2], h[FFT_N2:]
            y = jnp.concatenate([sr * hr - si * hi, sr * hi + si * hr], axis=0).astype(BF16)
            c = _dot(mbi_ref[...], y).astype(BF16)
            for n, st in enumerate(starts):
                spec_ref[pl.ds(st, blk), :] = c[:, n * cb:(n + 1) * cb]

        def per_pair(i, carry, filter_blocks=filter_blocks):
            st = pl.multiple_of(i * 2 * blk, blk)
            filter_blocks([st, st + blk])
            return carry

        lax.fori_loop(0, k1n // 2, per_pair, 0)
        if k1n % 2:
            filter_blocks([(k1n - 1) * blk])
        _fft_stage_a_inv(spec_ref, mai_ref, y_ref, l)

        u_ref, w_ref, b_ref = gates[o]
        dbias = db_ref[o:o + 1, :]
        for r0 in range(0, l, CONV_ROWS):
            rows = slice(r0, r0 + CONV_ROWS)
            znew = _short_conv_chunk(u_ref, w_ref, b_ref, r0, l) * (y_ref[rows, :] + z_ref[rows, :] * dbias)
            if o + 1 < order:
                z_ref[rows, :] = znew
            else:
                o_ref[0, rows, :] = znew.astype(BF16)


def _hyena(u, conv_w, conv_b, dbias, hf, consts, order):
    b, l, _ = u.shape
    _, _, _, k1n, _ = _fft_dims(l)
    cb = HY_CB
    nblk = D_HYENA // cb
    srows = k1n * 2 * FFT_N2
    ma_fwd, ma_inv, mb_fwd, mb_inv = consts
    upart = lambda p: pl.BlockSpec((1, l, cb), lambda c, bi: (bi, 0, p * nblk + c))
    wpart = lambda p: pl.BlockSpec((3, cb), lambda c, bi: (0, p * nblk + c))
    bpart = lambda p: pl.BlockSpec((1, cb), lambda c, bi: (0, p * nblk + c))
    return pl.pallas_call(
        functools.partial(_hyena_body, l=l, order=order),
        grid=(nblk, b),
        in_specs=[
            upart(0), upart(1), upart(2), wpart(0), wpart(1), wpart(2), bpart(0), bpart(1), bpart(2),
            pl.BlockSpec((order, cb), lambda c, bi: (0, c)),
            pl.BlockSpec((order, srows, cb), lambda c, bi: (0, 0, c), pipeline_mode=pl.Buffered(1)),
            _const_spec(ma_fwd.shape), _const_spec(ma_inv.shape),
            _const_spec(mb_fwd.shape), _const_spec(mb_inv.shape),
        ],
        out_specs=pl.BlockSpec((1, l, cb), lambda c, bi: (bi, 0, c)),
        out_shape=jax.ShapeDtypeStruct((b, l, D_HYENA), BF16),
        scratch_shapes=[
            pltpu.VMEM((l, cb), F32),
            pltpu.VMEM((l, cb), F32),
            pltpu.VMEM((srows, cb), BF16),
        ],
        compiler_params=_params("arbitrary", "arbitrary"),
        name="hyena",
    )(u, u, u, conv_w, conv_w, conv_w, conv_b, conv_b, conv_b, dbias, hf, ma_fwd, ma_inv, mb_fwd, mb_inv)


ATT_TQ = MXU_DIM
ATT_KEY_CHUNK = 2 * MXU_DIM


def _attn_body(qc_ref, qn_ref, k_ref, kn_ref, kc_ref, kcn_ref, vt_ref, vct_ref, lam_ref, sg_ref, o_ref,
               sa_ref, sb_ref, ma_ref, mb_ref):
    tq = ATT_TQ
    lat = k_ref.shape[1]
    lv = lam_ref[...]
    lam = (jnp.exp(jnp.sum(lv[0:1] * lv[1:2], axis=-1, keepdims=True))
           - jnp.exp(jnp.sum(lv[2:3] * lv[3:4], axis=-1, keepdims=True)) + LAM_INIT)

    n_ctx = kc_ref.shape[1]
    chunks = [(False, r, r, ATT_KEY_CHUNK) for r in range(0, lat, ATT_KEY_CHUNK)] + [(True, 0, lat, n_ctx)]

    def stage(qt, keys_ref, ctx_keys_ref, s_new, m_new, s_old, m_old):
        m_run = acc = denom = None
        if qt is not None:
            feat = lax.broadcasted_iota(jnp.int32, qt.shape, 0)
            qz = jnp.concatenate([jnp.where(feat < HEAD_DIM, qt, jnp.zeros_like(qt)),
                                  jnp.where(feat >= HEAD_DIM, qt, jnp.zeros_like(qt))], axis=1)
        if s_old is not None:
            m_prev = m_old[...]
        for is_ctx, src, dst, rows in chunks:
            if qt is not None:
                keys = ctx_keys_ref[0, src:src + rows, :] if is_ctx else keys_ref[0, src:src + rows, :]
                s = _dot(keys, qz)
                s_new[dst:dst + rows, :] = s
                cmax = jnp.max(s, axis=0, keepdims=True)
                m_run = cmax if m_run is None else jnp.maximum(m_run, cmax)
            if s_old is not None:
                p = jnp.exp2(s_old[dst:dst + rows, :] - m_prev)
                psum = jnp.sum(p, axis=0, keepdims=True)
                vals = vct_ref[0, :, src:src + rows] if is_ctx else vt_ref[0, :, src:src + rows]
                part = _dot(vals, p.astype(BF16))
                denom = psum if denom is None else denom + psum
                acc = part if acc is None else acc + part
        if qt is not None:
            m_new[...] = m_run
        if s_old is None:
            return None
        r = 1.0 / denom
        ot = acc[:, :tq] * r[:, :tq] - acc[:, tq:] * (lam * r[:, tq:])
        return (_rms(ot.T, sg_ref[...]) * (1.0 - LAM_INIT)).astype(BF16)

    @pl.when(pl.program_id(0) == 0)
    def _():
        stage(qc_ref[0, :, :tq], k_ref, kc_ref, sa_ref, ma_ref, None, None)

    o_ref[0, :tq, :] = stage(qc_ref[0, :, tq:], k_ref, kc_ref, sb_ref, mb_ref, sa_ref, ma_ref)
    o_ref[0, tq:, :] = stage(qn_ref[0, :, :tq], kn_ref, kcn_ref, sa_ref, ma_ref, sb_ref, mb_ref)


def _attn(qt, k, vt, kc, vct, lamv, subln_g):
    b, l, _ = k.shape
    c = kc.shape[1]
    dv = 2 * HEAD_DIM
    tq = ATT_TQ
    assert l % (2 * tq) == 0
    pairs = l // (2 * tq)
    steps = b * N_HEADS * pairs

    def split(j):
        return j // (N_HEADS * pairs), (j // pairs) % N_HEADS, j % pairs

    def cur(fn):
        return lambda j: fn(*split(j))

    def nxt(fn):
        return lambda j: fn(*split(jnp.minimum(j + 1, steps - 1)))

    q_map = lambda bi, h, i: (bi, h, i)
    k_map = lambda bi, h, i: (bi, 0, h)
    v_map = lambda bi, h, i: (bi, h, 0)
    return pl.pallas_call(
        _attn_body,
        grid=(steps,),
        in_specs=[
            pl.BlockSpec((1, dv, 2 * tq), cur(q_map)), pl.BlockSpec((1, dv, 2 * tq), nxt(q_map)),
            pl.BlockSpec((1, l, dv), cur(k_map)), pl.BlockSpec((1, l, dv), nxt(k_map)),
            pl.BlockSpec((1, c, dv), cur(k_map)), pl.BlockSpec((1, c, dv), nxt(k_map)),
            pl.BlockSpec((1, dv, l), cur(v_map)), pl.BlockSpec((1, dv, c), cur(v_map)),
            _const_spec(lamv.shape), _const_spec(subln_g.shape),
        ],
        out_specs=pl.BlockSpec((1, 2 * tq, dv), cur(lambda bi, h, i: (bi, i, h))),
        out_shape=jax.ShapeDtypeStruct((b, l, N_HEADS * dv), BF16),
        scratch_shapes=[pltpu.VMEM((l + c, 2 * tq), F32), pltpu.VMEM((l + c, 2 * tq), F32),
                        pltpu.VMEM((1, 2 * tq), F32), pltpu.VMEM((1, 2 * tq), F32)],
        compiler_params=_params("arbitrary"),
        name="attn",
    )(qt, qt, k, k, kc, kc, vt, vct, lamv, subln_g)


def _post_body(x_ref, zh_ref, oa_ref, gate_ref, g1_ref, sh2_ref, sc2_ref, g2_ref,
               gpost_ref, gfpre_ref, gfpost_ref,
               whu_ref, wau_ref, wo_ref, wg_ref, wu_ref, wd_ref, o_ref, *, ff_chunks):
    d = x_ref.shape[2]
    y_hy = _dot(zh_ref[0], whu_ref[...])
    y_att = _dot(oa_ref[0], wau_ref[...])
    mixed = gate_ref[0, :, :d].astype(F32) * y_hy + gate_ref[0, :, d:].astype(F32) * y_att
    mixed = _dot(mixed.astype(BF16), wo_ref[...])
    x1 = x_ref[0] + g1_ref[0] * _rms(mixed, gpost_ref[...])
    hf = (_rms(x1, gfpre_ref[...]) * (1.0 + sc2_ref[0]) + sh2_ref[0]).astype(BF16)
    f = None
    for c0, cw in ff_chunks:
        a = _dot(hf, wg_ref[:, c0:c0 + cw])
        t = (a * jax.nn.sigmoid(a) * _dot(hf, wu_ref[:, c0:c0 + cw])).astype(BF16)
        part = _dot(t, wd_ref[c0:c0 + cw, :])
        f = part if f is None else f + part
    o_ref[0] = x1 + g2_ref[0] * _rms(f, gfpost_ref[...])


def _post(x, zh, oa, gate, g1, sh2, sc2, g2, gpost, gfpre, gfpost, whu, wau, wo, wg, wu, wd):
    b, l, d = x.shape
    dff = wg.shape[1]
    tm = 512
    ff_chunk = 4 * MXU_DIM
    ff_chunks = tuple((c0, min(ff_chunk, dff - c0)) for c0 in range(0, dff, ff_chunk))
    tile = lambda bi, i: (bi, i, 0)
    per_b = lambda bi, i: (bi, 0, 0)
    return pl.pallas_call(
        functools.partial(_post_body, ff_chunks=ff_chunks),
        grid=(b, l // tm),
        in_specs=[
            pl.BlockSpec((1, tm, d), tile),
            pl.BlockSpec((1, tm, zh.shape[2]), tile),
            pl.BlockSpec((1, tm, oa.shape[2]), tile),
            pl.BlockSpec((1, tm, gate.shape[2]), tile),
            pl.BlockSpec((1, 1, d), per_b), pl.BlockSpec((1, 1, d), per_b),
            pl.BlockSpec((1, 1, d), per_b), pl.BlockSpec((1, 1, d), per_b),
            _const_spec((1, d)), _const_spec((1, d)), _const_spec((1, d)),
        ] + [_const_spec(w.shape) for w in (whu, wau, wo, wg, wu, wd)],
        out_specs=pl.BlockSpec((1, tm, d), tile),
        out_shape=jax.ShapeDtypeStruct((b, l, d), F32),
        compiler_params=_params("arbitrary", "arbitrary"),
        name="post",
    )(x, zh, oa, gate, g1, sh2, sc2, g2, gpost, gfpre, gfpost, whu, wau, wo, wg, wu, wd)


@functools.lru_cache(maxsize=None)
def _rope_tables(l):
    quarter = HEAD_DIM // 4
    inv = ROPE_BASE ** (-np.arange(quarter, dtype=np.float64) / quarter)
    t = np.arange(l)
    ang_r = (t // GRID_W)[:, None] * inv[None, :]
    ang_c = (t % GRID_W)[:, None] * inv[None, :]
    ang = np.concatenate([ang_r, ang_r, ang_c, ang_c], axis=1)
    reps = D_ATTN // HEAD_DIM
    cos = np.tile(np.cos(ang), (1, reps)).astype(np.float32)
    sin = np.tile(np.sin(ang), (1, reps)).astype(np.float32)
    j = np.arange(D_ATTN)
    second = (j % (2 * quarter)) >= quarter
    perm = np.where(second, j - quarter, j + quarter)
    sign = np.where(second, 1.0, -1.0).astype(np.float32)
    return cos, sin, perm, sign


@functools.lru_cache(maxsize=None)
def _filter_features(l, emb, kpad):
    bands = (emb - 1) // 2
    t = np.linspace(0.0, 1.0, l, dtype=np.float32).astype(np.float64)[:, None]
    w = (2.0 * math.pi / l) * np.arange(l, dtype=np.float64)[:, None]
    fr = np.linspace(1e-4, bands - 1, bands, dtype=np.float32).astype(np.float64)[None, :]
    z = np.concatenate([t, np.cos(fr * w), -np.sin(fr * w)], axis=-1)
    z = np.pad(z, ((0, 0), (0, kpad - emb))).astype(np.float32)
    deltas = np.abs(np.linspace(math.log(FILTER_TARGET) / SLOW_DECAY_PCT,
                                math.log(FILTER_TARGET) / FAST_DECAY_PCT, D_HYENA, dtype=np.float32))
    return z, deltas[None, :].astype(np.float32)


def _pad_to(a, rows, cols, value=0.0):
    return jnp.pad(a, ((0, rows - a.shape[0]), (0, cols - a.shape[1])), constant_values=value)


def kernel(x, c, ctx, c_ctx, w_ada, b_ada, g_mix_pre, g_mix_post, g_ffn_pre, g_ffn_post, w_in, hy_conv_w, hy_conv_b, hy_f_w1, hy_f_b1, hy_f_w2, hy_f_b2, hy_f_w3, hy_f_b3, hy_f_freq, hy_bias, lambda_q1, lambda_k1, lambda_q2, lambda_k2, att_subln_g, w_hy_up, w_att_up, w_out, w_ffn_gate, w_ffn_up, w_ffn_down):
    b, l, d = x.shape
    n_ctx = ctx.shape[1]
    order = hy_bias.shape[1]
    off_q = (order + 1) * D_HYENA
    off_k = off_q + D_ATTN
    off_v = off_k + D_ATTN
    off_g = off_v + D_ATTN
    assert w_in.shape[0] == 1, "single-layer block"
    assert l % CONV_ROWS == 0 and l % 512 == 0

    pad_rows = -(-(b + 1) // 8) * 8
    cc = jnp.concatenate([c, c_ctx[None, :], jnp.zeros((pad_rows - b - 1, d), F32)], axis=0)
    ada = _ada(cc, w_ada[0], b_ada[0][None, :])
    sh1, sc1, g1, sh2, sc2, g2 = (ada[:b, i * d:(i + 1) * d][:, None, :] for i in range(6))
    csh1, csc1 = ada[b:b + 1, 0:d], ada[b:b + 1, d:2 * d]

    row = lambda v: v[0][None, :]
    w = w_in[0].astype(BF16)
    cos, sin, perm, sign = _rope_tables(l)
    q_scale = HEAD_DIM ** -0.5 * math.log2(math.e)
    wq = (w_in[0][:, off_q:off_k] * q_scale).astype(BF16)
    wk = w[:, off_k:off_v]
    wvt = w[:, off_v:off_g].T
    wqst = (wq[:, perm] * sign.astype(BF16)).T
    wks = wk[:, perm] * sign.astype(BF16)

    kc, vct = _ctx_kv(ctx, row(g_mix_pre), csh1, csc1, wk, wvt)

    u, qt, k, vt, gate = _proj(x, row(g_mix_pre), sh1, sc1, jnp.asarray(cos), jnp.asarray(sin),
                               jnp.asarray(cos.T), jnp.asarray(sin.T),
                               w[:, :off_q], wq.T, wqst, wk, wks, wvt, w[:, off_g:])

    consts = tuple(jnp.asarray(a).astype(BF16) for a in _fft_constants(l))
    zfeat, deltas = _filter_features(l, hy_f_w1.shape[1], LANES)
    hf = _filt(jnp.asarray(zfeat),
               _pad_to(hy_f_w1[0], LANES, LANES), _pad_to(row(hy_f_b1), 1, LANES),
               _pad_to(hy_f_w2[0], LANES, LANES), _pad_to(row(hy_f_b2), 1, LANES),
               _pad_to(row(hy_f_freq), 1, LANES, 1.0),
               _pad_to(hy_f_w3[0], LANES, hy_f_w3.shape[2]), row(hy_f_b3),
               jnp.asarray(deltas), consts[0], consts[4], order, l)

    zh = _hyena(u, hy_conv_w[0], row(hy_conv_b), hy_bias[0], hf, consts[:4], order)

    lamv = jnp.concatenate([lambda_q1, lambda_k1, lambda_q2, lambda_k2], axis=0)
    oa = _attn(qt, k, vt, kc, vct, lamv, row(att_subln_g))

    return _post(x, zh, oa, gate, g1, sh2, sc2, g2, row(g_mix_post), row(g_ffn_pre), row(g_ffn_post),
                 w_hy_up[0].astype(BF16), w_att_up[0].astype(BF16), w_out[0].astype(BF16),
                 w_ffn_gate[0].astype(BF16), w_ffn_up[0].astype(BF16), w_ffn_down[0].astype(BF16))
```

```python
import functools
import math

import numpy as np
import jax
import jax.numpy as jnp
from jax import lax
from jax.experimental import pallas as pl
from jax.experimental.pallas import tpu as pltpu

F32 = jnp.float32
BF16 = jnp.bfloat16
HIGHEST = lax.Precision.HIGHEST

EPS = 1e-6
GRID_W = 64
HEAD_DIM = 64
N_HEADS = 4
D_HYENA = 512
D_ATTN = N_HEADS * 2 * HEAD_DIM
ROPE_BASE = 10000.0
FILTER_TARGET = 1e-2
FAST_DECAY_PCT = 0.3
SLOW_DECAY_PCT = 1.5
LAM_INIT = 0.8 - 0.6 * math.exp(0.0)

LANES = 128
MXU_DIM = 256
VMEM_LIMIT_BYTES = 56 * 1024 * 1024

FFT_N2 = 256
FFT_GROUP = 16
HY_CB = 256
FFT_GROUP_UNROLL = 8


def _dot(a, b):
    return jnp.dot(a, b, preferred_element_type=F32)


def _dot_hi(a, b):
    return jnp.dot(a, b, preferred_element_type=F32, precision=HIGHEST)


def _dot_nt(a, b):
    return lax.dot_general(a, b, (((1,), (1,)), ((), ())), preferred_element_type=F32)


def _rms(x, g):
    return x * lax.rsqrt(jnp.mean(x * x, axis=-1, keepdims=True) + EPS) * g


def _const_spec(shape):
    zeros = (0,) * len(shape)
    return pl.BlockSpec(shape, lambda *_: zeros, pipeline_mode=pl.Buffered(1))


def _params(*semantics):
    return pltpu.CompilerParams(dimension_semantics=semantics, vmem_limit_bytes=VMEM_LIMIT_BYTES)


def _ada_body(c_ref, w_ref, b_ref, o_ref):
    c = c_ref[...]
    o_ref[...] = _dot_hi(c * jax.nn.sigmoid(c), w_ref[...]) + b_ref[...]


def _ada(cc, w_ada, b_ada):
    rows, d = cc.shape
    n = w_ada.shape[1]
    tn = 1024
    return pl.pallas_call(
        _ada_body,
        grid=(n // tn,),
        in_specs=[
            pl.BlockSpec((rows, d), lambda j: (0, 0)),
            pl.BlockSpec((d, tn), lambda j: (0, j)),
            pl.BlockSpec((1, tn), lambda j: (0, j)),
        ],
        out_specs=pl.BlockSpec((rows, tn), lambda j: (0, j)),
        out_shape=jax.ShapeDtypeStruct((rows, n), F32),
        compiler_params=_params("arbitrary"),
        name="ada",
    )(cc, w_ada, b_ada)


def _ctx_kv_body(x_ref, g_ref, sh_ref, sc_ref, wk_ref, wvt_ref, k_ref, vt_ref):
    h = _rms(x_ref[0], g_ref[...]) * (1.0 + sc_ref[...]) + sh_ref[...]
    hb = h.astype(BF16)
    k_ref[0] = _dot(hb, wk_ref[...]).astype(BF16)
    vt_ref[0] = _dot_nt(wvt_ref[...], hb).astype(BF16)


def _ctx_kv(ctx, g, sh, sc, wk, wvt):
    b, n_ctx, d = ctx.shape
    n = wk.shape[1]
    return pl.pallas_call(
        _ctx_kv_body,
        grid=(b,),
        in_specs=[
            pl.BlockSpec((1, n_ctx, d), lambda i: (i, 0, 0)),
            _const_spec((1, d)), _const_spec((1, d)), _const_spec((1, d)),
            _const_spec((d, n)), _const_spec((n, d)),
        ],
        out_specs=[pl.BlockSpec((1, n_ctx, n), lambda i: (i, 0, 0)),
                   pl.BlockSpec((1, n, n_ctx), lambda i: (i, 0, 0))],
        out_shape=[jax.ShapeDtypeStruct((b, n_ctx, n), BF16), jax.ShapeDtypeStruct((b, n, n_ctx), BF16)],
        compiler_params=_params("arbitrary"),
        name="ctx_kv",
    )(ctx, g, sh, sc, wk, wvt)


def _proj_body(x_ref, g_ref, sh_ref, sc_ref, cos_ref, sin_ref, cost_ref, sint_ref,
               wu_ref, wqt_ref, wqst_ref, wk_ref, wks_ref, wvt_ref, wg_ref,
               u_ref, qt_ref, k_ref, vt_ref, gate_ref):
    h = _rms(x_ref[0], g_ref[...]) * (1.0 + sc_ref[0]) + sh_ref[0]
    hb = h.astype(BF16)
    u_ref[0] = _dot(hb, wu_ref[...]).astype(BF16)
    qt_ref[0] = (_dot_nt(wqt_ref[...], hb) * cost_ref[...]
                 + _dot_nt(wqst_ref[...], hb) * sint_ref[...]).astype(BF16)
    k_ref[0] = (_dot(hb, wk_ref[...]) * cos_ref[...] + _dot(hb, wks_ref[...]) * sin_ref[...]).astype(BF16)
    vt_ref[0] = _dot_nt(wvt_ref[...], hb).astype(BF16)
    gate_ref[0] = jax.nn.sigmoid(_dot(hb, wg_ref[...])).astype(BF16)


def _proj(x, g, sh, sc, cos, sin, cost, sint, wu, wqt, wqst, wk, wks, wvt, wg):
    b, l, d = x.shape
    tm = 512
    tile = lambda i, bi: (bi, i, 0)
    tile_t = lambda i, bi: (bi, 0, i)
    per_b = lambda i, bi: (bi, 0, 0)
    bf = lambda shape: jax.ShapeDtypeStruct(shape, BF16)
    n_u, n_g = wu.shape[1], wg.shape[1]
    return pl.pallas_call(
        _proj_body,
        grid=(l // tm, b),
        in_specs=[
            pl.BlockSpec((1, tm, d), tile),
            _const_spec((1, d)),
            pl.BlockSpec((1, 1, d), per_b), pl.BlockSpec((1, 1, d), per_b),
            pl.BlockSpec((tm, D_ATTN), lambda i, bi: (i, 0)), pl.BlockSpec((tm, D_ATTN), lambda i, bi: (i, 0)),
            pl.BlockSpec((D_ATTN, tm), lambda i, bi: (0, i)), pl.BlockSpec((D_ATTN, tm), lambda i, bi: (0, i)),
        ] + [_const_spec(w.shape) for w in (wu, wqt, wqst, wk, wks, wvt, wg)],
        out_specs=[pl.BlockSpec((1, tm, n_u), tile), pl.BlockSpec((1, D_ATTN, tm), tile_t),
                   pl.BlockSpec((1, tm, D_ATTN), tile), pl.BlockSpec((1, D_ATTN, tm), tile_t),
                   pl.BlockSpec((1, tm, n_g), tile)],
        out_shape=[bf((b, l, n_u)), bf((b, D_ATTN, l)), bf((b, l, D_ATTN)), bf((b, D_ATTN, l)), bf((b, l, n_g))],
        compiler_params=_params("arbitrary", "arbitrary"),
        name="proj",
    )(x, g, sh, sc, cos, sin, cost, sint, wu, wqt, wqst, wk, wks, wvt, wg)


def _fft_dims(l):
    n = 2 * l
    n1 = n // FFT_N2
    h1 = n1 // 2
    k1 = h1 + 1
    ng = FFT_N2 // FFT_GROUP
    return n, n1, h1, k1, ng


@functools.lru_cache(maxsize=None)
def _fft_constants(l):
    n, n1, h1, k1n, ng = _fft_dims(l)
    g = FFT_GROUP
    ma = np.zeros((ng, k1n, 2, g, h1, g), np.float64)
    for gi in range(ng):
        for j in range(g):
            n2 = gi * g + j
            nn = FFT_N2 * np.arange(h1)[None, :] + n2
            theta = 2.0 * np.pi * ((np.arange(k1n)[:, None] * nn) % n) / n
            ma[gi, :, 0, j, :, j] = np.cos(theta)
            ma[gi, :, 1, j, :, j] = -np.sin(theta)
    weight = np.full((k1n,), 2.0)
    weight[0] = weight[-1] = 1.0
    ma_fwd = ma.reshape(ng, k1n * 2 * g, h1 * g)
    ma_inv = (ma * (weight / n)[None, :, None, None, None, None]).reshape(ng, k1n * 2 * g, h1 * g)
    ma_inv = np.transpose(ma_inv, (0, 2, 1))
    kpad = -(-ma_inv.shape[2] // LANES) * LANES
    ma_inv = np.pad(ma_inv, ((0, 0), (0, 0), (0, kpad - ma_inv.shape[2])))
    ang = 2.0 * np.pi * ((np.arange(FFT_N2)[:, None] * np.arange(FFT_N2)[None, :]) % FFT_N2) / FFT_N2
    fr, fi = np.cos(ang), -np.sin(ang)
    mb_fwd = np.block([[fr, -fi], [fi, fr]])
    mb_inv = np.block([[fr, fi], [-fi, fr]])
    mb_filt = np.block([[fr, -fi, fr, -fi], [fi, fr, -fi, -fr]])
    return tuple(np.asarray(a, np.float32) for a in (ma_fwd, ma_inv, mb_fwd, mb_inv, mb_filt))


def _fft_stage_a(src_ref, ma_ref, spec_ref, l):
    _, _, h1, k1n, ng = _fft_dims(l)
    g = FFT_GROUP

    def group(gi, carry):
        off = gi * g
        xg = jnp.concatenate(
            [src_ref[pl.ds(pl.multiple_of(FFT_N2 * i + off, g), g), :] for i in range(h1)], axis=0)
        bg = _dot(ma_ref[gi], xg.astype(BF16)).astype(BF16)
        for c in range(2 * k1n):
            spec_ref[pl.ds(pl.multiple_of(c * FFT_N2 + off, g), g), :] = bg[c * g:(c + 1) * g]
        return carry

    lax.fori_loop(0, ng, group, 0, unroll=FFT_GROUP_UNROLL)


def _fft_stage_a_inv(spec_ref, ma_inv_ref, dst_ref, l):
    _, _, h1, k1n, ng = _fft_dims(l)
    g = FFT_GROUP
    kpad = ma_inv_ref.shape[2] - 2 * k1n * g

    def group(gi, carry):
        off = gi * g
        chunks = [spec_ref[pl.ds(pl.multiple_of(c * FFT_N2 + off, g), g), :] for c in range(2 * k1n)]
        if kpad:
            chunks.append(jnp.zeros((kpad, spec_ref.shape[1]), BF16))
        yg = _dot(ma_inv_ref[gi], jnp.concatenate(chunks, axis=0))
        for i in range(h1):
            dst_ref[pl.ds(pl.multiple_of(FFT_N2 * i + off, g), g), :] = yg[i * g:(i + 1) * g]
        return carry

    lax.fori_loop(0, ng, group, 0, unroll=FFT_GROUP_UNROLL)


def _filt_body(z_ref, w1_ref, b1_ref, w2_ref, b2_ref, fq_ref, w3f_ref, b3f_ref, w3b_ref, b3b_ref,
               dl_ref, ma_ref, mb_ref, hf_ref, hdn_ref, sig_ref, specf_ref, specb_ref, *, l):
    _, _, _, k1n, _ = _fft_dims(l)

    @pl.when((pl.program_id(0) == 0) & (pl.program_id(1) == 0))
    def _():
        fq = fq_ref[...]
        h1 = jnp.sin(fq * (_dot_hi(z_ref[...], w1_ref[...]) + b1_ref[...]))
        hdn_ref[...] = jnp.sin(fq * (_dot_hi(h1, w2_ref[...]) + b2_ref[...]))

    hdn = hdn_ref[...]
    cb = specf_ref.shape[1]
    blk = 2 * FFT_N2
    rows = lax.broadcasted_iota(jnp.int32, (l, cb), 0)
    window = jnp.exp(rows.astype(F32) * (-1.0 / (l - 1)) * dl_ref[...])

    sig_ref[...] = (_dot_hi(hdn, w3f_ref[...]) + b3f_ref[...]) * window
    _fft_stage_a(sig_ref, ma_ref, specf_ref, l)
    sig_ref[...] = jnp.where(rows > 0, (_dot_hi(hdn, w3b_ref[...]) + b3b_ref[...]) * window, 0.0)
    _fft_stage_a(sig_ref, ma_ref, specb_ref, l)

    def blocks(starts):
        x = jnp.concatenate(
            [jnp.concatenate([specf_ref[pl.ds(st, blk), :], specb_ref[pl.ds(st, blk), :]], axis=0)
             for st in starts], axis=1)
        s = _dot(mb_ref[...], x).astype(BF16)
        for n, st in enumerate(starts):
            hf_ref[0, pl.ds(st, blk), :] = s[:, n * cb:(n + 1) * cb]

    def per_pair(i, carry):
        st = pl.multiple_of(i * 2 * blk, blk)
        blocks([st, st + blk])
        return carry

    lax.fori_loop(0, k1n // 2, per_pair, 0)
    if k1n % 2:
        blocks([(k1n - 1) * blk])


def _filt(zfeat, w1, b1, w2, b2, fq, w3, b3, deltas, ma_fwd, mb_filt, order, l):
    _, _, _, k1n, _ = _fft_dims(l)
    cb = HY_CB
    nblk = D_HYENA // cb
    srows = k1n * 2 * FFT_N2
    fcol = lambda o, c: (0, (2 * o) * nblk + c)
    bcol = lambda o, c: (0, (2 * o + 1) * nblk + c)
    return pl.pallas_call(
        functools.partial(_filt_body, l=l),
        grid=(order, nblk),
        in_specs=[
            _const_spec(zfeat.shape), _const_spec(w1.shape), _const_spec(b1.shape),
            _const_spec(w2.shape), _const_spec(b2.shape), _const_spec(fq.shape),
            pl.BlockSpec((w3.shape[0], cb), fcol), pl.BlockSpec((1, cb), fcol),
            pl.BlockSpec((w3.shape[0], cb), bcol), pl.BlockSpec((1, cb), bcol),
            pl.BlockSpec((1, cb), lambda o, c: (0, c)),
            _const_spec(ma_fwd.shape), _const_spec(mb_filt.shape),
        ],
        out_specs=pl.BlockSpec((1, srows, cb), lambda o, c: (o, 0, c)),
        out_shape=jax.ShapeDtypeStruct((order, srows, D_HYENA), BF16),
        scratch_shapes=[
            pltpu.VMEM((l, w2.shape[1]), F32),
            pltpu.VMEM((l, cb), F32),
            pltpu.VMEM((srows, cb), BF16),
            pltpu.VMEM((srows, cb), BF16),
        ],
        compiler_params=_params("arbitrary", "arbitrary"),
        name="filt",
    )(zfeat, w1, b1, w2, b2, fq, w3, b3, w3, b3, deltas, ma_fwd, mb_filt)


CONV_ROWS = 512
HALO = 16


def _short_conv_chunk(u_ref, w_ref, b_ref, r0, l):
    c = u_ref.shape[2]
    zeros = jnp.zeros((HALO, c), F32)
    before = zeros if r0 == 0 else u_ref[0, r0 - HALO:r0, :].astype(F32)
    after = zeros if r0 + CONV_ROWS == l else u_ref[0, r0 + CONV_ROWS:r0 + CONV_ROWS + HALO, :].astype(F32)
    ext = jnp.concatenate([before, u_ref[0, r0:r0 + CONV_ROWS, :].astype(F32), after], axis=0)
    w = w_ref[...]
    return (ext[HALO - 1:HALO - 1 + CONV_ROWS] * w[0:1]
            + ext[HALO:HALO + CONV_ROWS] * w[1:2]
            + ext[HALO + 1:HALO + 1 + CONV_ROWS] * w[2:3]
            + b_ref[...])


def _hyena_body(u0_ref, u1_ref, u2_ref, w0_ref, w1_ref, w2_ref, b0_ref, b1_ref, b2_ref, db_ref, hf_ref,
                ma_ref, mai_ref, mb_ref, mbi_ref, o_ref, z_ref, y_ref, spec_ref, *, l, order):
    _, _, _, k1n, _ = _fft_dims(l)
    blk = 2 * FFT_N2
    gates = ((u1_ref, w1_ref, b1_ref), (u2_ref, w2_ref, b2_ref))

    for r0 in range(0, l, CONV_ROWS):
        z_ref[r0:r0 + CONV_ROWS, :] = _short_conv_chunk(u0_ref, w0_ref, b0_ref, r0, l)

    for o in range(order):
        _fft_stage_a(z_ref, ma_ref, spec_ref, l)

        cb = spec_ref.shape[1]
        groups = [list(range(i, min(i + 2, k1n))) for i in range(0, k1n, 2)]

        def stage_b(ks):
            return _dot(mb_ref[...], jnp.concatenate([spec_ref[k * blk:(k + 1) * blk, :] for k in ks], axis=1))

        def filter_and_invert(ks, s, o=o):
            h = jnp.concatenate([hf_ref[o, k * blk:(k + 1) * blk, :] for k in ks], axis=1).astype(F32)
            sr, si, hr, hi = s[:FFT_N2], s[FFT_N2:], h[:FFT_N2], h[FFT_N2:]
            y = jnp.concatenate([sr * hr - si * hi, sr * hi + si * hr], axis=0).astype(BF16)
            c = _dot(mbi_ref[...], y).astype(BF16)
            for n, k in enumerate(ks):
                spec_ref[k * blk:(k + 1) * blk, :] = c[:, n * cb:(n + 1) * cb]

        s_next = stage_b(groups[0])
        for gi, ks in enumerate(groups):
            s_cur = s_next
            if gi + 1 < len(groups):
                s_next = stage_b(groups[gi + 1])
            filter_and_invert(ks, s_cur)
        _fft_stage_a_inv(spec_ref, mai_ref, y_ref, l)

        u_ref, w_ref, b_ref = gates[o]
        dbias = db_ref[o:o + 1, :]
        for r0 in range(0, l, CONV_ROWS):
            rows = slice(r0, r0 + CONV_ROWS)
            znew = _short_conv_chunk(u_ref, w_ref, b_ref, r0, l) * (y_ref[rows, :] + z_ref[rows, :] * dbias)
            if o + 1 < order:
                z_ref[rows, :] = znew
            else:
                o_ref[0, rows, :] = znew.astype(BF16)


def _hyena(u, conv_w, conv_b, dbias, hf, consts, order):
    b, l, _ = u.shape
    _, _, _, k1n, _ = _fft_dims(l)
    cb = HY_CB
    nblk = D_HYENA // cb
    srows = k1n * 2 * FFT_N2
    ma_fwd, ma_inv, mb_fwd, mb_inv = consts
    upart = lambda p: pl.BlockSpec((1, l, cb), lambda c, bi: (bi, 0, p * nblk + c))
    wpart = lambda p: pl.BlockSpec((3, cb), lambda c, bi: (0, p * nblk + c))
    bpart = lambda p: pl.BlockSpec((1, cb), lambda c, bi: (0, p * nblk + c))
    return pl.pallas_call(
        functools.partial(_hyena_body, l=l, order=order),
        grid=(nblk, b),
        in_specs=[
            upart(0), upart(1), upart(2), wpart(0), wpart(1), wpart(2), bpart(0), bpart(1), bpart(2),
            pl.BlockSpec((order, cb), lambda c, bi: (0, c)),
            pl.BlockSpec((order, srows, cb), lambda c, bi: (0, 0, c), pipeline_mode=pl.Buffered(1)),
            _const_spec(ma_fwd.shape), _const_spec(ma_inv.shape),
            _const_spec(mb_fwd.shape), _const_spec(mb_inv.shape),
        ],
        out_specs=pl.BlockSpec((1, l, cb), lambda c, bi: (bi, 0, c)),
        out_shape=jax.ShapeDtypeStruct((b, l, D_HYENA), BF16),
        scratch_shapes=[
            pltpu.VMEM((l, cb), F32),
            pltpu.VMEM((l, cb), F32),
            pltpu.VMEM((srows, cb), BF16),
        ],
        compiler_params=_params("arbitrary", "arbitrary"),
        name="hyena",
    )(u, u, u, conv_w, conv_w, conv_w, conv_b, conv_b, conv_b, dbias, hf, ma_fwd, ma_inv, mb_fwd, mb_inv)


ATT_TQ = MXU_DIM
ATT_KEY_CHUNK = 2 * MXU_DIM


def _attn_body(qc_ref, qn_ref, k_ref, kn_ref, kc_ref, kcn_ref, vt_ref, vct_ref, lam_ref, sg_ref, o_ref,
               sa_ref, sb_ref, ma_ref, mb_ref):
    tq = ATT_TQ
    lat = k_ref.shape[1]
    lv = lam_ref[...]
    lam = (jnp.exp(jnp.sum(lv[0:1] * lv[1:2], axis=-1, keepdims=True))
           - jnp.exp(jnp.sum(lv[2:3] * lv[3:4], axis=-1, keepdims=True)) + LAM_INIT)

    n_ctx = kc_ref.shape[1]
    chunks = [(False, r, r, ATT_KEY_CHUNK) for r in range(0, lat, ATT_KEY_CHUNK)] + [(True, 0, lat, n_ctx)]

    def stage(qt, keys_ref, ctx_keys_ref, s_new, m_new, s_old, m_old):
        m_run = acc = denom = None
        if qt is not None:
            feat = lax.broadcasted_iota(jnp.int32, qt.shape, 0)
            qz = jnp.concatenate([jnp.where(feat < HEAD_DIM, qt, jnp.zeros_like(qt)),
                                  jnp.where(feat >= HEAD_DIM, qt, jnp.zeros_like(qt))], axis=1)
        if s_old is not None:
            m_prev = m_old[...]
        for is_ctx, src, dst, rows in chunks:
            if qt is not None:
                keys = ctx_keys_ref[0, src:src + rows, :] if is_ctx else keys_ref[0, src:src + rows, :]
                s = _dot(keys, qz)
                s_new[dst:dst + rows, :] = s
                cmax = jnp.max(s, axis=0, keepdims=True)
                m_run = cmax if m_run is None else jnp.maximum(m_run, cmax)
            if s_old is not None:
                p = jnp.exp2(s_old[dst:dst + rows, :] - m_prev)
                psum = jnp.sum(p, axis=0, keepdims=True)
                vals = vct_ref[0, :, src:src + rows] if is_ctx else vt_ref[0, :, src:src + rows]
                part = _dot(vals, p.astype(BF16))
                denom = psum if denom is None else denom + psum
                acc = part if acc is None else acc + part
        if qt is not None:
            m_new[...] = m_run
        if s_old is None:
            return None
        r = 1.0 / denom
        ot = acc[:, :tq] * r[:, :tq] - acc[:, tq:] * (lam * r[:, tq:])
        return (_rms(ot.T, sg_ref[...]) * (1.0 - LAM_INIT)).astype(BF16)

    @pl.when(pl.program_id(0) == 0)
    def _():
        stage(qc_ref[0, :, :tq], k_ref, kc_ref, sa_ref, ma_ref, None, None)

    o_ref[0, :tq, :] = stage(qc_ref[0, :, tq:], k_ref, kc_ref, sb_ref, mb_ref, sa_ref, ma_ref)
    o_ref[0, tq:, :] = stage(qn_ref[0, :, :tq], kn_ref, kcn_ref, sa_ref, ma_ref, sb_ref, mb_ref)


def _attn(qt, k, vt, kc, vct, lamv, subln_g):
    b, l, _ = k.shape
    c = kc.shape[1]
    dv = 2 * HEAD_DIM
    tq = ATT_TQ
    assert l % (2 * tq) == 0
    pairs = l // (2 * tq)
    steps = b * N_HEADS * pairs

    def split(j):
        return j // (N_HEADS * pairs), (j // pairs) % N_HEADS, j % pairs

    def cur(fn):
        return lambda j: fn(*split(j))

    def nxt(fn):
        return lambda j: fn(*split(jnp.minimum(j + 1, steps - 1)))

    q_map = lambda bi, h, i: (bi, h, i)
    k_map = lambda bi, h, i: (bi, 0, h)
    v_map = lambda bi, h, i: (bi, h, 0)
    return pl.pallas_call(
        _attn_body,
        grid=(steps,),
        in_specs=[
            pl.BlockSpec((1, dv, 2 * tq), cur(q_map)), pl.BlockSpec((1, dv, 2 * tq), nxt(q_map)),
            pl.BlockSpec((1, l, dv), cur(k_map)), pl.BlockSpec((1, l, dv), nxt(k_map)),
            pl.BlockSpec((1, c, dv), cur(k_map)), pl.BlockSpec((1, c, dv), nxt(k_map)),
            pl.BlockSpec((1, dv, l), cur(v_map)), pl.BlockSpec((1, dv, c), cur(v_map)),
            _const_spec(lamv.shape), _const_spec(subln_g.shape),
        ],
        out_specs=pl.BlockSpec((1, 2 * tq, dv), cur(lambda bi, h, i: (bi, i, h))),
        out_shape=jax.ShapeDtypeStruct((b, l, N_HEADS * dv), BF16),
        scratch_shapes=[pltpu.VMEM((l + c, 2 * tq), F32), pltpu.VMEM((l + c, 2 * tq), F32),
                        pltpu.VMEM((1, 2 * tq), F32), pltpu.VMEM((1, 2 * tq), F32)],
        compiler_params=_params("arbitrary"),
        name="attn",
    )(qt, qt, k, k, kc, kc, vt, vct, lamv, subln_g)


def _post_body(x_ref, zh_ref, oa_ref, gate_ref, g1_ref, sh2_ref, sc2_ref, g2_ref,
               gpost_ref, gfpre_ref, gfpost_ref,
               whu_ref, wau_ref, wo_ref, wg_ref, wu_ref, wd_ref, o_ref, *, ff_chunks):
    d = x_ref.shape[2]
    y_hy = _dot(zh_ref[0], whu_ref[...])
    y_att = _dot(oa_ref[0], wau_ref[...])
    mixed = gate_ref[0, :, :d].astype(F32) * y_hy + gate_ref[0, :, d:].astype(F32) * y_att
    mixed = _dot(mixed.astype(BF16), wo_ref[...])
    x1 = x_ref[0] + g1_ref[0] * _rms(mixed, gpost_ref[...])
    hf = (_rms(x1, gfpre_ref[...]) * (1.0 + sc2_ref[0]) + sh2_ref[0]).astype(BF16)
    f = None
    for c0, cw in ff_chunks:
        a = _dot(hf, wg_ref[:, c0:c0 + cw])
        t = (a * jax.nn.sigmoid(a) * _dot(hf, wu_ref[:, c0:c0 + cw])).astype(BF16)
        part = _dot(t, wd_ref[c0:c0 + cw, :])
        f = part if f is None else f + part
    o_ref[0] = x1 + g2_ref[0] * _rms(f, gfpost_ref[...])


def _post(x, zh, oa, gate, g1, sh2, sc2, g2, gpost, gfpre, gfpost, whu, wau, wo, wg, wu, wd):
    b, l, d = x.shape
    dff = wg.shape[1]
    tm = 512
    ff_chunk = 4 * MXU_DIM
    ff_chunks = tuple((c0, min(ff_chunk, dff - c0)) for c0 in range(0, dff, ff_chunk))
    tile = lambda bi, i: (bi, i, 0)
    per_b = lambda bi, i: (bi, 0, 0)
    return pl.pallas_call(
        functools.partial(_post_body, ff_chunks=ff_chunks),
        grid=(b, l // tm),
        in_specs=[
            pl.BlockSpec((1, tm, d), tile),
            pl.BlockSpec((1, tm, zh.shape[2]), tile),
            pl.BlockSpec((1, tm, oa.shape[2]), tile),
            pl.BlockSpec((1, tm, gate.shape[2]), tile),
            pl.BlockSpec((1, 1, d), per_b), pl.BlockSpec((1, 1, d), per_b),
            pl.BlockSpec((1, 1, d), per_b), pl.BlockSpec((1, 1, d), per_b),
            _const_spec((1, d)), _const_spec((1, d)), _const_spec((1, d)),
        ] + [_const_spec(w.shape) for w in (whu, wau, wo, wg, wu, wd)],
        out_specs=pl.BlockSpec((1, tm, d), tile),
        out_shape=jax.ShapeDtypeStruct((b, l, d), F32),
        compiler_params=_params("arbitrary", "arbitrary"),
        name="post",
    )(x, zh, oa, gate, g1, sh2, sc2, g2, gpost, gfpre, gfpost, whu, wau, wo, wg, wu, wd)


@functools.lru_cache(maxsize=None)
def _rope_tables(l):
    quarter = HEAD_DIM // 4
    inv = ROPE_BASE ** (-np.arange(quarter, dtype=np.float64) / quarter)
    t = np.arange(l)
    ang_r = (t // GRID_W)[:, None] * inv[None, :]
    ang_c = (t % GRID_W)[:, None] * inv[None, :]
    ang = np.concatenate([ang_r, ang_r, ang_c, ang_c], axis=1)
    reps = D_ATTN // HEAD_DIM
    cos = np.tile(np.cos(ang), (1, reps)).astype(np.float32)
    sin = np.tile(np.sin(ang), (1, reps)).astype(np.float32)
    j = np.arange(D_ATTN)
    second = (j % (2 * quarter)) >= quarter
    perm = np.where(second, j - quarter, j + quarter)
    sign = np.where(second, 1.0, -1.0).astype(np.float32)
    return cos, sin, perm, sign


@functools.lru_cache(maxsize=None)
def _filter_features(l, emb, kpad):
    bands = (emb - 1) // 2
    t = np.linspace(0.0, 1.0, l, dtype=np.float32).astype(np.float64)[:, None]
    w = (2.0 * math.pi / l) * np.arange(l, dtype=np.float64)[:, None]
    fr = np.linspace(1e-4, bands - 1, bands, dtype=np.float32).astype(np.float64)[None, :]
    z = np.concatenate([t, np.cos(fr * w), -np.sin(fr * w)], axis=-1)
    z = np.pad(z, ((0, 0), (0, kpad - emb))).astype(np.float32)
    deltas = np.abs(np.linspace(math.log(FILTER_TARGET) / SLOW_DECAY_PCT,
                                math.log(FILTER_TARGET) / FAST_DECAY_PCT, D_HYENA, dtype=np.float32))
    return z, deltas[None, :].astype(np.float32)


def _pad_to(a, rows, cols, value=0.0):
    return jnp.pad(a, ((0, rows - a.shape[0]), (0, cols - a.shape[1])), constant_values=value)


def kernel(x, c, ctx, c_ctx, w_ada, b_ada, g_mix_pre, g_mix_post, g_ffn_pre, g_ffn_post, w_in, hy_conv_w, hy_conv_b, hy_f_w1, hy_f_b1, hy_f_w2, hy_f_b2, hy_f_w3, hy_f_b3, hy_f_freq, hy_bias, lambda_q1, lambda_k1, lambda_q2, lambda_k2, att_subln_g, w_hy_up, w_att_up, w_out, w_ffn_gate, w_ffn_up, w_ffn_down):
    b, l, d = x.shape
    n_ctx = ctx.shape[1]
    order = hy_bias.shape[1]
    off_q = (order + 1) * D_HYENA
    off_k = off_q + D_ATTN
    off_v = off_k + D_ATTN
    off_g = off_v + D_ATTN
    assert w_in.shape[0] == 1, "single-layer block"
    assert l % CONV_ROWS == 0 and l % 512 == 0

    pad_rows = -(-(b + 1) // 8) * 8
    cc = jnp.concatenate([c, c_ctx[None, :], jnp.zeros((pad_rows - b - 1, d), F32)], axis=0)
    ada = _ada(cc, w_ada[0], b_ada[0][None, :])
    sh1, sc1, g1, sh2, sc2, g2 = (ada[:b, i * d:(i + 1) * d][:, None, :] for i in range(6))
    csh1, csc1 = ada[b:b + 1, 0:d], ada[b:b + 1, d:2 * d]

    row = lambda v: v[0][None, :]
    w = w_in[0].astype(BF16)
    cos, sin, perm, sign = _rope_tables(l)
    q_scale = HEAD_DIM ** -0.5 * math.log2(math.e)
    wq = (w_in[0][:, off_q:off_k] * q_scale).astype(BF16)
    wk = w[:, off_k:off_v]
    wvt = w[:, off_v:off_g].T
    wqst = (wq[:, perm] * sign.astype(BF16)).T
    wks = wk[:, perm] * sign.astype(BF16)

    kc, vct = _ctx_kv(ctx, row(g_mix_pre), csh1, csc1, wk, wvt)

    u, qt, k, vt, gate = _proj(x, row(g_mix_pre), sh1, sc1, jnp.asarray(cos), jnp.asarray(sin),
                               jnp.asarray(cos.T), jnp.asarray(sin.T),
                               w[:, :off_q], wq.T, wqst, wk, wks, wvt, w[:, off_g:])

    consts = tuple(jnp.asarray(a).astype(BF16) for a in _fft_constants(l))
    zfeat, deltas = _filter_features(l, hy_f_w1.shape[1], LANES)
    hf = _filt(jnp.asarray(zfeat),
               _pad_to(hy_f_w1[0], LANES, LANES), _pad_to(row(hy_f_b1), 1, LANES),
               _pad_to(hy_f_w2[0], LANES, LANES), _pad_to(row(hy_f_b2), 1, LANES),
               _pad_to(row(hy_f_freq), 1, LANES, 1.0),
               _pad_to(hy_f_w3[0], LANES, hy_f_w3.shape[2]), row(hy_f_b3),
               jnp.asarray(deltas), consts[0], consts[4], order, l)

    zh = _hyena(u, hy_conv_w[0], row(hy_conv_b), hy_bias[0], hf, consts[:4], order)

    lamv = jnp.concatenate([lambda_q1, lambda_k1, lambda_q2, lambda_k2], axis=0)
    oa = _attn(qt, k, vt, kc, vct, lamv, row(att_subln_g))

    return _post(x, zh, oa, gate, g1, sh2, sc2, g2, row(g_mix_post), row(g_ffn_pre), row(g_ffn_post),
                 w_hy_up[0].astype(BF16), w_att_up[0].astype(BF16), w_out[0].astype(BF16),
                 w_ffn_gate[0].astype(BF16), w_ffn_up[0].astype(BF16), w_ffn_down[0].astype(BF16))
```

```python
import functools
import math

import numpy as np
import jax
import jax.numpy as jnp
from jax import lax
from jax.experimental import pallas as pl
from jax.experimental.pallas import tpu as pltpu

F32 = jnp.float32
BF16 = jnp.bfloat16
HIGHEST = lax.Precision.HIGHEST

EPS = 1e-6
GRID_W = 64
HEAD_DIM = 64
N_HEADS = 4
D_HYENA = 512
D_ATTN = N_HEADS * 2 * HEAD_DIM
ROPE_BASE = 10000.0
FILTER_TARGET = 1e-2
FAST_DECAY_PCT = 0.3
SLOW_DECAY_PCT = 1.5
LAM_INIT = 0.8 - 0.6 * math.exp(0.0)

LANES = 128
MXU_DIM = 256
VMEM_LIMIT_BYTES = 56 * 1024 * 1024

FFT_N2 = 256
FFT_GROUP = 16
HY_CB = 256
FFT_GROUP_UNROLL = 8


def _dot(a, b):
    return jnp.dot(a, b, preferred_element_type=F32)


def _dot_hi(a, b):
    return jnp.dot(a, b, preferred_element_type=F32, precision=HIGHEST)


def _dot_nt(a, b):
    return lax.dot_general(a, b, (((1,), (1,)), ((), ())), preferred_element_type=F32)


def _rms(x, g):
    return x * lax.rsqrt(jnp.mean(x * x, axis=-1, keepdims=True) + EPS) * g


def _const_spec(shape):
    zeros = (0,) * len(shape)
    return pl.BlockSpec(shape, lambda *_: zeros, pipeline_mode=pl.Buffered(1))


def _params(*semantics):
    return pltpu.CompilerParams(dimension_semantics=semantics, vmem_limit_bytes=VMEM_LIMIT_BYTES)


def _ada_body(c_ref, w_ref, b_ref, o_ref):
    c = c_ref[...]
    o_ref[...] = _dot_hi(c * jax.nn.sigmoid(c), w_ref[...]) + b_ref[...]


def _ada(cc, w_ada, b_ada):
    rows, d = cc.shape
    n = w_ada.shape[1]
    tn = 1024
    return pl.pallas_call(
        _ada_body,
        grid=(n // tn,),
        in_specs=[
            pl.BlockSpec((rows, d), lambda j: (0, 0)),
            pl.BlockSpec((d, tn), lambda j: (0, j)),
            pl.BlockSpec((1, tn), lambda j: (0, j)),
        ],
        out_specs=pl.BlockSpec((rows, tn), lambda j: (0, j)),
        out_shape=jax.ShapeDtypeStruct((rows, n), F32),
        compiler_params=_params("arbitrary"),
        name="ada",
    )(cc, w_ada, b_ada)


def _ctx_kv_body(x_ref, g_ref, sh_ref, sc_ref, wk_ref, wvt_ref, k_ref, vt_ref):
    h = _rms(x_ref[0], g_ref[...]) * (1.0 + sc_ref[...]) + sh_ref[...]
    hb = h.astype(BF16)
    k_ref[0] = _dot(hb, wk_ref[...]).astype(BF16)
    vt_ref[0] = _dot_nt(wvt_ref[...], hb).astype(BF16)


def _ctx_kv(ctx, g, sh, sc, wk, wvt):
    b, n_ctx, d = ctx.shape
    n = wk.shape[1]
    return pl.pallas_call(
        _ctx_kv_body,
        grid=(b,),
        in_specs=[
            pl.BlockSpec((1, n_ctx, d), lambda i: (i, 0, 0)),
            _const_spec((1, d)), _const_spec((1, d)), _const_spec((1, d)),
            _const_spec((d, n)), _const_spec((n, d)),
        ],
        out_specs=[pl.BlockSpec((1, n_ctx, n), lambda i: (i, 0, 0)),
                   pl.BlockSpec((1, n, n_ctx), lambda i: (i, 0, 0))],
        out_shape=[jax.ShapeDtypeStruct((b, n_ctx, n), BF16), jax.ShapeDtypeStruct((b, n, n_ctx), BF16)],
        compiler_params=_params("arbitrary"),
        name="ctx_kv",
    )(ctx, g, sh, sc, wk, wvt)


ROT = HEAD_DIM // 4


def _proj_body(x_ref, g_ref, sh_ref, sc_ref, cos_ref, sin_ref, cost_ref, sint_ref,
               wu_ref, wqt_ref, wk_ref, wvt_ref, wg_ref,
               u_ref, qt_ref, k_ref, vt_ref, gate_ref):
    h = _rms(x_ref[0], g_ref[...]) * (1.0 + sc_ref[0]) + sh_ref[0]
    hb = h.astype(BF16)
    u_ref[0] = _dot(hb, wu_ref[...]).astype(BF16)
    qv = _dot_nt(wqt_ref[...], hb)
    partner = jnp.concatenate(
        [qv[r + ROT:r + 2 * ROT] if half == 0 else qv[r:r + ROT]
         for r in range(0, qv.shape[0], 2 * ROT) for half in (0, 1)], axis=0)
    qt_ref[0] = (qv * cost_ref[...] + partner * sint_ref[...]).astype(BF16)
    kv = _dot(hb, wk_ref[...])
    lane = lax.broadcasted_iota(jnp.int32, (kv.shape[0], LANES), 1)
    first = (lane % (2 * ROT)) < ROT
    slabs = []
    for c0 in range(0, kv.shape[1], LANES):
        slab = kv[:, c0:c0 + LANES]
        slabs.append(jnp.where(first, pltpu.roll(slab, LANES - ROT, axis=1), pltpu.roll(slab, ROT, axis=1)))
    k_ref[0] = (kv * cos_ref[...] + jnp.concatenate(slabs, axis=1) * sin_ref[...]).astype(BF16)
    vt_ref[0] = _dot_nt(wvt_ref[...], hb).astype(BF16)
    gate_ref[0] = jax.nn.sigmoid(_dot(hb, wg_ref[...])).astype(BF16)


def _proj(x, g, sh, sc, cos, sin, cost, sint, wu, wqt, wk, wvt, wg):
    b, l, d = x.shape
    tm = 512
    tile = lambda i, bi: (bi, i, 0)
    tile_t = lambda i, bi: (bi, 0, i)
    per_b = lambda i, bi: (bi, 0, 0)
    bf = lambda shape: jax.ShapeDtypeStruct(shape, BF16)
    n_u, n_g = wu.shape[1], wg.shape[1]
    return pl.pallas_call(
        _proj_body,
        grid=(l // tm, b),
        in_specs=[
            pl.BlockSpec((1, tm, d), tile),
            _const_spec((1, d)),
            pl.BlockSpec((1, 1, d), per_b), pl.BlockSpec((1, 1, d), per_b),
            pl.BlockSpec((tm, D_ATTN), lambda i, bi: (i, 0)), pl.BlockSpec((tm, D_ATTN), lambda i, bi: (i, 0)),
            pl.BlockSpec((D_ATTN, tm), lambda i, bi: (0, i)), pl.BlockSpec((D_ATTN, tm), lambda i, bi: (0, i)),
        ] + [_const_spec(w.shape) for w in (wu, wqt, wk, wvt, wg)],
        out_specs=[pl.BlockSpec((1, tm, n_u), tile), pl.BlockSpec((1, D_ATTN, tm), tile_t),
                   pl.BlockSpec((1, tm, D_ATTN), tile), pl.BlockSpec((1, D_ATTN, tm), tile_t),
                   pl.BlockSpec((1, tm, n_g), tile)],
        out_shape=[bf((b, l, n_u)), bf((b, D_ATTN, l)), bf((b, l, D_ATTN)), bf((b, D_ATTN, l)), bf((b, l, n_g))],
        compiler_params=_params("arbitrary", "arbitrary"),
        name="proj",
    )(x, g, sh, sc, cos, sin, cost, sint, wu, wqt, wk, wvt, wg)


def _fft_dims(l):
    n = 2 * l
    n1 = n // FFT_N2
    h1 = n1 // 2
    k1 = h1 + 1
    ng = FFT_N2 // FFT_GROUP
    return n, n1, h1, k1, ng


@functools.lru_cache(maxsize=None)
def _fft_constants(l):
    n, n1, h1, k1n, ng = _fft_dims(l)
    g = FFT_GROUP
    ma = np.zeros((ng, k1n, 2, g, h1, g), np.float64)
    for gi in range(ng):
        for j in range(g):
            n2 = gi * g + j
            nn = FFT_N2 * np.arange(h1)[None, :] + n2
            theta = 2.0 * np.pi * ((np.arange(k1n)[:, None] * nn) % n) / n
            ma[gi, :, 0, j, :, j] = np.cos(theta)
            ma[gi, :, 1, j, :, j] = -np.sin(theta)
    weight = np.full((k1n,), 2.0)
    weight[0] = weight[-1] = 1.0
    ma_fwd = ma.reshape(ng, k1n * 2 * g, h1 * g)
    ma_inv = (ma * (weight / n)[None, :, None, None, None, None]).reshape(ng, k1n * 2 * g, h1 * g)
    ma_inv = np.transpose(ma_inv, (0, 2, 1))
    kpad = -(-ma_inv.shape[2] // LANES) * LANES
    ma_inv = np.pad(ma_inv, ((0, 0), (0, 0), (0, kpad - ma_inv.shape[2])))
    ang = 2.0 * np.pi * ((np.arange(FFT_N2)[:, None] * np.arange(FFT_N2)[None, :]) % FFT_N2) / FFT_N2
    fr, fi = np.cos(ang), -np.sin(ang)
    mb_fwd = np.block([[fr, -fi], [fi, fr]])
    mb_inv = np.block([[fr, fi], [-fi, fr]])
    mb_filt = np.block([[fr, -fi, fr, -fi], [fi, fr, -fi, -fr]])
    return tuple(np.asarray(a, np.float32) for a in (ma_fwd, ma_inv, mb_fwd, mb_inv, mb_filt))


def _fft_stage_a(src_ref, ma_ref, spec_ref, l):
    _, _, h1, k1n, ng = _fft_dims(l)
    g = FFT_GROUP

    def group(gi, carry):
        off = gi * g
        xg = jnp.concatenate(
            [src_ref[pl.ds(pl.multiple_of(FFT_N2 * i + off, g), g), :] for i in range(h1)], axis=0)
        bg = _dot(ma_ref[gi], xg.astype(BF16)).astype(BF16)
        for c in range(2 * k1n):
            spec_ref[pl.ds(pl.multiple_of(c * FFT_N2 + off, g), g), :] = bg[c * g:(c + 1) * g]
        return carry

    lax.fori_loop(0, ng, group, 0, unroll=FFT_GROUP_UNROLL)


def _fft_stage_a_inv(spec_ref, ma_inv_ref, dst_ref, l):
    _, _, h1, k1n, ng = _fft_dims(l)
    g = FFT_GROUP
    kpad = ma_inv_ref.shape[2] - 2 * k1n * g

    def group(gi, carry):
        off = gi * g
        chunks = [spec_ref[pl.ds(pl.multiple_of(c * FFT_N2 + off, g), g), :] for c in range(2 * k1n)]
        if kpad:
            chunks.append(jnp.zeros((kpad, spec_ref.shape[1]), BF16))
        yg = _dot(ma_inv_ref[gi], jnp.concatenate(chunks, axis=0))
        for i in range(h1):
            dst_ref[pl.ds(pl.multiple_of(FFT_N2 * i + off, g), g), :] = yg[i * g:(i + 1) * g]
        return carry

    lax.fori_loop(0, ng, group, 0, unroll=FFT_GROUP_UNROLL)


def _filt_body(z_ref, w1_ref, b1_ref, w2_ref, b2_ref, fq_ref, w3f_ref, b3f_ref, w3b_ref, b3b_ref,
               dl_ref, ma_ref, mb_ref, hf_ref, hdn_ref, sig_ref, specf_ref, specb_ref, *, l):
    _, _, _, k1n, _ = _fft_dims(l)

    @pl.when((pl.program_id(0) == 0) & (pl.program_id(1) == 0))
    def _():
        fq = fq_ref[...]
        h1 = jnp.sin(fq * (_dot_hi(z_ref[...], w1_ref[...]) + b1_ref[...]))
        hdn_ref[...] = jnp.sin(fq * (_dot_hi(h1, w2_ref[...]) + b2_ref[...]))

    hdn = hdn_ref[...]
    cb = specf_ref.shape[1]
    blk = 2 * FFT_N2
    rows = lax.broadcasted_iota(jnp.int32, (l, cb), 0)
    window = jnp.exp(rows.astype(F32) * (-1.0 / (l - 1)) * dl_ref[...])

    sig_ref[...] = (_dot_hi(hdn, w3f_ref[...]) + b3f_ref[...]) * window
    _fft_stage_a(sig_ref, ma_ref, specf_ref, l)
    sig_ref[...] = jnp.where(rows > 0, (_dot_hi(hdn, w3b_ref[...]) + b3b_ref[...]) * window, 0.0)
    _fft_stage_a(sig_ref, ma_ref, specb_ref, l)

    def blocks(starts):
        x = jnp.concatenate(
            [jnp.concatenate([specf_ref[pl.ds(st, blk), :], specb_ref[pl.ds(st, blk), :]], axis=0)
             for st in starts], axis=1)
        s = _dot(mb_ref[...], x).astype(BF16)
        for n, st in enumerate(starts):
            hf_ref[0, pl.ds(st, blk), :] = s[:, n * cb:(n + 1) * cb]

    def per_pair(i, carry):
        st = pl.multiple_of(i * 2 * blk, blk)
        blocks([st, st + blk])
        return carry

    lax.fori_loop(0, k1n // 2, per_pair, 0)
    if k1n % 2:
        blocks([(k1n - 1) * blk])


def _filt(zfeat, w1, b1, w2, b2, fq, w3, b3, deltas, ma_fwd, mb_filt, order, l):
    _, _, _, k1n, _ = _fft_dims(l)
    cb = HY_CB
    nblk = D_HYENA // cb
    srows = k1n * 2 * FFT_N2
    fcol = lambda o, c: (0, (2 * o) * nblk + c)
    bcol = lambda o, c: (0, (2 * o + 1) * nblk + c)
    return pl.pallas_call(
        functools.partial(_filt_body, l=l),
        grid=(order, nblk),
        in_specs=[
            _const_spec(zfeat.shape), _const_spec(w1.shape), _const_spec(b1.shape),
            _const_spec(w2.shape), _const_spec(b2.shape), _const_spec(fq.shape),
            pl.BlockSpec((w3.shape[0], cb), fcol), pl.BlockSpec((1, cb), fcol),
            pl.BlockSpec((w3.shape[0], cb), bcol), pl.BlockSpec((1, cb), bcol),
            pl.BlockSpec((1, cb), lambda o, c: (0, c)),
            _const_spec(ma_fwd.shape), _const_spec(mb_filt.shape),
        ],
        out_specs=pl.BlockSpec((1, srows, cb), lambda o, c: (o, 0, c)),
        out_shape=jax.ShapeDtypeStruct((order, srows, D_HYENA), BF16),
        scratch_shapes=[
            pltpu.VMEM((l, w2.shape[1]), F32),
            pltpu.VMEM((l, cb), F32),
            pltpu.VMEM((srows, cb), BF16),
            pltpu.VMEM((srows, cb), BF16),
        ],
        compiler_params=_params("arbitrary", "arbitrary"),
        name="filt",
    )(zfeat, w1, b1, w2, b2, fq, w3, b3, w3, b3, deltas, ma_fwd, mb_filt)


CONV_ROWS = 512
HALO = 16


def _short_conv_chunk(u_ref, w_ref, b_ref, r0, l):
    c = u_ref.shape[2]
    zeros = jnp.zeros((HALO, c), F32)
    before = zeros if r0 == 0 else u_ref[0, r0 - HALO:r0, :].astype(F32)
    after = zeros if r0 + CONV_ROWS == l else u_ref[0, r0 + CONV_ROWS:r0 + CONV_ROWS + HALO, :].astype(F32)
    ext = jnp.concatenate([before, u_ref[0, r0:r0 + CONV_ROWS, :].astype(F32), after], axis=0)
    w = w_ref[...]
    return (ext[HALO - 1:HALO - 1 + CONV_ROWS] * w[0:1]
            + ext[HALO:HALO + CONV_ROWS] * w[1:2]
            + ext[HALO + 1:HALO + 1 + CONV_ROWS] * w[2:3]
            + b_ref[...])


def _hyena_body(u0_ref, u1_ref, u2_ref, w0_ref, w1_ref, w2_ref, b0_ref, b1_ref, b2_ref, db_ref, hf_ref,
                ma_ref, mai_ref, mb_ref, mbi_ref, o_ref, z_ref, y_ref, spec_ref, *, l, order):
    _, _, _, k1n, _ = _fft_dims(l)
    blk = 2 * FFT_N2
    gates = ((u1_ref, w1_ref, b1_ref), (u2_ref, w2_ref, b2_ref))

    for r0 in range(0, l, CONV_ROWS):
        z_ref[r0:r0 + CONV_ROWS, :] = _short_conv_chunk(u0_ref, w0_ref, b0_ref, r0, l)

    for o in range(order):
        _fft_stage_a(z_ref, ma_ref, spec_ref, l)

        cb = spec_ref.shape[1]
        groups = [list(range(i, min(i + 2, k1n))) for i in range(0, k1n, 2)]

        def stage_b(ks):
            return _dot(mb_ref[...], jnp.concatenate([spec_ref[k * blk:(k + 1) * blk, :] for k in ks], axis=1))

        def filter_and_invert(ks, s, o=o):
            h = jnp.concatenate([hf_ref[o, k * blk:(k + 1) * blk, :] for k in ks], axis=1).astype(F32)
            sr, si, hr, hi = s[:FFT_N2], s[FFT_N2:], h[:FFT_N2], h[FFT_N2:]
            y = jnp.concatenate([sr * hr - si * hi, sr * hi + si * hr], axis=0).astype(BF16)
            c = _dot(mbi_ref[...], y).astype(BF16)
            for n, k in enumerate(ks):
                spec_ref[k * blk:(k + 1) * blk, :] = c[:, n * cb:(n + 1) * cb]

        s_next = stage_b(groups[0])
        for gi, ks in enumerate(groups):
            s_cur = s_next
            if gi + 1 < len(groups):
                s_next = stage_b(groups[gi + 1])
            filter_and_invert(ks, s_cur)
        _fft_stage_a_inv(spec_ref, mai_ref, y_ref, l)

        u_ref, w_ref, b_ref = gates[o]
        dbias = db_ref[o:o + 1, :]
        for r0 in range(0, l, CONV_ROWS):
            rows = slice(r0, r0 + CONV_ROWS)
            znew = _short_conv_chunk(u_ref, w_ref, b_ref, r0, l) * (y_ref[rows, :] + z_ref[rows, :] * dbias)
            if o + 1 < order:
                z_ref[rows, :] = znew
            else:
                o_ref[0, rows, :] = znew.astype(BF16)


def _hyena(u, conv_w, conv_b, dbias, hf, consts, order):
    b, l, _ = u.shape
    _, _, _, k1n, _ = _fft_dims(l)
    cb = HY_CB
    nblk = D_HYENA // cb
    srows = k1n * 2 * FFT_N2
    ma_fwd, ma_inv, mb_fwd, mb_inv = consts
    upart = lambda p: pl.BlockSpec((1, l, cb), lambda c, bi: (bi, 0, p * nblk + c))
    wpart = lambda p: pl.BlockSpec((3, cb), lambda c, bi: (0, p * nblk + c))
    bpart = lambda p: pl.BlockSpec((1, cb), lambda c, bi: (0, p * nblk + c))
    return pl.pallas_call(
        functools.partial(_hyena_body, l=l, order=order),
        grid=(nblk, b),
        in_specs=[
            upart(0), upart(1), upart(2), wpart(0), wpart(1), wpart(2), bpart(0), bpart(1), bpart(2),
            pl.BlockSpec((order, cb), lambda c, bi: (0, c)),
            pl.BlockSpec((order, srows, cb), lambda c, bi: (0, 0, c), pipeline_mode=pl.Buffered(1)),
            _const_spec(ma_fwd.shape), _const_spec(ma_inv.shape),
            _const_spec(mb_fwd.shape), _const_spec(mb_inv.shape),
        ],
        out_specs=pl.BlockSpec((1, l, cb), lambda c, bi: (bi, 0, c)),
        out_shape=jax.ShapeDtypeStruct((b, l, D_HYENA), BF16),
        scratch_shapes=[
            pltpu.VMEM((l, cb), F32),
            pltpu.VMEM((l, cb), F32),
            pltpu.VMEM((srows, cb), BF16),
        ],
        compiler_params=_params("arbitrary", "arbitrary"),
        name="hyena",
    )(u, u, u, conv_w, conv_w, conv_w, conv_b, conv_b, conv_b, dbias, hf, ma_fwd, ma_inv, mb_fwd, mb_inv)


ATT_TQ = MXU_DIM
ATT_KEY_CHUNK = 2 * MXU_DIM
ATT_BLOCKS = 4


def _attn_body(qc_ref, qn_ref, k_ref, kn_ref, kc_ref, kcn_ref, vt_ref, vct_ref, lam_ref, sg_ref, o_ref,
               sa_ref, sb_ref, ma_ref, mb_ref):
    tq = ATT_TQ
    lat = k_ref.shape[1]
    lv = lam_ref[...]
    lam = (jnp.exp(jnp.sum(lv[0:1] * lv[1:2], axis=-1, keepdims=True))
           - jnp.exp(jnp.sum(lv[2:3] * lv[3:4], axis=-1, keepdims=True)) + LAM_INIT)

    n_ctx = kc_ref.shape[1]
    chunks = [(False, r, r, ATT_KEY_CHUNK) for r in range(0, lat, ATT_KEY_CHUNK)] + [(True, 0, lat, n_ctx)]

    def stage(qt, keys_ref, ctx_keys_ref, s_new, m_new, s_old, m_old, hook=None):
        m_run = acc = denom = None
        feat = lax.broadcasted_iota(jnp.int32, qt.shape, 0)
        qz = jnp.concatenate([jnp.where(feat < HEAD_DIM, qt, jnp.zeros_like(qt)),
                              jnp.where(feat >= HEAD_DIM, qt, jnp.zeros_like(qt))], axis=1)
        if s_old is not None:
            m_prev = m_old[...]
        for n, (is_ctx, src, dst, rows) in enumerate(chunks):
            keys = ctx_keys_ref[0, src:src + rows, :] if is_ctx else keys_ref[0, src:src + rows, :]
            s = _dot(keys, qz)
            s_new[dst:dst + rows, :] = s
            cmax = jnp.max(s, axis=0, keepdims=True)
            m_run = cmax if m_run is None else jnp.maximum(m_run, cmax)
            if s_old is not None:
                p = jnp.exp2(s_old[dst:dst + rows, :] - m_prev)
                psum = jnp.sum(p, axis=0, keepdims=True)
                vals = vct_ref[0, :, src:src + rows] if is_ctx else vt_ref[0, :, src:src + rows]
                part = _dot(vals, p.astype(BF16))
                denom = psum if denom is None else denom + psum
                acc = part if acc is None else acc + part
            if n == 0 and hook is not None:
                hook()
        m_new[...] = m_run
        return acc, denom

    def finish(acc, denom):
        r = 1.0 / denom
        ot = acc[:, :tq] * r[:, :tq] - acc[:, tq:] * (lam * r[:, tq:])
        return (_rms(ot.T, sg_ref[...]) * (1.0 - LAM_INIT)).astype(BF16)

    @pl.when(pl.program_id(0) == 0)
    def _():
        stage(qc_ref[0, :, :tq], k_ref, kc_ref, sa_ref, ma_ref, None, None)

    bufs = ((sa_ref, ma_ref), (sb_ref, mb_ref))
    pending = None
    for n in range(ATT_BLOCKS):
        if n + 1 < ATT_BLOCKS:
            new = (qc_ref[0, :, (n + 1) * tq:(n + 2) * tq], k_ref, kc_ref)
        else:
            new = (qn_ref[0, :, :tq], kn_ref, kcn_ref)
        raw = stage(*new, *bufs[(n + 1) % 2], *bufs[n % 2], hook=pending)

        def pending(raw=raw, n=n):
            o_ref[0, n * tq:(n + 1) * tq, :] = finish(*raw)

    pending()


def _attn(qt, k, vt, kc, vct, lamv, subln_g):
    b, l, _ = k.shape
    c = kc.shape[1]
    dv = 2 * HEAD_DIM
    tq = ATT_TQ
    tstep = ATT_BLOCKS * tq
    assert l % tstep == 0 and ATT_BLOCKS % 2 == 0
    pairs = l // tstep
    steps = b * N_HEADS * pairs

    def split(j):
        return j // (N_HEADS * pairs), (j // pairs) % N_HEADS, j % pairs

    def cur(fn):
        return lambda j: fn(*split(j))

    def nxt(fn):
        return lambda j: fn(*split(jnp.minimum(j + 1, steps - 1)))

    q_map = lambda bi, h, i: (bi, h, i)
    k_map = lambda bi, h, i: (bi, 0, h)
    v_map = lambda bi, h, i: (bi, h, 0)
    return pl.pallas_call(
        _attn_body,
        grid=(steps,),
        in_specs=[
            pl.BlockSpec((1, dv, tstep), cur(q_map)), pl.BlockSpec((1, dv, tstep), nxt(q_map)),
            pl.BlockSpec((1, l, dv), cur(k_map)), pl.BlockSpec((1, l, dv), nxt(k_map)),
            pl.BlockSpec((1, c, dv), cur(k_map)), pl.BlockSpec((1, c, dv), nxt(k_map)),
            pl.BlockSpec((1, dv, l), cur(v_map)), pl.BlockSpec((1, dv, c), cur(v_map)),
            _const_spec(lamv.shape), _const_spec(subln_g.shape),
        ],
        out_specs=pl.BlockSpec((1, tstep, dv), cur(lambda bi, h, i: (bi, i, h))),
        out_shape=jax.ShapeDtypeStruct((b, l, N_HEADS * dv), BF16),
        scratch_shapes=[pltpu.VMEM((l + c, 2 * tq), F32), pltpu.VMEM((l + c, 2 * tq), F32),
                        pltpu.VMEM((1, 2 * tq), F32), pltpu.VMEM((1, 2 * tq), F32)],
        compiler_params=_params("arbitrary"),
        name="attn",
    )(qt, qt, k, k, kc, kc, vt, vct, lamv, subln_g)


def _post_body(x_ref, zh_ref, oa_ref, gate_ref, g1_ref, sh2_ref, sc2_ref, g2_ref,
               gpost_ref, gfpre_ref, gfpost_ref,
               whu_ref, wau_ref, wo_ref, wg_ref, wu_ref, wd_ref, o_ref, *, ff_chunks):
    d = x_ref.shape[2]
    y_hy = _dot(zh_ref[0], whu_ref[...])
    y_att = _dot(oa_ref[0], wau_ref[...])
    mixed = gate_ref[0, :, :d].astype(F32) * y_hy + gate_ref[0, :, d:].astype(F32) * y_att
    mixed = _dot(mixed.astype(BF16), wo_ref[...])
    x1 = x_ref[0] + g1_ref[0] * _rms(mixed, gpost_ref[...])
    hf = (_rms(x1, gfpre_ref[...]) * (1.0 + sc2_ref[0]) + sh2_ref[0]).astype(BF16)
    f = None
    for c0, cw in ff_chunks:
        a = _dot(hf, wg_ref[:, c0:c0 + cw])
        t = (a * jax.nn.sigmoid(a) * _dot(hf, wu_ref[:, c0:c0 + cw])).astype(BF16)
        part = _dot(t, wd_ref[c0:c0 + cw, :])
        f = part if f is None else f + part
    o_ref[0] = x1 + g2_ref[0] * _rms(f, gfpost_ref[...])


def _post(x, zh, oa, gate, g1, sh2, sc2, g2, gpost, gfpre, gfpost, whu, wau, wo, wg, wu, wd):
    b, l, d = x.shape
    dff = wg.shape[1]
    tm = 512
    ff_chunk = 4 * MXU_DIM
    ff_chunks = tuple((c0, min(ff_chunk, dff - c0)) for c0 in range(0, dff, ff_chunk))
    tile = lambda bi, i: (bi, i, 0)
    per_b = lambda bi, i: (bi, 0, 0)
    return pl.pallas_call(
        functools.partial(_post_body, ff_chunks=ff_chunks),
        grid=(b, l // tm),
        in_specs=[
            pl.BlockSpec((1, tm, d), tile),
            pl.BlockSpec((1, tm, zh.shape[2]), tile),
            pl.BlockSpec((1, tm, oa.shape[2]), tile),
            pl.BlockSpec((1, tm, gate.shape[2]), tile),
            pl.BlockSpec((1, 1, d), per_b), pl.BlockSpec((1, 1, d), per_b),
            pl.BlockSpec((1, 1, d), per_b), pl.BlockSpec((1, 1, d), per_b),
            _const_spec((1, d)), _const_spec((1, d)), _const_spec((1, d)),
        ] + [_const_spec(w.shape) for w in (whu, wau, wo, wg, wu, wd)],
        out_specs=pl.BlockSpec((1, tm, d), tile),
        out_shape=jax.ShapeDtypeStruct((b, l, d), F32),
        compiler_params=_params("arbitrary", "arbitrary"),
        name="post",
    )(x, zh, oa, gate, g1, sh2, sc2, g2, gpost, gfpre, gfpost, whu, wau, wo, wg, wu, wd)


@functools.lru_cache(maxsize=None)
def _rope_tables(l):
    quarter = HEAD_DIM // 4
    inv = ROPE_BASE ** (-np.arange(quarter, dtype=np.float64) / quarter)
    t = np.arange(l)
    ang_r = (t // GRID_W)[:, None] * inv[None, :]
    ang_c = (t % GRID_W)[:, None] * inv[None, :]
    ang = np.concatenate([ang_r, ang_r, ang_c, ang_c], axis=1)
    reps = D_ATTN // HEAD_DIM
    cos = np.tile(np.cos(ang), (1, reps)).astype(np.float32)
    second = (np.arange(D_ATTN) % (2 * quarter)) >= quarter
    sin = (np.tile(np.sin(ang), (1, reps)) * np.where(second, 1.0, -1.0)[None, :]).astype(np.float32)
    return cos, sin


@functools.lru_cache(maxsize=None)
def _filter_features(l, emb, kpad):
    bands = (emb - 1) // 2
    t = np.linspace(0.0, 1.0, l, dtype=np.float32).astype(np.float64)[:, None]
    w = (2.0 * math.pi / l) * np.arange(l, dtype=np.float64)[:, None]
    fr = np.linspace(1e-4, bands - 1, bands, dtype=np.float32).astype(np.float64)[None, :]
    z = np.concatenate([t, np.cos(fr * w), -np.sin(fr * w)], axis=-1)
    z = np.pad(z, ((0, 0), (0, kpad - emb))).astype(np.float32)
    deltas = np.abs(np.linspace(math.log(FILTER_TARGET) / SLOW_DECAY_PCT,
                                math.log(FILTER_TARGET) / FAST_DECAY_PCT, D_HYENA, dtype=np.float32))
    return z, deltas[None, :].astype(np.float32)


def _pad_to(a, rows, cols, value=0.0):
    return jnp.pad(a, ((0, rows - a.shape[0]), (0, cols - a.shape[1])), constant_values=value)


def kernel(x, c, ctx, c_ctx, w_ada, b_ada, g_mix_pre, g_mix_post, g_ffn_pre, g_ffn_post, w_in, hy_conv_w, hy_conv_b, hy_f_w1, hy_f_b1, hy_f_w2, hy_f_b2, hy_f_w3, hy_f_b3, hy_f_freq, hy_bias, lambda_q1, lambda_k1, lambda_q2, lambda_k2, att_subln_g, w_hy_up, w_att_up, w_out, w_ffn_gate, w_ffn_up, w_ffn_down):
    b, l, d = x.shape
    n_ctx = ctx.shape[1]
    order = hy_bias.shape[1]
    off_q = (order + 1) * D_HYENA
    off_k = off_q + D_ATTN
    off_v = off_k + D_ATTN
    off_g = off_v + D_ATTN
    assert w_in.shape[0] == 1, "single-layer block"
    assert l % CONV_ROWS == 0 and l % 512 == 0

    pad_rows = -(-(b + 1) // 8) * 8
    cc = jnp.concatenate([c, c_ctx[None, :], jnp.zeros((pad_rows - b - 1, d), F32)], axis=0)
    ada = _ada(cc, w_ada[0], b_ada[0][None, :])
    sh1, sc1, g1, sh2, sc2, g2 = (ada[:b, i * d:(i + 1) * d][:, None, :] for i in range(6))
    csh1, csc1 = ada[b:b + 1, 0:d], ada[b:b + 1, d:2 * d]

    row = lambda v: v[0][None, :]
    w = w_in[0].astype(BF16)
    cos, sin = _rope_tables(l)
    q_scale = HEAD_DIM ** -0.5 * math.log2(math.e)
    wqt = (w_in[0][:, off_q:off_k] * q_scale).astype(BF16).T
    wk = w[:, off_k:off_v]
    wvt = w[:, off_v:off_g].T

    kc, vct = _ctx_kv(ctx, row(g_mix_pre), csh1, csc1, wk, wvt)

    u, qt, k, vt, gate = _proj(x, row(g_mix_pre), sh1, sc1, jnp.asarray(cos), jnp.asarray(sin),
                               jnp.asarray(cos.T), jnp.asarray(sin.T),
                               w[:, :off_q], wqt, wk, wvt, w[:, off_g:])

    consts = tuple(jnp.asarray(a).astype(BF16) for a in _fft_constants(l))
    zfeat, deltas = _filter_features(l, hy_f_w1.shape[1], LANES)
    hf = _filt(jnp.asarray(zfeat),
               _pad_to(hy_f_w1[0], LANES, LANES), _pad_to(row(hy_f_b1), 1, LANES),
               _pad_to(hy_f_w2[0], LANES, LANES), _pad_to(row(hy_f_b2), 1, LANES),
               _pad_to(row(hy_f_freq), 1, LANES, 1.0),
               _pad_to(hy_f_w3[0], LANES, hy_f_w3.shape[2]), row(hy_f_b3),
               jnp.asarray(deltas), consts[0], consts[4], order, l)

    zh = _hyena(u, hy_conv_w[0], row(hy_conv_b), hy_bias[0], hf, consts[:4], order)

    lamv = jnp.concatenate([lambda_q1, lambda_k1, lambda_q2, lambda_k2], axis=0)
    oa = _attn(qt, k, vt, kc, vct, lamv, row(att_subln_g))

    return _post(x, zh, oa, gate, g1, sh2, sc2, g2, row(g_mix_post), row(g_ffn_pre), row(g_ffn_post),
                 w_hy_up[0].astype(BF16), w_att_up[0].astype(BF16), w_out[0].astype(BF16),
                 w_ffn_gate[0].astype(BF16), w_ffn_up[0].astype(BF16), w_ffn_down[0].astype(BF16))
```

```python
import functools
import math

import numpy as np
import jax
import jax.numpy as jnp
from jax import lax
from jax.experimental import pallas as pl
from jax.experimental.pallas import tpu as pltpu

F32 = jnp.float32
BF16 = jnp.bfloat16
HIGHEST = lax.Precision.HIGHEST

EPS = 1e-6
GRID_W = 64
HEAD_DIM = 64
N_HEADS = 4
D_HYENA = 512
D_ATTN = N_HEADS * 2 * HEAD_DIM
ROPE_BASE = 10000.0
FILTER_TARGET = 1e-2
FAST_DECAY_PCT = 0.3
SLOW_DECAY_PCT = 1.5
LAM_INIT = 0.8 - 0.6 * math.exp(0.0)

LANES = 128
MXU_DIM = 256
VMEM_LIMIT_BYTES = 56 * 1024 * 1024

FFT_N2 = 256
FFT_GROUP = 16
HY_CB = 256
FFT_GROUP_UNROLL = 8


def _dot(a, b):
    return jnp.dot(a, b, preferred_element_type=F32)


def _dot_hi(a, b):
    return jnp.dot(a, b, preferred_element_type=F32, precision=HIGHEST)


def _dot_nt(a, b):
    return lax.dot_general(a, b, (((1,), (1,)), ((), ())), preferred_element_type=F32)


def _rms(x, g):
    return x * lax.rsqrt(jnp.mean(x * x, axis=-1, keepdims=True) + EPS) * g


def _const_spec(shape):
    zeros = (0,) * len(shape)
    return pl.BlockSpec(shape, lambda *_: zeros, pipeline_mode=pl.Buffered(1))


def _params(*semantics):
    return pltpu.CompilerParams(dimension_semantics=semantics, vmem_limit_bytes=VMEM_LIMIT_BYTES)


def _ada_body(c_ref, w_ref, b_ref, o_ref):
    c = c_ref[...]
    o_ref[...] = _dot_hi(c * jax.nn.sigmoid(c), w_ref[...]) + b_ref[...]


def _ada(cc, w_ada, b_ada):
    rows, d = cc.shape
    n = w_ada.shape[1]
    tn = 1024
    return pl.pallas_call(
        _ada_body,
        grid=(n // tn,),
        in_specs=[
            pl.BlockSpec((rows, d), lambda j: (0, 0)),
            pl.BlockSpec((d, tn), lambda j: (0, j)),
            pl.BlockSpec((1, tn), lambda j: (0, j)),
        ],
        out_specs=pl.BlockSpec((rows, tn), lambda j: (0, j)),
        out_shape=jax.ShapeDtypeStruct((rows, n), F32),
        compiler_params=_params("arbitrary"),
        name="ada",
    )(cc, w_ada, b_ada)


def _ctx_kv_body(x_ref, g_ref, sh_ref, sc_ref, wk_ref, wvt_ref, k_ref, vt_ref):
    h = _rms(x_ref[0], g_ref[...]) * (1.0 + sc_ref[...]) + sh_ref[...]
    hb = h.astype(BF16)
    k_ref[0] = _dot(hb, wk_ref[...]).astype(BF16)
    vt_ref[0] = _dot_nt(wvt_ref[...], hb).astype(BF16)


def _ctx_kv(ctx, g, sh, sc, wk, wvt):
    b, n_ctx, d = ctx.shape
    n = wk.shape[1]
    return pl.pallas_call(
        _ctx_kv_body,
        grid=(b,),
        in_specs=[
            pl.BlockSpec((1, n_ctx, d), lambda i: (i, 0, 0)),
            _const_spec((1, d)), _const_spec((1, d)), _const_spec((1, d)),
            _const_spec((d, n)), _const_spec((n, d)),
        ],
        out_specs=[pl.BlockSpec((1, n_ctx, n), lambda i: (i, 0, 0)),
                   pl.BlockSpec((1, n, n_ctx), lambda i: (i, 0, 0))],
        out_shape=[jax.ShapeDtypeStruct((b, n_ctx, n), BF16), jax.ShapeDtypeStruct((b, n, n_ctx), BF16)],
        compiler_params=_params("arbitrary"),
        name="ctx_kv",
    )(ctx, g, sh, sc, wk, wvt)


ROT = HEAD_DIM // 4


def _proj_body(x_ref, g_ref, sh_ref, sc_ref, cos_ref, sin_ref, cost_ref, sint_ref,
               wu_ref, wqt_ref, wk_ref, wvt_ref, wg_ref,
               u_ref, qt_ref, k_ref, vt_ref, gate_ref):
    half = x_ref.shape[1] // 2
    rows_a, rows_b = slice(0, half), slice(half, 2 * half)

    def normed(rows):
        return (_rms(x_ref[0, rows, :], g_ref[...]) * (1.0 + sc_ref[0]) + sh_ref[0]).astype(BF16)

    def hyena_in(rows, hb):
        u_ref[0, rows, :] = _dot(hb, wu_ref[...]).astype(BF16)

    def gates_and_keys(rows, hb):
        gate_ref[0, rows, :] = jax.nn.sigmoid(_dot(hb, wg_ref[...])).astype(BF16)
        kv = _dot(hb, wk_ref[...])
        lane = lax.broadcasted_iota(jnp.int32, (kv.shape[0], LANES), 1)
        first = (lane % (2 * ROT)) < ROT
        slabs = []
        for c0 in range(0, kv.shape[1], LANES):
            slab = kv[:, c0:c0 + LANES]
            slabs.append(jnp.where(first, pltpu.roll(slab, LANES - ROT, axis=1), pltpu.roll(slab, ROT, axis=1)))
        k_ref[0, rows, :] = (kv * cos_ref[rows, :]
                             + jnp.concatenate(slabs, axis=1) * sin_ref[rows, :]).astype(BF16)

    hb_a = normed(rows_a)
    hyena_in(rows_a, hb_a)
    hb_b = normed(rows_b)
    gates_and_keys(rows_a, hb_a)
    hyena_in(rows_b, hb_b)
    gates_and_keys(rows_b, hb_b)
    hb = jnp.concatenate([hb_a, hb_b], axis=0)
    qv = _dot_nt(wqt_ref[...], hb)
    partner = jnp.concatenate(
        [qv[r + ROT:r + 2 * ROT] if part == 0 else qv[r:r + ROT]
         for r in range(0, qv.shape[0], 2 * ROT) for part in (0, 1)], axis=0)
    qt_ref[0] = (qv * cost_ref[...] + partner * sint_ref[...]).astype(BF16)
    vt_ref[0] = _dot_nt(wvt_ref[...], hb).astype(BF16)


def _proj(x, g, sh, sc, cos, sin, cost, sint, wu, wqt, wk, wvt, wg):
    b, l, d = x.shape
    tm = 512
    tile = lambda i, bi: (bi, i, 0)
    tile_t = lambda i, bi: (bi, 0, i)
    per_b = lambda i, bi: (bi, 0, 0)
    bf = lambda shape: jax.ShapeDtypeStruct(shape, BF16)
    n_u, n_g = wu.shape[1], wg.shape[1]
    return pl.pallas_call(
        _proj_body,
        grid=(l // tm, b),
        in_specs=[
            pl.BlockSpec((1, tm, d), tile),
            _const_spec((1, d)),
            pl.BlockSpec((1, 1, d), per_b), pl.BlockSpec((1, 1, d), per_b),
            pl.BlockSpec((tm, D_ATTN), lambda i, bi: (i, 0)), pl.BlockSpec((tm, D_ATTN), lambda i, bi: (i, 0)),
            pl.BlockSpec((D_ATTN, tm), lambda i, bi: (0, i)), pl.BlockSpec((D_ATTN, tm), lambda i, bi: (0, i)),
        ] + [_const_spec(w.shape) for w in (wu, wqt, wk, wvt, wg)],
        out_specs=[pl.BlockSpec((1, tm, n_u), tile), pl.BlockSpec((1, D_ATTN, tm), tile_t),
                   pl.BlockSpec((1, tm, D_ATTN), tile), pl.BlockSpec((1, D_ATTN, tm), tile_t),
                   pl.BlockSpec((1, tm, n_g), tile)],
        out_shape=[bf((b, l, n_u)), bf((b, D_ATTN, l)), bf((b, l, D_ATTN)), bf((b, D_ATTN, l)), bf((b, l, n_g))],
        compiler_params=_params("arbitrary", "arbitrary"),
        name="proj",
    )(x, g, sh, sc, cos, sin, cost, sint, wu, wqt, wk, wvt, wg)


def _fft_dims(l):
    n = 2 * l
    n1 = n // FFT_N2
    h1 = n1 // 2
    k1 = h1 + 1
    ng = FFT_N2 // FFT_GROUP
    return n, n1, h1, k1, ng


@functools.lru_cache(maxsize=None)
def _fft_constants(l):
    n, n1, h1, k1n, ng = _fft_dims(l)
    g = FFT_GROUP
    ma = np.zeros((ng, k1n, 2, g, h1, g), np.float64)
    for gi in range(ng):
        for j in range(g):
            n2 = gi * g + j
            nn = FFT_N2 * np.arange(h1)[None, :] + n2
            theta = 2.0 * np.pi * ((np.arange(k1n)[:, None] * nn) % n) / n
            ma[gi, :, 0, j, :, j] = np.cos(theta)
            ma[gi, :, 1, j, :, j] = -np.sin(theta)
    weight = np.full((k1n,), 2.0)
    weight[0] = weight[-1] = 1.0
    ma_fwd = ma.reshape(ng, k1n * 2 * g, h1 * g)
    ma_inv = (ma * (weight / n)[None, :, None, None, None, None]).reshape(ng, k1n * 2 * g, h1 * g)
    ma_inv = np.transpose(ma_inv, (0, 2, 1))
    kpad = -(-ma_inv.shape[2] // LANES) * LANES
    ma_inv = np.pad(ma_inv, ((0, 0), (0, 0), (0, kpad - ma_inv.shape[2])))
    ang = 2.0 * np.pi * ((np.arange(FFT_N2)[:, None] * np.arange(FFT_N2)[None, :]) % FFT_N2) / FFT_N2
    fr, fi = np.cos(ang), -np.sin(ang)
    mb_fwd = np.block([[fr, -fi], [fi, fr]])
    mb_inv = np.block([[fr, fi], [-fi, fr]])
    mb_filt = np.block([[fr, -fi, fr, -fi], [fi, fr, -fi, -fr]])
    return tuple(np.asarray(a, np.float32) for a in (ma_fwd, ma_inv, mb_fwd, mb_inv, mb_filt))


def _fft_stage_a(src_ref, ma_ref, spec_ref, l):
    _, _, h1, k1n, ng = _fft_dims(l)
    g = FFT_GROUP

    def group(gi, carry):
        off = gi * g
        xg = jnp.concatenate(
            [src_ref[pl.ds(pl.multiple_of(FFT_N2 * i + off, g), g), :] for i in range(h1)], axis=0)
        bg = _dot(ma_ref[gi], xg.astype(BF16)).astype(BF16)
        for c in range(2 * k1n):
            spec_ref[pl.ds(pl.multiple_of(c * FFT_N2 + off, g), g), :] = bg[c * g:(c + 1) * g]
        return carry

    lax.fori_loop(0, ng, group, 0, unroll=FFT_GROUP_UNROLL)


def _fft_stage_a_inv(spec_ref, ma_inv_ref, dst_ref, l):
    _, _, h1, k1n, ng = _fft_dims(l)
    g = FFT_GROUP
    kpad = ma_inv_ref.shape[2] - 2 * k1n * g

    def group(gi, carry):
        off = gi * g
        chunks = [spec_ref[pl.ds(pl.multiple_of(c * FFT_N2 + off, g), g), :] for c in range(2 * k1n)]
        if kpad:
            chunks.append(jnp.zeros((kpad, spec_ref.shape[1]), BF16))
        yg = _dot(ma_inv_ref[gi], jnp.concatenate(chunks, axis=0))
        for i in range(h1):
            dst_ref[pl.ds(pl.multiple_of(FFT_N2 * i + off, g), g), :] = yg[i * g:(i + 1) * g]
        return carry

    lax.fori_loop(0, ng, group, 0, unroll=FFT_GROUP_UNROLL)


def _filt_body(z_ref, w1_ref, b1_ref, w2_ref, b2_ref, fq_ref, w3f_ref, b3f_ref, w3b_ref, b3b_ref,
               dl_ref, ma_ref, mb_ref, hf_ref, hdn_ref, sig_ref, specf_ref, specb_ref, *, l):
    _, _, _, k1n, _ = _fft_dims(l)

    @pl.when((pl.program_id(0) == 0) & (pl.program_id(1) == 0))
    def _():
        fq = fq_ref[...]
        h1 = jnp.sin(fq * (_dot_hi(z_ref[...], w1_ref[...]) + b1_ref[...]))
        hdn_ref[...] = jnp.sin(fq * (_dot_hi(h1, w2_ref[...]) + b2_ref[...])).astype(BF16)

    hdn = hdn_ref[...]
    cb = specf_ref.shape[1]
    blk = 2 * FFT_N2
    rows = lax.broadcasted_iota(jnp.int32, (l, cb), 0)
    window = jnp.exp(rows.astype(F32) * (-1.0 / (l - 1)) * dl_ref[...])

    sig_ref[...] = (_dot(hdn, w3f_ref[...].astype(BF16)) + b3f_ref[...]) * window
    _fft_stage_a(sig_ref, ma_ref, specf_ref, l)
    sig_ref[...] = jnp.where(rows > 0, (_dot(hdn, w3b_ref[...].astype(BF16)) + b3b_ref[...]) * window, 0.0)
    _fft_stage_a(sig_ref, ma_ref, specb_ref, l)

    def blocks(starts):
        x = jnp.concatenate(
            [jnp.concatenate([specf_ref[pl.ds(st, blk), :], specb_ref[pl.ds(st, blk), :]], axis=0)
             for st in starts], axis=1)
        s = _dot(mb_ref[...], x).astype(BF16)
        for n, st in enumerate(starts):
            hf_ref[0, pl.ds(st, blk), :] = s[:, n * cb:(n + 1) * cb]

    def per_pair(i, carry):
        st = pl.multiple_of(i * 2 * blk, blk)
        blocks([st, st + blk])
        return carry

    lax.fori_loop(0, k1n // 2, per_pair, 0)
    if k1n % 2:
        blocks([(k1n - 1) * blk])


def _filt(zfeat, w1, b1, w2, b2, fq, w3, b3, deltas, ma_fwd, mb_filt, order, l):
    _, _, _, k1n, _ = _fft_dims(l)
    cb = HY_CB
    nblk = D_HYENA // cb
    srows = k1n * 2 * FFT_N2
    fcol = lambda o, c: (0, (2 * o) * nblk + c)
    bcol = lambda o, c: (0, (2 * o + 1) * nblk + c)
    return pl.pallas_call(
        functools.partial(_filt_body, l=l),
        grid=(order, nblk),
        in_specs=[
            _const_spec(zfeat.shape), _const_spec(w1.shape), _const_spec(b1.shape),
            _const_spec(w2.shape), _const_spec(b2.shape), _const_spec(fq.shape),
            pl.BlockSpec((w3.shape[0], cb), fcol), pl.BlockSpec((1, cb), fcol),
            pl.BlockSpec((w3.shape[0], cb), bcol), pl.BlockSpec((1, cb), bcol),
            pl.BlockSpec((1, cb), lambda o, c: (0, c)),
            _const_spec(ma_fwd.shape), _const_spec(mb_filt.shape),
        ],
        out_specs=pl.BlockSpec((1, srows, cb), lambda o, c: (o, 0, c)),
        out_shape=jax.ShapeDtypeStruct((order, srows, D_HYENA), BF16),
        scratch_shapes=[
            pltpu.VMEM((l, w2.shape[1]), BF16),
            pltpu.VMEM((l, cb), F32),
            pltpu.VMEM((srows, cb), BF16),
            pltpu.VMEM((srows, cb), BF16),
        ],
        compiler_params=_params("arbitrary", "arbitrary"),
        name="filt",
    )(zfeat, w1, b1, w2, b2, fq, w3, b3, w3, b3, deltas, ma_fwd, mb_filt)


CONV_ROWS = 512
HALO = 16


def _short_conv_chunk(u_ref, w_ref, b_ref, r0, l):
    c = u_ref.shape[2]
    zeros = jnp.zeros((HALO, c), F32)
    before = zeros if r0 == 0 else u_ref[0, r0 - HALO:r0, :].astype(F32)
    after = zeros if r0 + CONV_ROWS == l else u_ref[0, r0 + CONV_ROWS:r0 + CONV_ROWS + HALO, :].astype(F32)
    ext = jnp.concatenate([before, u_ref[0, r0:r0 + CONV_ROWS, :].astype(F32), after], axis=0)
    w = w_ref[...]
    n = ext.shape[0]
    prev = pltpu.roll(ext, 1, axis=0)[HALO:HALO + CONV_ROWS]
    nxt = pltpu.roll(ext, n - 1, axis=0)[HALO:HALO + CONV_ROWS]
    return prev * w[0:1] + ext[HALO:HALO + CONV_ROWS] * w[1:2] + nxt * w[2:3] + b_ref[...]


def _hyena_body(u0_ref, u1_ref, u2_ref, w0_ref, w1_ref, w2_ref, b0_ref, b1_ref, b2_ref, db_ref, hf_ref,
                ma_ref, mai_ref, mb_ref, mbi_ref, o_ref, z_ref, y_ref, spec_ref, *, l, order):
    _, _, _, k1n, _ = _fft_dims(l)
    blk = 2 * FFT_N2
    gates = ((u1_ref, w1_ref, b1_ref), (u2_ref, w2_ref, b2_ref))

    for r0 in range(0, l, CONV_ROWS):
        z_ref[r0:r0 + CONV_ROWS, :] = _short_conv_chunk(u0_ref, w0_ref, b0_ref, r0, l)

    for o in range(order):
        _fft_stage_a(z_ref, ma_ref, spec_ref, l)

        cb = spec_ref.shape[1]
        groups = [list(range(i, min(i + 2, k1n))) for i in range(0, k1n, 2)]

        def stage_b(ks):
            return _dot(mb_ref[...], jnp.concatenate([spec_ref[k * blk:(k + 1) * blk, :] for k in ks], axis=1))

        def filter_and_invert(ks, s, o=o):
            h = jnp.concatenate([hf_ref[o, k * blk:(k + 1) * blk, :] for k in ks], axis=1).astype(F32)
            sr, si, hr, hi = s[:FFT_N2], s[FFT_N2:], h[:FFT_N2], h[FFT_N2:]
            y = jnp.concatenate([sr * hr - si * hi, sr * hi + si * hr], axis=0).astype(BF16)
            c = _dot(mbi_ref[...], y).astype(BF16)
            for n, k in enumerate(ks):
                spec_ref[k * blk:(k + 1) * blk, :] = c[:, n * cb:(n + 1) * cb]

        s_next = stage_b(groups[0])
        for gi, ks in enumerate(groups):
            s_cur = s_next
            if gi + 1 < len(groups):
                s_next = stage_b(groups[gi + 1])
            filter_and_invert(ks, s_cur)
        _fft_stage_a_inv(spec_ref, mai_ref, y_ref, l)

        u_ref, w_ref, b_ref = gates[o]
        dbias = db_ref[o:o + 1, :]
        for r0 in range(0, l, CONV_ROWS):
            rows = slice(r0, r0 + CONV_ROWS)
            znew = _short_conv_chunk(u_ref, w_ref, b_ref, r0, l) * (y_ref[rows, :] + z_ref[rows, :] * dbias)
            if o + 1 < order:
                z_ref[rows, :] = znew
            else:
                o_ref[0, rows, :] = znew.astype(BF16)


def _hyena(u, conv_w, conv_b, dbias, hf, consts, order):
    b, l, _ = u.shape
    _, _, _, k1n, _ = _fft_dims(l)
    cb = HY_CB
    nblk = D_HYENA // cb
    srows = k1n * 2 * FFT_N2
    ma_fwd, ma_inv, mb_fwd, mb_inv = consts
    upart = lambda p: pl.BlockSpec((1, l, cb), lambda c, bi: (bi, 0, p * nblk + c))
    wpart = lambda p: pl.BlockSpec((3, cb), lambda c, bi: (0, p * nblk + c))
    bpart = lambda p: pl.BlockSpec((1, cb), lambda c, bi: (0, p * nblk + c))
    return pl.pallas_call(
        functools.partial(_hyena_body, l=l, order=order),
        grid=(nblk, b),
        in_specs=[
            upart(0), upart(1), upart(2), wpart(0), wpart(1), wpart(2), bpart(0), bpart(1), bpart(2),
            pl.BlockSpec((order, cb), lambda c, bi: (0, c)),
            pl.BlockSpec((order, srows, cb), lambda c, bi: (0, 0, c), pipeline_mode=pl.Buffered(1)),
            _const_spec(ma_fwd.shape), _const_spec(ma_inv.shape),
            _const_spec(mb_fwd.shape), _const_spec(mb_inv.shape),
        ],
        out_specs=pl.BlockSpec((1, l, cb), lambda c, bi: (bi, 0, c)),
        out_shape=jax.ShapeDtypeStruct((b, l, D_HYENA), BF16),
        scratch_shapes=[
            pltpu.VMEM((l, cb), F32),
            pltpu.VMEM((l, cb), F32),
            pltpu.VMEM((srows, cb), BF16),
        ],
        compiler_params=_params("arbitrary", "arbitrary"),
        name="hyena",
    )(u, u, u, conv_w, conv_w, conv_w, conv_b, conv_b, conv_b, dbias, hf, ma_fwd, ma_inv, mb_fwd, mb_inv)


ATT_TQ = MXU_DIM
ATT_KEY_CHUNK = 2 * MXU_DIM
ATT_BLOCKS = 4


def _attn_body(qc_ref, qn_ref, k_ref, kn_ref, kc_ref, kcn_ref, vt_ref, vct_ref, lam_ref, sg_ref, o_ref,
               sa_ref, sb_ref, ma_ref, mb_ref):
    tq = ATT_TQ
    lat = k_ref.shape[1]
    lv = lam_ref[...]
    lam = (jnp.exp(jnp.sum(lv[0:1] * lv[1:2], axis=-1, keepdims=True))
           - jnp.exp(jnp.sum(lv[2:3] * lv[3:4], axis=-1, keepdims=True)) + LAM_INIT)

    n_ctx = kc_ref.shape[1]
    chunks = [(False, r, r, ATT_KEY_CHUNK) for r in range(0, lat, ATT_KEY_CHUNK)] + [(True, 0, lat, n_ctx)]

    def stage(qt, keys_ref, ctx_keys_ref, s_new, m_new, s_old, m_old, hook=None):
        m_run = acc = denom = None
        feat = lax.broadcasted_iota(jnp.int32, qt.shape, 0)
        qz = jnp.concatenate([jnp.where(feat < HEAD_DIM, qt, jnp.zeros_like(qt)),
                              jnp.where(feat >= HEAD_DIM, qt, jnp.zeros_like(qt))], axis=1)
        if s_old is not None:
            m_prev = m_old[...]
        for n, (is_ctx, src, dst, rows) in enumerate(chunks):
            keys = ctx_keys_ref[0, src:src + rows, :] if is_ctx else keys_ref[0, src:src + rows, :]
            s = _dot(keys, qz)
            s_new[dst:dst + rows, :] = s
            cmax = jnp.max(s, axis=0, keepdims=True)
            m_run = cmax if m_run is None else jnp.maximum(m_run, cmax)
            if s_old is not None:
                p = jnp.exp2(s_old[dst:dst + rows, :] - m_prev)
                psum = jnp.sum(p, axis=0, keepdims=True)
                vals = vct_ref[0, :, src:src + rows] if is_ctx else vt_ref[0, :, src:src + rows]
                part = _dot(vals, p.astype(BF16))
                denom = psum if denom is None else denom + psum
                acc = part if acc is None else acc + part
            if n == 0 and hook is not None:
                hook()
        m_new[...] = m_run
        return acc, denom

    def finish(acc, denom):
        r = 1.0 / denom
        ot = acc[:, :tq] * r[:, :tq] - acc[:, tq:] * (lam * r[:, tq:])
        return (_rms(ot.T, sg_ref[...]) * (1.0 - LAM_INIT)).astype(BF16)

    @pl.when(pl.program_id(0) == 0)
    def _():
        stage(qc_ref[0, :, :tq], k_ref, kc_ref, sa_ref, ma_ref, None, None)

    bufs = ((sa_ref, ma_ref), (sb_ref, mb_ref))
    pending = None
    for n in range(ATT_BLOCKS):
        if n + 1 < ATT_BLOCKS:
            new = (qc_ref[0, :, (n + 1) * tq:(n + 2) * tq], k_ref, kc_ref)
        else:
            new = (qn_ref[0, :, :tq], kn_ref, kcn_ref)
        raw = stage(*new, *bufs[(n + 1) % 2], *bufs[n % 2], hook=pending)

        def pending(raw=raw, n=n):
            o_ref[0, n * tq:(n + 1) * tq, :] = finish(*raw)

    pending()


def _attn(qt, k, vt, kc, vct, lamv, subln_g):
    b, l, _ = k.shape
    c = kc.shape[1]
    dv = 2 * HEAD_DIM
    tq = ATT_TQ
    tstep = ATT_BLOCKS * tq
    assert l % tstep == 0 and ATT_BLOCKS % 2 == 0
    pairs = l // tstep
    steps = b * N_HEADS * pairs

    def split(j):
        return j // (N_HEADS * pairs), (j // pairs) % N_HEADS, j % pairs

    def cur(fn):
        return lambda j: fn(*split(j))

    def nxt(fn):
        return lambda j: fn(*split(jnp.minimum(j + 1, steps - 1)))

    q_map = lambda bi, h, i: (bi, h, i)
    k_map = lambda bi, h, i: (bi, 0, h)
    v_map = lambda bi, h, i: (bi, h, 0)
    return pl.pallas_call(
        _attn_body,
        grid=(steps,),
        in_specs=[
            pl.BlockSpec((1, dv, tstep), cur(q_map)), pl.BlockSpec((1, dv, tstep), nxt(q_map)),
            pl.BlockSpec((1, l, dv), cur(k_map)), pl.BlockSpec((1, l, dv), nxt(k_map)),
            pl.BlockSpec((1, c, dv), cur(k_map)), pl.BlockSpec((1, c, dv), nxt(k_map)),
            pl.BlockSpec((1, dv, l), cur(v_map)), pl.BlockSpec((1, dv, c), cur(v_map)),
            _const_spec(lamv.shape), _const_spec(subln_g.shape),
        ],
        out_specs=pl.BlockSpec((1, tstep, dv), cur(lambda bi, h, i: (bi, i, h))),
        out_shape=jax.ShapeDtypeStruct((b, l, N_HEADS * dv), BF16),
        scratch_shapes=[pltpu.VMEM((l + c, 2 * tq), F32), pltpu.VMEM((l + c, 2 * tq), F32),
                        pltpu.VMEM((1, 2 * tq), F32), pltpu.VMEM((1, 2 * tq), F32)],
        compiler_params=_params("arbitrary"),
        name="attn",
    )(qt, qt, k, k, kc, kc, vt, vct, lamv, subln_g)


def _post_body(x_ref, zh_ref, oa_ref, gate_ref, g1_ref, sh2_ref, sc2_ref, g2_ref,
               gpost_ref, gfpre_ref, gfpost_ref,
               whu_ref, wau_ref, wo_ref, wg_ref, wu_ref, wd_ref, o_ref, *, ff_chunks):
    d = x_ref.shape[2]
    half = x_ref.shape[1] // 2
    rows_a, rows_b = slice(0, half), slice(half, 2 * half)

    def lifts(rows):
        return _dot(zh_ref[0, rows, :], whu_ref[...]), _dot(oa_ref[0, rows, :], wau_ref[...])

    def out_proj(rows, y_hy, y_att):
        mixed = gate_ref[0, rows, :d].astype(F32) * y_hy + gate_ref[0, rows, d:].astype(F32) * y_att
        return _dot(mixed.astype(BF16), wo_ref[...])

    def residual_and_norm(rows, mixed):
        x1 = x_ref[0, rows, :] + g1_ref[0] * _rms(mixed, gpost_ref[...])
        return x1, (_rms(x1, gfpre_ref[...]) * (1.0 + sc2_ref[0]) + sh2_ref[0]).astype(BF16)

    def ffn_chunk(hf, f, chunk):
        c0, cw = chunk
        a = _dot(hf, wg_ref[:, c0:c0 + cw])
        t = (a * jax.nn.sigmoid(a) * _dot(hf, wu_ref[:, c0:c0 + cw])).astype(BF16)
        part = _dot(t, wd_ref[c0:c0 + cw, :])
        return part if f is None else f + part

    def finish(rows, x1, f):
        o_ref[0, rows, :] = x1 + g2_ref[0] * _rms(f, gfpost_ref[...])

    mixed_a = out_proj(rows_a, *lifts(rows_a))
    lifted_b = lifts(rows_b)
    x1_a, hf_a = residual_and_norm(rows_a, mixed_a)
    mixed_b = out_proj(rows_b, *lifted_b)
    f_a = ffn_chunk(hf_a, None, ff_chunks[0])
    x1_b, hf_b = residual_and_norm(rows_b, mixed_b)
    for chunk in ff_chunks[1:]:
        f_a = ffn_chunk(hf_a, f_a, chunk)
    f_b = ffn_chunk(hf_b, None, ff_chunks[0])
    finish(rows_a, x1_a, f_a)
    for chunk in ff_chunks[1:]:
        f_b = ffn_chunk(hf_b, f_b, chunk)
    finish(rows_b, x1_b, f_b)


def _post(x, zh, oa, gate, g1, sh2, sc2, g2, gpost, gfpre, gfpost, whu, wau, wo, wg, wu, wd):
    b, l, d = x.shape
    dff = wg.shape[1]
    tm = 512
    ff_chunk = 4 * MXU_DIM
    ff_chunks = tuple((c0, min(ff_chunk, dff - c0)) for c0 in range(0, dff, ff_chunk))
    tile = lambda bi, i: (bi, i, 0)
    per_b = lambda bi, i: (bi, 0, 0)
    return pl.pallas_call(
        functools.partial(_post_body, ff_chunks=ff_chunks),
        grid=(b, l // tm),
        in_specs=[
            pl.BlockSpec((1, tm, d), tile),
            pl.BlockSpec((1, tm, zh.shape[2]), tile),
            pl.BlockSpec((1, tm, oa.shape[2]), tile),
            pl.BlockSpec((1, tm, gate.shape[2]), tile),
            pl.BlockSpec((1, 1, d), per_b), pl.BlockSpec((1, 1, d), per_b),
            pl.BlockSpec((1, 1, d), per_b), pl.BlockSpec((1, 1, d), per_b),
            _const_spec((1, d)), _const_spec((1, d)), _const_spec((1, d)),
        ] + [_const_spec(w.shape) for w in (whu, wau, wo, wg, wu, wd)],
        out_specs=pl.BlockSpec((1, tm, d), tile),
        out_shape=jax.ShapeDtypeStruct((b, l, d), F32),
        compiler_params=_params("arbitrary", "arbitrary"),
        name="post",
    )(x, zh, oa, gate, g1, sh2, sc2, g2, gpost, gfpre, gfpost, whu, wau, wo, wg, wu, wd)


@functools.lru_cache(maxsize=None)
def _rope_tables(l):
    quarter = HEAD_DIM // 4
    inv = ROPE_BASE ** (-np.arange(quarter, dtype=np.float64) / quarter)
    t = np.arange(l)
    ang_r = (t // GRID_W)[:, None] * inv[None, :]
    ang_c = (t % GRID_W)[:, None] * inv[None, :]
    ang = np.concatenate([ang_r, ang_r, ang_c, ang_c], axis=1)
    reps = D_ATTN // HEAD_DIM
    cos = np.tile(np.cos(ang), (1, reps)).astype(np.float32)
    second = (np.arange(D_ATTN) % (2 * quarter)) >= quarter
    sin = (np.tile(np.sin(ang), (1, reps)) * np.where(second, 1.0, -1.0)[None, :]).astype(np.float32)
    return cos, sin


@functools.lru_cache(maxsize=None)
def _filter_features(l, emb, kpad):
    bands = (emb - 1) // 2
    t = np.linspace(0.0, 1.0, l, dtype=np.float32).astype(np.float64)[:, None]
    w = (2.0 * math.pi / l) * np.arange(l, dtype=np.float64)[:, None]
    fr = np.linspace(1e-4, bands - 1, bands, dtype=np.float32).astype(np.float64)[None, :]
    z = np.concatenate([t, np.cos(fr * w), -np.sin(fr * w)], axis=-1)
    z = np.pad(z, ((0, 0), (0, kpad - emb))).astype(np.float32)
    deltas = np.abs(np.linspace(math.log(FILTER_TARGET) / SLOW_DECAY_PCT,
                                math.log(FILTER_TARGET) / FAST_DECAY_PCT, D_HYENA, dtype=np.float32))
    return z, deltas[None, :].astype(np.float32)


def _pad_to(a, rows, cols, value=0.0):
    return jnp.pad(a, ((0, rows - a.shape[0]), (0, cols - a.shape[1])), constant_values=value)


def kernel(x, c, ctx, c_ctx, w_ada, b_ada, g_mix_pre, g_mix_post, g_ffn_pre, g_ffn_post, w_in, hy_conv_w, hy_conv_b, hy_f_w1, hy_f_b1, hy_f_w2, hy_f_b2, hy_f_w3, hy_f_b3, hy_f_freq, hy_bias, lambda_q1, lambda_k1, lambda_q2, lambda_k2, att_subln_g, w_hy_up, w_att_up, w_out, w_ffn_gate, w_ffn_up, w_ffn_down):
    b, l, d = x.shape
    n_ctx = ctx.shape[1]
    order = hy_bias.shape[1]
    off_q = (order + 1) * D_HYENA
    off_k = off_q + D_ATTN
    off_v = off_k + D_ATTN
    off_g = off_v + D_ATTN
    assert w_in.shape[0] == 1, "single-layer block"
    assert l % CONV_ROWS == 0 and l % 512 == 0

    pad_rows = -(-(b + 1) // 8) * 8
    cc = jnp.concatenate([c, c_ctx[None, :], jnp.zeros((pad_rows - b - 1, d), F32)], axis=0)
    ada = _ada(cc, w_ada[0], b_ada[0][None, :])
    sh1, sc1, g1, sh2, sc2, g2 = (ada[:b, i * d:(i + 1) * d][:, None, :] for i in range(6))
    csh1, csc1 = ada[b:b + 1, 0:d], ada[b:b + 1, d:2 * d]

    row = lambda v: v[0][None, :]
    w = w_in[0].astype(BF16)
    cos, sin = _rope_tables(l)
    q_scale = HEAD_DIM ** -0.5 * math.log2(math.e)
    wqt = (w_in[0][:, off_q:off_k] * q_scale).astype(BF16).T
    wk = w[:, off_k:off_v]
    wvt = w[:, off_v:off_g].T

    kc, vct = _ctx_kv(ctx, row(g_mix_pre), csh1, csc1, wk, wvt)

    u, qt, k, vt, gate = _proj(x, row(g_mix_pre), sh1, sc1, jnp.asarray(cos), jnp.asarray(sin),
                               jnp.asarray(cos.T), jnp.asarray(sin.T),
                               w[:, :off_q], wqt, wk, wvt, w[:, off_g:])

    consts = tuple(jnp.asarray(a).astype(BF16) for a in _fft_constants(l))
    zfeat, deltas = _filter_features(l, hy_f_w1.shape[1], LANES)
    hf = _filt(jnp.asarray(zfeat),
               _pad_to(hy_f_w1[0], LANES, LANES), _pad_to(row(hy_f_b1), 1, LANES),
               _pad_to(hy_f_w2[0], LANES, LANES), _pad_to(row(hy_f_b2), 1, LANES),
               _pad_to(row(hy_f_freq), 1, LANES, 1.0),
               _pad_to(hy_f_w3[0], LANES, hy_f_w3.shape[2]), row(hy_f_b3),
               jnp.asarray(deltas), consts[0], consts[4], order, l)

    zh = _hyena(u, hy_conv_w[0], row(hy_conv_b), hy_bias[0], hf, consts[:4], order)

    lamv = jnp.concatenate([lambda_q1, lambda_k1, lambda_q2, lambda_k2], axis=0)
    oa = _attn(qt, k, vt, kc, vct, lamv, row(att_subln_g))

    return _post(x, zh, oa, gate, g1, sh2, sc2, g2, row(g_mix_post), row(g_ffn_pre), row(g_ffn_post),
                 w_hy_up[0].astype(BF16), w_att_up[0].astype(BF16), w_out[0].astype(BF16),
                 w_ffn_gate[0].astype(BF16), w_ffn_up[0].astype(BF16), w_ffn_down[0].astype(BF16))
```

```python
import functools
import math

import numpy as np
import jax
import jax.numpy as jnp
from jax import lax
from jax.experimental import pallas as pl
from jax.experimental.pallas import tpu as pltpu

F32 = jnp.float32
BF16 = jnp.bfloat16
HIGHEST = lax.Precision.HIGHEST

EPS = 1e-6
GRID_W = 64
HEAD_DIM = 64
N_HEADS = 4
D_HYENA = 512
D_ATTN = N_HEADS * 2 * HEAD_DIM
ROPE_BASE = 10000.0
FILTER_TARGET = 1e-2
FAST_DECAY_PCT = 0.3
SLOW_DECAY_PCT = 1.5
LAM_INIT = 0.8 - 0.6 * math.exp(0.0)

LANES = 128
MXU_DIM = 256
VMEM_LIMIT_BYTES = 56 * 1024 * 1024

FFT_N2 = 256
FFT_GROUP = 16
HY_CB = 256
FFT_GROUP_UNROLL = 8


def _dot(a, b):
    return jnp.dot(a, b, preferred_element_type=F32)


def _dot_hi(a, b):
    return jnp.dot(a, b, preferred_element_type=F32, precision=HIGHEST)


def _dot_nt(a, b):
    return lax.dot_general(a, b, (((1,), (1,)), ((), ())), preferred_element_type=F32)


def _rms(x, g):
    return x * lax.rsqrt(jnp.mean(x * x, axis=-1, keepdims=True) + EPS) * g


def _const_spec(shape):
    zeros = (0,) * len(shape)
    return pl.BlockSpec(shape, lambda *_: zeros, pipeline_mode=pl.Buffered(1))


def _params(*semantics):
    return pltpu.CompilerParams(dimension_semantics=semantics, vmem_limit_bytes=VMEM_LIMIT_BYTES)


def _ada_body(c_ref, w_ref, b_ref, o_ref):
    c = c_ref[...]
    o_ref[...] = _dot_hi(c * jax.nn.sigmoid(c), w_ref[...]) + b_ref[...]


def _ada(cc, w_ada, b_ada):
    rows, d = cc.shape
    n = w_ada.shape[1]
    tn = 2048
    return pl.pallas_call(
        _ada_body,
        grid=(n // tn,),
        in_specs=[
            pl.BlockSpec((rows, d), lambda j: (0, 0)),
            pl.BlockSpec((d, tn), lambda j: (0, j)),
            pl.BlockSpec((1, tn), lambda j: (0, j)),
        ],
        out_specs=pl.BlockSpec((rows, tn), lambda j: (0, j)),
        out_shape=jax.ShapeDtypeStruct((rows, n), F32),
        compiler_params=_params("arbitrary"),
        name="ada",
    )(cc, w_ada, b_ada)


def _ctx_kv_body(x_ref, g_ref, sh_ref, sc_ref, wk_ref, wvt_ref, k_ref, vt_ref):
    h = _rms(x_ref[0], g_ref[...]) * (1.0 + sc_ref[...]) + sh_ref[...]
    hb = h.astype(BF16)
    k_ref[0] = _dot(hb, wk_ref[...]).astype(BF16)
    vt_ref[0] = _dot_nt(wvt_ref[...], hb).astype(BF16)


def _ctx_kv(ctx, g, sh, sc, wk, wvt):
    b, n_ctx, d = ctx.shape
    n = wk.shape[1]
    return pl.pallas_call(
        _ctx_kv_body,
        grid=(b,),
        in_specs=[
            pl.BlockSpec((1, n_ctx, d), lambda i: (i, 0, 0)),
            _const_spec((1, d)), _const_spec((1, d)), _const_spec((1, d)),
            _const_spec((d, n)), _const_spec((n, d)),
        ],
        out_specs=[pl.BlockSpec((1, n_ctx, n), lambda i: (i, 0, 0)),
                   pl.BlockSpec((1, n, n_ctx), lambda i: (i, 0, 0))],
        out_shape=[jax.ShapeDtypeStruct((b, n_ctx, n), BF16), jax.ShapeDtypeStruct((b, n, n_ctx), BF16)],
        compiler_params=_params("arbitrary"),
        name="ctx_kv",
    )(ctx, g, sh, sc, wk, wvt)


ROT = HEAD_DIM // 4


def _proj_body(x_ref, g_ref, sh_ref, sc_ref, cos_ref, sin_ref, cost_ref, sint_ref,
               wu_ref, wqt_ref, wk_ref, wvt_ref, wg_ref,
               u_ref, qt_ref, k_ref, vt_ref, gate_ref):
    half = x_ref.shape[1] // 2
    rows_a, rows_b = slice(0, half), slice(half, 2 * half)

    def normed(rows):
        return (_rms(x_ref[0, rows, :], g_ref[...]) * (1.0 + sc_ref[0]) + sh_ref[0]).astype(BF16)

    def hyena_in(rows, hb):
        u_ref[0, rows, :] = _dot(hb, wu_ref[...]).astype(BF16)

    def gates_and_keys(rows, hb):
        gate_ref[0, rows, :] = jax.nn.sigmoid(_dot(hb, wg_ref[...])).astype(BF16)
        kv = _dot(hb, wk_ref[...])
        lane = lax.broadcasted_iota(jnp.int32, (kv.shape[0], LANES), 1)
        first = (lane % (2 * ROT)) < ROT
        slabs = []
        for c0 in range(0, kv.shape[1], LANES):
            slab = kv[:, c0:c0 + LANES]
            slabs.append(jnp.where(first, pltpu.roll(slab, LANES - ROT, axis=1), pltpu.roll(slab, ROT, axis=1)))
        k_ref[0, rows, :] = (kv * cos_ref[rows, :]
                             + jnp.concatenate(slabs, axis=1) * sin_ref[rows, :]).astype(BF16)

    hb_a = normed(rows_a)
    hyena_in(rows_a, hb_a)
    hb_b = normed(rows_b)
    gates_and_keys(rows_a, hb_a)
    hyena_in(rows_b, hb_b)
    gates_and_keys(rows_b, hb_b)
    hb = jnp.concatenate([hb_a, hb_b], axis=0)
    qv = _dot_nt(wqt_ref[...], hb)
    partner = jnp.concatenate(
        [qv[r + ROT:r + 2 * ROT] if part == 0 else qv[r:r + ROT]
         for r in range(0, qv.shape[0], 2 * ROT) for part in (0, 1)], axis=0)
    qt_ref[0] = (qv * cost_ref[...] + partner * sint_ref[...]).astype(BF16)
    vt_ref[0] = _dot_nt(wvt_ref[...], hb).astype(BF16)


def _proj(x, g, sh, sc, cos, sin, cost, sint, wu, wqt, wk, wvt, wg):
    b, l, d = x.shape
    tm = 512
    tile = lambda i, bi: (bi, i, 0)
    tile_t = lambda i, bi: (bi, 0, i)
    per_b = lambda i, bi: (bi, 0, 0)
    bf = lambda shape: jax.ShapeDtypeStruct(shape, BF16)
    n_u, n_g = wu.shape[1], wg.shape[1]
    return pl.pallas_call(
        _proj_body,
        grid=(l // tm, b),
        in_specs=[
            pl.BlockSpec((1, tm, d), tile),
            _const_spec((1, d)),
            pl.BlockSpec((1, 1, d), per_b), pl.BlockSpec((1, 1, d), per_b),
            pl.BlockSpec((tm, D_ATTN), lambda i, bi: (i, 0)), pl.BlockSpec((tm, D_ATTN), lambda i, bi: (i, 0)),
            pl.BlockSpec((D_ATTN, tm), lambda i, bi: (0, i)), pl.BlockSpec((D_ATTN, tm), lambda i, bi: (0, i)),
        ] + [_const_spec(w.shape) for w in (wu, wqt, wk, wvt, wg)],
        out_specs=[pl.BlockSpec((1, tm, n_u), tile), pl.BlockSpec((1, D_ATTN, tm), tile_t),
                   pl.BlockSpec((1, tm, D_ATTN), tile), pl.BlockSpec((1, D_ATTN, tm), tile_t),
                   pl.BlockSpec((1, tm, n_g), tile)],
        out_shape=[bf((b, l, n_u)), bf((b, D_ATTN, l)), bf((b, l, D_ATTN)), bf((b, D_ATTN, l)), bf((b, l, n_g))],
        compiler_params=_params("arbitrary", "arbitrary"),
        name="proj",
    )(x, g, sh, sc, cos, sin, cost, sint, wu, wqt, wk, wvt, wg)


def _fft_dims(l):
    n = 2 * l
    n1 = n // FFT_N2
    h1 = n1 // 2
    k1 = h1 + 1
    ng = FFT_N2 // FFT_GROUP
    return n, n1, h1, k1, ng


@functools.lru_cache(maxsize=None)
def _fft_constants(l):
    n, n1, h1, k1n, ng = _fft_dims(l)
    g = FFT_GROUP
    ma = np.zeros((ng, k1n, 2, g, h1, g), np.float64)
    for gi in range(ng):
        for j in range(g):
            n2 = gi * g + j
            nn = FFT_N2 * np.arange(h1)[None, :] + n2
            theta = 2.0 * np.pi * ((np.arange(k1n)[:, None] * nn) % n) / n
            ma[gi, :, 0, j, :, j] = np.cos(theta)
            ma[gi, :, 1, j, :, j] = -np.sin(theta)
    weight = np.full((k1n,), 2.0)
    weight[0] = weight[-1] = 1.0
    ma_fwd = ma.reshape(ng, k1n * 2 * g, h1 * g)
    ma_inv = (ma * (weight / n)[None, :, None, None, None, None]).reshape(ng, k1n * 2 * g, h1 * g)
    ma_inv = np.transpose(ma_inv, (0, 2, 1))
    kpad = -(-ma_inv.shape[2] // LANES) * LANES
    ma_inv = np.pad(ma_inv, ((0, 0), (0, 0), (0, kpad - ma_inv.shape[2])))
    ang = 2.0 * np.pi * ((np.arange(FFT_N2)[:, None] * np.arange(FFT_N2)[None, :]) % FFT_N2) / FFT_N2
    fr, fi = np.cos(ang), -np.sin(ang)
    mb_fwd = np.block([[fr, -fi], [fi, fr]])
    mb_inv = np.block([[fr, fi], [-fi, fr]])
    mb_filt = np.block([[fr, -fi, fr, -fi], [fi, fr, -fi, -fr]])
    return tuple(np.asarray(a, np.float32) for a in (ma_fwd, ma_inv, mb_fwd, mb_inv, mb_filt))


def _fft_stage_a(src_ref, ma_ref, spec_ref, l):
    _, _, h1, k1n, ng = _fft_dims(l)
    g = FFT_GROUP

    def group(gi, carry):
        off = gi * g
        xg = jnp.concatenate(
            [src_ref[pl.ds(pl.multiple_of(FFT_N2 * i + off, g), g), :] for i in range(h1)], axis=0)
        bg = _dot(ma_ref[gi], xg.astype(BF16)).astype(BF16)
        for c in range(2 * k1n):
            spec_ref[pl.ds(pl.multiple_of(c * FFT_N2 + off, g), g), :] = bg[c * g:(c + 1) * g]
        return carry

    lax.fori_loop(0, ng, group, 0, unroll=FFT_GROUP_UNROLL)


def _fft_stage_a_inv(spec_ref, ma_inv_ref, dst_ref, l):
    _, _, h1, k1n, ng = _fft_dims(l)
    g = FFT_GROUP
    kpad = ma_inv_ref.shape[2] - 2 * k1n * g

    def group(gi, carry):
        off = gi * g
        chunks = [spec_ref[pl.ds(pl.multiple_of(c * FFT_N2 + off, g), g), :] for c in range(2 * k1n)]
        if kpad:
            chunks.append(jnp.zeros((kpad, spec_ref.shape[1]), BF16))
        yg = _dot(ma_inv_ref[gi], jnp.concatenate(chunks, axis=0))
        for i in range(h1):
            dst_ref[pl.ds(pl.multiple_of(FFT_N2 * i + off, g), g), :] = yg[i * g:(i + 1) * g]
        return carry

    lax.fori_loop(0, ng, group, 0, unroll=FFT_GROUP_UNROLL)


def _filt_body(z_ref, w1_ref, b1_ref, w2_ref, b2_ref, fq_ref, w3f_ref, b3f_ref, w3b_ref, b3b_ref,
               dl_ref, ma_ref, mb_ref, hf_ref, hdn_ref, sig_ref, specf_ref, specb_ref, *, l):
    _, _, _, k1n, _ = _fft_dims(l)

    @pl.when((pl.program_id(0) == 0) & (pl.program_id(1) == 0))
    def _():
        fq = fq_ref[...]
        h1 = jnp.sin(fq * (_dot_hi(z_ref[...], w1_ref[...]) + b1_ref[...]))
        hdn_ref[...] = jnp.sin(fq * (_dot_hi(h1, w2_ref[...]) + b2_ref[...])).astype(BF16)

    hdn = hdn_ref[...]
    cb = specf_ref.shape[1]
    blk = 2 * FFT_N2
    rows = lax.broadcasted_iota(jnp.int32, (l, cb), 0)
    window = jnp.exp(rows.astype(F32) * (-1.0 / (l - 1)) * dl_ref[...])

    sig_ref[...] = (_dot(hdn, w3f_ref[...].astype(BF16)) + b3f_ref[...]) * window
    _fft_stage_a(sig_ref, ma_ref, specf_ref, l)
    sig_ref[...] = jnp.where(rows > 0, (_dot(hdn, w3b_ref[...].astype(BF16)) + b3b_ref[...]) * window, 0.0)
    _fft_stage_a(sig_ref, ma_ref, specb_ref, l)

    def blocks(starts):
        x = jnp.concatenate(
            [jnp.concatenate([specf_ref[pl.ds(st, blk), :], specb_ref[pl.ds(st, blk), :]], axis=0)
             for st in starts], axis=1)
        s = _dot(mb_ref[...], x).astype(BF16)
        for n, st in enumerate(starts):
            hf_ref[0, pl.ds(st, blk), :] = s[:, n * cb:(n + 1) * cb]

    def per_pair(i, carry):
        st = pl.multiple_of(i * 2 * blk, blk)
        blocks([st, st + blk])
        return carry

    lax.fori_loop(0, k1n // 2, per_pair, 0)
    if k1n % 2:
        blocks([(k1n - 1) * blk])


def _filt(zfeat, w1, b1, w2, b2, fq, w3, b3, deltas, ma_fwd, mb_filt, order, l):
    _, _, _, k1n, _ = _fft_dims(l)
    cb = HY_CB
    nblk = D_HYENA // cb
    srows = k1n * 2 * FFT_N2
    fcol = lambda o, c: (0, (2 * o) * nblk + c)
    bcol = lambda o, c: (0, (2 * o + 1) * nblk + c)
    return pl.pallas_call(
        functools.partial(_filt_body, l=l),
        grid=(order, nblk),
        in_specs=[
            _const_spec(zfeat.shape), _const_spec(w1.shape), _const_spec(b1.shape),
            _const_spec(w2.shape), _const_spec(b2.shape), _const_spec(fq.shape),
            pl.BlockSpec((w3.shape[0], cb), fcol), pl.BlockSpec((1, cb), fcol),
            pl.BlockSpec((w3.shape[0], cb), bcol), pl.BlockSpec((1, cb), bcol),
            pl.BlockSpec((1, cb), lambda o, c: (0, c)),
            _const_spec(ma_fwd.shape), _const_spec(mb_filt.shape),
        ],
        out_specs=pl.BlockSpec((1, srows, cb), lambda o, c: (o, 0, c)),
        out_shape=jax.ShapeDtypeStruct((order, srows, D_HYENA), BF16),
        scratch_shapes=[
            pltpu.VMEM((l, w2.shape[1]), BF16),
            pltpu.VMEM((l, cb), F32),
            pltpu.VMEM((srows, cb), BF16),
            pltpu.VMEM((srows, cb), BF16),
        ],
        compiler_params=_params("arbitrary", "arbitrary"),
        name="filt",
    )(zfeat, w1, b1, w2, b2, fq, w3, b3, w3, b3, deltas, ma_fwd, mb_filt)


CONV_ROWS = 512
HALO = 16


def _short_conv_chunk(u_ref, w_ref, b_ref, r0, l):
    c = u_ref.shape[2]
    zeros = jnp.zeros((HALO, c), F32)
    before = zeros if r0 == 0 else u_ref[0, r0 - HALO:r0, :].astype(F32)
    after = zeros if r0 + CONV_ROWS == l else u_ref[0, r0 + CONV_ROWS:r0 + CONV_ROWS + HALO, :].astype(F32)
    ext = jnp.concatenate([before, u_ref[0, r0:r0 + CONV_ROWS, :].astype(F32), after], axis=0)
    w = w_ref[...]
    n = ext.shape[0]
    prev = pltpu.roll(ext, 1, axis=0)[HALO:HALO + CONV_ROWS]
    nxt = pltpu.roll(ext, n - 1, axis=0)[HALO:HALO + CONV_ROWS]
    return prev * w[0:1] + ext[HALO:HALO + CONV_ROWS] * w[1:2] + nxt * w[2:3] + b_ref[...]


def _hyena_body(u0_ref, u1_ref, u2_ref, w0_ref, w1_ref, w2_ref, b0_ref, b1_ref, b2_ref, db_ref, hf_ref,
                ma_ref, mai_ref, mb_ref, mbi_ref, o_ref, z_ref, y_ref, spec_ref, *, l, order):
    _, _, _, k1n, _ = _fft_dims(l)
    blk = 2 * FFT_N2
    gates = ((u1_ref, w1_ref, b1_ref), (u2_ref, w2_ref, b2_ref))

    for r0 in range(0, l, CONV_ROWS):
        z_ref[r0:r0 + CONV_ROWS, :] = _short_conv_chunk(u0_ref, w0_ref, b0_ref, r0, l)

    for o in range(order):
        _fft_stage_a(z_ref, ma_ref, spec_ref, l)

        cb = spec_ref.shape[1]
        groups = [list(range(i, min(i + 2, k1n))) for i in range(0, k1n, 2)]

        def stage_b(ks):
            return _dot(mb_ref[...], jnp.concatenate([spec_ref[k * blk:(k + 1) * blk, :] for k in ks], axis=1))

        def filter_and_invert(ks, s, o=o):
            h = jnp.concatenate([hf_ref[o, k * blk:(k + 1) * blk, :] for k in ks], axis=1).astype(F32)
            sr, si, hr, hi = s[:FFT_N2], s[FFT_N2:], h[:FFT_N2], h[FFT_N2:]
            y = jnp.concatenate([sr * hr - si * hi, sr * hi + si * hr], axis=0).astype(BF16)
            c = _dot(mbi_ref[...], y).astype(BF16)
            for n, k in enumerate(ks):
                spec_ref[k * blk:(k + 1) * blk, :] = c[:, n * cb:(n + 1) * cb]

        s_next = stage_b(groups[0])
        for gi, ks in enumerate(groups):
            s_cur = s_next
            if gi + 1 < len(groups):
                s_next = stage_b(groups[gi + 1])
            filter_and_invert(ks, s_cur)
        _fft_stage_a_inv(spec_ref, mai_ref, y_ref, l)

        u_ref, w_ref, b_ref = gates[o]
        dbias = db_ref[o:o + 1, :]
        for r0 in range(0, l, CONV_ROWS):
            rows = slice(r0, r0 + CONV_ROWS)
            znew = _short_conv_chunk(u_ref, w_ref, b_ref, r0, l) * (y_ref[rows, :] + z_ref[rows, :] * dbias)
            if o + 1 < order:
                z_ref[rows, :] = znew
            else:
                o_ref[0, rows, :] = znew.astype(BF16)


def _hyena(u, conv_w, conv_b, dbias, hf, consts, order):
    b, l, _ = u.shape
    _, _, _, k1n, _ = _fft_dims(l)
    cb = HY_CB
    nblk = D_HYENA // cb
    srows = k1n * 2 * FFT_N2
    ma_fwd, ma_inv, mb_fwd, mb_inv = consts
    upart = lambda p: pl.BlockSpec((1, l, cb), lambda c, bi: (bi, 0, p * nblk + c))
    wpart = lambda p: pl.BlockSpec((3, cb), lambda c, bi: (0, p * nblk + c))
    bpart = lambda p: pl.BlockSpec((1, cb), lambda c, bi: (0, p * nblk + c))
    return pl.pallas_call(
        functools.partial(_hyena_body, l=l, order=order),
        grid=(nblk, b),
        in_specs=[
            upart(0), upart(1), upart(2), wpart(0), wpart(1), wpart(2), bpart(0), bpart(1), bpart(2),
            pl.BlockSpec((order, cb), lambda c, bi: (0, c)),
            pl.BlockSpec((order, srows, cb), lambda c, bi: (0, 0, c), pipeline_mode=pl.Buffered(1)),
            _const_spec(ma_fwd.shape), _const_spec(ma_inv.shape),
            _const_spec(mb_fwd.shape), _const_spec(mb_inv.shape),
        ],
        out_specs=pl.BlockSpec((1, l, cb), lambda c, bi: (bi, 0, c)),
        out_shape=jax.ShapeDtypeStruct((b, l, D_HYENA), BF16),
        scratch_shapes=[
            pltpu.VMEM((l, cb), F32),
            pltpu.VMEM((l, cb), F32),
            pltpu.VMEM((srows, cb), BF16),
        ],
        compiler_params=_params("arbitrary", "arbitrary"),
        name="hyena",
    )(u, u, u, conv_w, conv_w, conv_w, conv_b, conv_b, conv_b, dbias, hf, ma_fwd, ma_inv, mb_fwd, mb_inv)


ATT_TQ = MXU_DIM
ATT_KEY_CHUNK = 2 * MXU_DIM
ATT_BLOCKS = 8


def _attn_body(qc_ref, qn_ref, k_ref, kn_ref, kc_ref, kcn_ref, vt_ref, vct_ref, lam_ref, sg_ref, o_ref,
               sa_ref, sb_ref, ma_ref, mb_ref):
    tq = ATT_TQ
    lat = k_ref.shape[1]
    lv = lam_ref[...]
    lam = (jnp.exp(jnp.sum(lv[0:1] * lv[1:2], axis=-1, keepdims=True))
           - jnp.exp(jnp.sum(lv[2:3] * lv[3:4], axis=-1, keepdims=True)) + LAM_INIT)

    n_ctx = kc_ref.shape[1]
    chunks = [(False, r, r, ATT_KEY_CHUNK) for r in range(0, lat, ATT_KEY_CHUNK)] + [(True, 0, lat, n_ctx)]

    def stage(qt, keys_ref, ctx_keys_ref, s_new, m_new, s_old, m_old, hook=None):
        m_run = acc = denom = None
        feat = lax.broadcasted_iota(jnp.int32, qt.shape, 0)
        qz = jnp.concatenate([jnp.where(feat < HEAD_DIM, qt, jnp.zeros_like(qt)),
                              jnp.where(feat >= HEAD_DIM, qt, jnp.zeros_like(qt))], axis=1)
        if s_old is not None:
            m_prev = m_old[...]
        for n, (is_ctx, src, dst, rows) in enumerate(chunks):
            keys = ctx_keys_ref[0, src:src + rows, :] if is_ctx else keys_ref[0, src:src + rows, :]
            s = _dot(keys, qz)
            s_new[dst:dst + rows, :] = s
            cmax = jnp.max(s, axis=0, keepdims=True)
            m_run = cmax if m_run is None else jnp.maximum(m_run, cmax)
            if s_old is not None:
                p = jnp.exp2(s_old[dst:dst + rows, :] - m_prev)
                psum = jnp.sum(p, axis=0, keepdims=True)
                vals = vct_ref[0, :, src:src + rows] if is_ctx else vt_ref[0, :, src:src + rows]
                part = _dot(vals, p.astype(BF16))
                denom = psum if denom is None else denom + psum
                acc = part if acc is None else acc + part
            if n == 0 and hook is not None:
                hook()
        m_new[...] = m_run
        return acc, denom

    def finish(acc, denom):
        r = 1.0 / denom
        ot = acc[:, :tq] * r[:, :tq] - acc[:, tq:] * (lam * r[:, tq:])
        return (_rms(ot.T, sg_ref[...]) * (1.0 - LAM_INIT)).astype(BF16)

    @pl.when(pl.program_id(0) == 0)
    def _():
        stage(qc_ref[0, :, :tq], k_ref, kc_ref, sa_ref, ma_ref, None, None)

    bufs = ((sa_ref, ma_ref), (sb_ref, mb_ref))
    pending = None
    for n in range(ATT_BLOCKS):
        if n + 1 < ATT_BLOCKS:
            new = (qc_ref[0, :, (n + 1) * tq:(n + 2) * tq], k_ref, kc_ref)
        else:
            new = (qn_ref[0, :, :tq], kn_ref, kcn_ref)
        raw = stage(*new, *bufs[(n + 1) % 2], *bufs[n % 2], hook=pending)

        def pending(raw=raw, n=n):
            o_ref[0, n * tq:(n + 1) * tq, :] = finish(*raw)

    pending()


def _attn(qt, k, vt, kc, vct, lamv, subln_g):
    b, l, _ = k.shape
    c = kc.shape[1]
    dv = 2 * HEAD_DIM
    tq = ATT_TQ
    tstep = ATT_BLOCKS * tq
    assert l % tstep == 0 and ATT_BLOCKS % 2 == 0
    pairs = l // tstep
    steps = b * N_HEADS * pairs

    def split(j):
        return j // (N_HEADS * pairs), (j // pairs) % N_HEADS, j % pairs

    def cur(fn):
        return lambda j: fn(*split(j))

    def nxt(fn):
        return lambda j: fn(*split(jnp.minimum(j + 1, steps - 1)))

    q_map = lambda bi, h, i: (bi, h, i)
    k_map = lambda bi, h, i: (bi, 0, h)
    v_map = lambda bi, h, i: (bi, h, 0)
    return pl.pallas_call(
        _attn_body,
        grid=(steps,),
        in_specs=[
            pl.BlockSpec((1, dv, tstep), cur(q_map)), pl.BlockSpec((1, dv, tstep), nxt(q_map)),
            pl.BlockSpec((1, l, dv), cur(k_map)), pl.BlockSpec((1, l, dv), nxt(k_map)),
            pl.BlockSpec((1, c, dv), cur(k_map)), pl.BlockSpec((1, c, dv), nxt(k_map)),
            pl.BlockSpec((1, dv, l), cur(v_map)), pl.BlockSpec((1, dv, c), cur(v_map)),
            _const_spec(lamv.shape), _const_spec(subln_g.shape),
        ],
        out_specs=pl.BlockSpec((1, tstep, dv), cur(lambda bi, h, i: (bi, i, h))),
        out_shape=jax.ShapeDtypeStruct((b, l, N_HEADS * dv), BF16),
        scratch_shapes=[pltpu.VMEM((l + c, 2 * tq), F32), pltpu.VMEM((l + c, 2 * tq), F32),
                        pltpu.VMEM((1, 2 * tq), F32), pltpu.VMEM((1, 2 * tq), F32)],
        compiler_params=_params("arbitrary"),
        name="attn",
    )(qt, qt, k, k, kc, kc, vt, vct, lamv, subln_g)


def _post_body(x_ref, zh_ref, oa_ref, gate_ref, g1_ref, sh2_ref, sc2_ref, g2_ref,
               gpost_ref, gfpre_ref, gfpost_ref,
               whu_ref, wau_ref, wo_ref, wg_ref, wu_ref, wd_ref, o_ref, *, ff_chunks):
    d = x_ref.shape[2]
    half = x_ref.shape[1] // 2
    rows_a, rows_b = slice(0, half), slice(half, 2 * half)

    def lifts(rows):
        return _dot(zh_ref[0, rows, :], whu_ref[...]), _dot(oa_ref[0, rows, :], wau_ref[...])

    def out_proj(rows, y_hy, y_att):
        mixed = gate_ref[0, rows, :d].astype(F32) * y_hy + gate_ref[0, rows, d:].astype(F32) * y_att
        return _dot(mixed.astype(BF16), wo_ref[...])

    def residual_and_norm(rows, mixed):
        x1 = x_ref[0, rows, :] + g1_ref[0] * _rms(mixed, gpost_ref[...])
        return x1, (_rms(x1, gfpre_ref[...]) * (1.0 + sc2_ref[0]) + sh2_ref[0]).astype(BF16)

    def ffn_chunk(hf, f, chunk):
        c0, cw = chunk
        a = _dot(hf, wg_ref[:, c0:c0 + cw])
        t = (a * jax.nn.sigmoid(a) * _dot(hf, wu_ref[:, c0:c0 + cw])).astype(BF16)
        part = _dot(t, wd_ref[c0:c0 + cw, :])
        return part if f is None else f + part

    def finish(rows, x1, f):
        o_ref[0, rows, :] = x1 + g2_ref[0] * _rms(f, gfpost_ref[...])

    mixed_a = out_proj(rows_a, *lifts(rows_a))
    lifted_b = lifts(rows_b)
    x1_a, hf_a = residual_and_norm(rows_a, mixed_a)
    mixed_b = out_proj(rows_b, *lifted_b)
    f_a = ffn_chunk(hf_a, None, ff_chunks[0])
    x1_b, hf_b = residual_and_norm(rows_b, mixed_b)
    for chunk in ff_chunks[1:]:
        f_a = ffn_chunk(hf_a, f_a, chunk)
    f_b = ffn_chunk(hf_b, None, ff_chunks[0])
    finish(rows_a, x1_a, f_a)
    for chunk in ff_chunks[1:]:
        f_b = ffn_chunk(hf_b, f_b, chunk)
    finish(rows_b, x1_b, f_b)


def _post(x, zh, oa, gate, g1, sh2, sc2, g2, gpost, gfpre, gfpost, whu, wau, wo, wg, wu, wd):
    b, l, d = x.shape
    dff = wg.shape[1]
    tm = 512
    ff_chunk = 4 * MXU_DIM
    ff_chunks = tuple((c0, min(ff_chunk, dff - c0)) for c0 in range(0, dff, ff_chunk))
    tile = lambda bi, i: (bi, i, 0)
    per_b = lambda bi, i: (bi, 0, 0)
    return pl.pallas_call(
        functools.partial(_post_body, ff_chunks=ff_chunks),
        grid=(b, l // tm),
        in_specs=[
            pl.BlockSpec((1, tm, d), tile),
            pl.BlockSpec((1, tm, zh.shape[2]), tile),
            pl.BlockSpec((1, tm, oa.shape[2]), tile),
            pl.BlockSpec((1, tm, gate.shape[2]), tile),
            pl.BlockSpec((1, 1, d), per_b), pl.BlockSpec((1, 1, d), per_b),
            pl.BlockSpec((1, 1, d), per_b), pl.BlockSpec((1, 1, d), per_b),
            _const_spec((1, d)), _const_spec((1, d)), _const_spec((1, d)),
        ] + [_const_spec(w.shape) for w in (whu, wau, wo, wg, wu, wd)],
        out_specs=pl.BlockSpec((1, tm, d), tile),
        out_shape=jax.ShapeDtypeStruct((b, l, d), F32),
        compiler_params=_params("arbitrary", "arbitrary"),
        name="post",
    )(x, zh, oa, gate, g1, sh2, sc2, g2, gpost, gfpre, gfpost, whu, wau, wo, wg, wu, wd)


@functools.lru_cache(maxsize=None)
def _rope_tables(l):
    quarter = HEAD_DIM // 4
    inv = ROPE_BASE ** (-np.arange(quarter, dtype=np.float64) / quarter)
    t = np.arange(l)
    ang_r = (t // GRID_W)[:, None] * inv[None, :]
    ang_c = (t % GRID_W)[:, None] * inv[None, :]
    ang = np.concatenate([ang_r, ang_r, ang_c, ang_c], axis=1)
    reps = D_ATTN // HEAD_DIM
    cos = np.tile(np.cos(ang), (1, reps)).astype(np.float32)
    second = (np.arange(D_ATTN) % (2 * quarter)) >= quarter
    sin = (np.tile(np.sin(ang), (1, reps)) * np.where(second, 1.0, -1.0)[None, :]).astype(np.float32)
    return cos, sin


@functools.lru_cache(maxsize=None)
def _filter_features(l, emb, kpad):
    bands = (emb - 1) // 2
    t = np.linspace(0.0, 1.0, l, dtype=np.float32).astype(np.float64)[:, None]
    w = (2.0 * math.pi / l) * np.arange(l, dtype=np.float64)[:, None]
    fr = np.linspace(1e-4, bands - 1, bands, dtype=np.float32).astype(np.float64)[None, :]
    z = np.concatenate([t, np.cos(fr * w), -np.sin(fr * w)], axis=-1)
    z = np.pad(z, ((0, 0), (0, kpad - emb))).astype(np.float32)
    deltas = np.abs(np.linspace(math.log(FILTER_TARGET) / SLOW_DECAY_PCT,
                                math.log(FILTER_TARGET) / FAST_DECAY_PCT, D_HYENA, dtype=np.float32))
    return z, deltas[None, :].astype(np.float32)


def _pad_to(a, rows, cols, value=0.0):
    return jnp.pad(a, ((0, rows - a.shape[0]), (0, cols - a.shape[1])), constant_values=value)


def kernel(x, c, ctx, c_ctx, w_ada, b_ada, g_mix_pre, g_mix_post, g_ffn_pre, g_ffn_post, w_in, hy_conv_w, hy_conv_b, hy_f_w1, hy_f_b1, hy_f_w2, hy_f_b2, hy_f_w3, hy_f_b3, hy_f_freq, hy_bias, lambda_q1, lambda_k1, lambda_q2, lambda_k2, att_subln_g, w_hy_up, w_att_up, w_out, w_ffn_gate, w_ffn_up, w_ffn_down):
    b, l, d = x.shape
    n_ctx = ctx.shape[1]
    order = hy_bias.shape[1]
    off_q = (order + 1) * D_HYENA
    off_k = off_q + D_ATTN
    off_v = off_k + D_ATTN
    off_g = off_v + D_ATTN
    assert w_in.shape[0] == 1, "single-layer block"
    assert l % CONV_ROWS == 0 and l % 512 == 0

    pad_rows = -(-(b + 1) // 8) * 8
    cc = jnp.concatenate([c, c_ctx[None, :], jnp.zeros((pad_rows - b - 1, d), F32)], axis=0)
    ada = _ada(cc, w_ada[0], b_ada[0][None, :])
    sh1, sc1, g1, sh2, sc2, g2 = (ada[:b, i * d:(i + 1) * d][:, None, :] for i in range(6))
    csh1, csc1 = ada[b:b + 1, 0:d], ada[b:b + 1, d:2 * d]

    row = lambda v: v[0][None, :]
    w = w_in[0]
    cos, sin = _rope_tables(l)
    q_scale = HEAD_DIM ** -0.5 * math.log2(math.e)
    wqt = (w[:, off_q:off_k] * q_scale).T.astype(BF16)
    wk = w[:, off_k:off_v].astype(BF16)
    wvt = w[:, off_v:off_g].T.astype(BF16)

    kc, vct = _ctx_kv(ctx, row(g_mix_pre), csh1, csc1, wk, wvt)

    u, qt, k, vt, gate = _proj(x, row(g_mix_pre), sh1, sc1, jnp.asarray(cos), jnp.asarray(sin),
                               jnp.asarray(cos.T), jnp.asarray(sin.T),
                               w[:, :off_q].astype(BF16), wqt, wk, wvt, w[:, off_g:].astype(BF16))

    consts = tuple(jnp.asarray(a).astype(BF16) for a in _fft_constants(l))
    zfeat, deltas = _filter_features(l, hy_f_w1.shape[1], LANES)
    hf = _filt(jnp.asarray(zfeat),
               _pad_to(hy_f_w1[0], LANES, LANES), _pad_to(row(hy_f_b1), 1, LANES),
               _pad_to(hy_f_w2[0], LANES, LANES), _pad_to(row(hy_f_b2), 1, LANES),
               _pad_to(row(hy_f_freq), 1, LANES, 1.0),
               _pad_to(hy_f_w3[0], LANES, hy_f_w3.shape[2]), row(hy_f_b3),
               jnp.asarray(deltas), consts[0], consts[4], order, l)

    zh = _hyena(u, hy_conv_w[0], row(hy_conv_b), hy_bias[0], hf, consts[:4], order)

    lamv = jnp.concatenate([lambda_q1, lambda_k1, lambda_q2, lambda_k2], axis=0)
    oa = _attn(qt, k, vt, kc, vct, lamv, row(att_subln_g))

    return _post(x, zh, oa, gate, g1, sh2, sc2, g2, row(g_mix_post), row(g_ffn_pre), row(g_ffn_post),
                 w_hy_up[0].astype(BF16), w_att_up[0].astype(BF16), w_out[0].astype(BF16),
                 w_ffn_gate[0].astype(BF16), w_ffn_up[0].astype(BF16), w_ffn_down[0].astype(BF16))
```

```python
import functools
import math

import numpy as np
import jax
import jax.numpy as jnp
from jax import lax
from jax.experimental import pallas as pl
from jax.experimental.pallas import tpu as pltpu

F32 = jnp.float32
BF16 = jnp.bfloat16
HIGHEST = lax.Precision.HIGHEST

EPS = 1e-6
GRID_W = 64
HEAD_DIM = 64
N_HEADS = 4
D_HYENA = 512
D_ATTN = N_HEADS * 2 * HEAD_DIM
ROPE_BASE = 10000.0
FILTER_TARGET = 1e-2
FAST_DECAY_PCT = 0.3
SLOW_DECAY_PCT = 1.5
LAM_INIT = 0.8 - 0.6 * math.exp(0.0)

LANES = 128
MXU_DIM = 256
VMEM_LIMIT_BYTES = 56 * 1024 * 1024

FFT_N2 = 256
FFT_GROUP = 16
HY_CB = 256
FFT_GROUP_UNROLL = 8


def _dot(a, b):
    return jnp.dot(a, b, preferred_element_type=F32)


def _dot_hi(a, b):
    return jnp.dot(a, b, preferred_element_type=F32, precision=HIGHEST)


def _dot_nt(a, b):
    return lax.dot_general(a, b, (((1,), (1,)), ((), ())), preferred_element_type=F32)


def _rms(x, g):
    return x * lax.rsqrt(jnp.mean(x * x, axis=-1, keepdims=True) + EPS) * g


def _const_spec(shape):
    zeros = (0,) * len(shape)
    return pl.BlockSpec(shape, lambda *_: zeros, pipeline_mode=pl.Buffered(1))


def _params(*semantics):
    return pltpu.CompilerParams(dimension_semantics=semantics, vmem_limit_bytes=VMEM_LIMIT_BYTES)


def _ada_body(c_ref, w_ref, b_ref, o_ref):
    c = c_ref[...]
    o_ref[...] = _dot_hi(c * jax.nn.sigmoid(c), w_ref[...]) + b_ref[...]


def _ada(cc, w_ada, b_ada):
    rows, d = cc.shape
    n = w_ada.shape[1]
    tn = 1024
    return pl.pallas_call(
        _ada_body,
        grid=(n // tn,),
        in_specs=[
            pl.BlockSpec((rows, d), lambda j: (0, 0)),
            pl.BlockSpec((d, tn), lambda j: (0, j)),
            pl.BlockSpec((1, tn), lambda j: (0, j)),
        ],
        out_specs=pl.BlockSpec((rows, tn), lambda j: (0, j)),
        out_shape=jax.ShapeDtypeStruct((rows, n), F32),
        compiler_params=_params("arbitrary"),
        name="ada",
    )(cc, w_ada, b_ada)


def _ctx_kv_body(x_ref, g_ref, sh_ref, sc_ref, wk_ref, wvt_ref, k_ref, vt_ref):
    h = _rms(x_ref[0], g_ref[...]) * (1.0 + sc_ref[...]) + sh_ref[...]
    hb = h.astype(BF16)
    k_ref[0] = _dot(hb, wk_ref[...]).astype(BF16)
    vt_ref[0] = _dot_nt(wvt_ref[...], hb).astype(BF16)


def _ctx_kv(ctx, g, sh, sc, wk, wvt):
    b, n_ctx, d = ctx.shape
    n = wk.shape[1]
    return pl.pallas_call(
        _ctx_kv_body,
        grid=(b,),
        in_specs=[
            pl.BlockSpec((1, n_ctx, d), lambda i: (i, 0, 0)),
            _const_spec((1, d)), _const_spec((1, d)), _const_spec((1, d)),
            _const_spec((d, n)), _const_spec((n, d)),
        ],
        out_specs=[pl.BlockSpec((1, n_ctx, n), lambda i: (i, 0, 0)),
                   pl.BlockSpec((1, n, n_ctx), lambda i: (i, 0, 0))],
        out_shape=[jax.ShapeDtypeStruct((b, n_ctx, n), BF16), jax.ShapeDtypeStruct((b, n, n_ctx), BF16)],
        compiler_params=_params("arbitrary"),
        name="ctx_kv",
    )(ctx, g, sh, sc, wk, wvt)


ROT = HEAD_DIM // 4


def _proj_body(x_ref, g_ref, sh_ref, sc_ref, cos_ref, sin_ref, cost_ref, sint_ref,
               wu_ref, wqt_ref, wk_ref, wvt_ref, wg_ref,
               u_ref, qt_ref, k_ref, vt_ref, gate_ref):
    half = x_ref.shape[1] // 2
    rows_a, rows_b = slice(0, half), slice(half, 2 * half)

    def normed(rows):
        return (_rms(x_ref[0, rows, :], g_ref[...]) * (1.0 + sc_ref[0]) + sh_ref[0]).astype(BF16)

    def hyena_in(rows, hb):
        u_ref[0, rows, :] = _dot(hb, wu_ref[...]).astype(BF16)

    def gates_and_keys(rows, hb):
        gate_ref[0, rows, :] = jax.nn.sigmoid(_dot(hb, wg_ref[...])).astype(BF16)
        kv = _dot(hb, wk_ref[...])
        lane = lax.broadcasted_iota(jnp.int32, (kv.shape[0], LANES), 1)
        first = (lane % (2 * ROT)) < ROT
        slabs = []
        for c0 in range(0, kv.shape[1], LANES):
            slab = kv[:, c0:c0 + LANES]
            slabs.append(jnp.where(first, pltpu.roll(slab, LANES - ROT, axis=1), pltpu.roll(slab, ROT, axis=1)))
        k_ref[0, rows, :] = (kv * cos_ref[rows, :]
                             + jnp.concatenate(slabs, axis=1) * sin_ref[rows, :]).astype(BF16)

    hb_a = normed(rows_a)
    hyena_in(rows_a, hb_a)
    hb_b = normed(rows_b)
    gates_and_keys(rows_a, hb_a)
    hyena_in(rows_b, hb_b)
    gates_and_keys(rows_b, hb_b)
    hb = jnp.concatenate([hb_a, hb_b], axis=0)
    qv = _dot_nt(wqt_ref[...], hb)
    partner = jnp.concatenate(
        [qv[r + ROT:r + 2 * ROT] if part == 0 else qv[r:r + ROT]
         for r in range(0, qv.shape[0], 2 * ROT) for part in (0, 1)], axis=0)
    qt_ref[0] = (qv * cost_ref[...] + partner * sint_ref[...]).astype(BF16)
    vt_ref[0] = _dot_nt(wvt_ref[...], hb).astype(BF16)


def _proj(x, g, sh, sc, cos, sin, cost, sint, wu, wqt, wk, wvt, wg):
    b, l, d = x.shape
    tm = 512
    tile = lambda i, bi: (bi, i, 0)
    tile_t = lambda i, bi: (bi, 0, i)
    per_b = lambda i, bi: (bi, 0, 0)
    bf = lambda shape: jax.ShapeDtypeStruct(shape, BF16)
    n_u, n_g = wu.shape[1], wg.shape[1]
    return pl.pallas_call(
        _proj_body,
        grid=(l // tm, b),
        in_specs=[
            pl.BlockSpec((1, tm, d), tile),
            _const_spec((1, d)),
            pl.BlockSpec((1, 1, d), per_b), pl.BlockSpec((1, 1, d), per_b),
            pl.BlockSpec((tm, D_ATTN), lambda i, bi: (i, 0)), pl.BlockSpec((tm, D_ATTN), lambda i, bi: (i, 0)),
            pl.BlockSpec((D_ATTN, tm), lambda i, bi: (0, i)), pl.BlockSpec((D_ATTN, tm), lambda i, bi: (0, i)),
        ] + [_const_spec(w.shape) for w in (wu, wqt, wk, wvt, wg)],
        out_specs=[pl.BlockSpec((1, tm, n_u), tile), pl.BlockSpec((1, D_ATTN, tm), tile_t),
                   pl.BlockSpec((1, tm, D_ATTN), tile), pl.BlockSpec((1, D_ATTN, tm), tile_t),
                   pl.BlockSpec((1, tm, n_g), tile)],
        out_shape=[bf((b, l, n_u)), bf((b, D_ATTN, l)), bf((b, l, D_ATTN)), bf((b, D_ATTN, l)), bf((b, l, n_g))],
        compiler_params=_params("arbitrary", "arbitrary"),
        name="proj",
    )(x, g, sh, sc, cos, sin, cost, sint, wu, wqt, wk, wvt, wg)


def _fft_dims(l):
    n = 2 * l
    n1 = n // FFT_N2
    h1 = n1 // 2
    k1 = h1 + 1
    ng = FFT_N2 // FFT_GROUP
    return n, n1, h1, k1, ng


@functools.lru_cache(maxsize=None)
def _fft_constants(l):
    n, n1, h1, k1n, ng = _fft_dims(l)
    g = FFT_GROUP
    ma = np.zeros((ng, k1n, 2, g, h1, g), np.float64)
    for gi in range(ng):
        for j in range(g):
            n2 = gi * g + j
            nn = FFT_N2 * np.arange(h1)[None, :] + n2
            theta = 2.0 * np.pi * ((np.arange(k1n)[:, None] * nn) % n) / n
            ma[gi, :, 0, j, :, j] = np.cos(theta)
            ma[gi, :, 1, j, :, j] = -np.sin(theta)
    weight = np.full((k1n,), 2.0)
    weight[0] = weight[-1] = 1.0
    ma_fwd = ma.reshape(ng, k1n * 2 * g, h1 * g)
    ma_inv = (ma * (weight / n)[None, :, None, None, None, None]).reshape(ng, k1n * 2 * g, h1 * g)
    ma_inv = np.transpose(ma_inv, (0, 2, 1))
    kpad = -(-ma_inv.shape[2] // LANES) * LANES
    ma_inv = np.pad(ma_inv, ((0, 0), (0, 0), (0, kpad - ma_inv.shape[2])))
    ang = 2.0 * np.pi * ((np.arange(FFT_N2)[:, None] * np.arange(FFT_N2)[None, :]) % FFT_N2) / FFT_N2
    fr, fi = np.cos(ang), -np.sin(ang)
    dft3 = np.stack([fr, fi, fr + fi, fr - fi])
    mb_filt = np.block([[fr, -fi, fr, -fi], [fi, fr, -fi, -fr]])
    return tuple(np.asarray(a, np.float32) for a in (ma_fwd, ma_inv, dft3, mb_filt))


def _fft_stage_a(src_ref, ma_ref, spec_ref, l):
    _, _, h1, k1n, ng = _fft_dims(l)
    g = FFT_GROUP

    def group(gi, carry):
        off = gi * g
        xg = jnp.concatenate(
            [src_ref[pl.ds(pl.multiple_of(FFT_N2 * i + off, g), g), :] for i in range(h1)], axis=0)
        bg = _dot(ma_ref[gi], xg.astype(BF16)).astype(BF16)
        for c in range(2 * k1n):
            spec_ref[pl.ds(pl.multiple_of(c * FFT_N2 + off, g), g), :] = bg[c * g:(c + 1) * g]
        return carry

    lax.fori_loop(0, ng, group, 0, unroll=FFT_GROUP_UNROLL)


def _fft_stage_a_inv(spec_ref, ma_inv_ref, emit, l):
    _, _, h1, k1n, ng = _fft_dims(l)
    g = FFT_GROUP
    kpad = ma_inv_ref.shape[2] - 2 * k1n * g

    def group(gi, carry):
        off = gi * g
        chunks = [spec_ref[pl.ds(pl.multiple_of(c * FFT_N2 + off, g), g), :] for c in range(2 * k1n)]
        if kpad:
            chunks.append(jnp.zeros((kpad, spec_ref.shape[1]), BF16))
        yg = _dot(ma_inv_ref[gi], jnp.concatenate(chunks, axis=0))
        for i in range(h1):
            emit(pl.ds(pl.multiple_of(FFT_N2 * i + off, g), g), yg[i * g:(i + 1) * g])
        return carry

    lax.fori_loop(0, ng, group, 0, unroll=FFT_GROUP_UNROLL)


def _filt_body(z_ref, w1_ref, b1_ref, w2_ref, b2_ref, fq_ref, w3f_ref, b3f_ref, w3b_ref, b3b_ref,
               dl_ref, ma_ref, mb_ref, hf_ref, hdn_ref, sig_ref, specf_ref, specb_ref, *, l):
    _, _, _, k1n, _ = _fft_dims(l)

    @pl.when((pl.program_id(0) == 0) & (pl.program_id(1) == 0))
    def _():
        fq = fq_ref[...]
        h1 = jnp.sin(fq * (_dot_hi(z_ref[...], w1_ref[...]) + b1_ref[...]))
        hdn_ref[...] = jnp.sin(fq * (_dot_hi(h1, w2_ref[...]) + b2_ref[...])).astype(BF16)

    hdn = hdn_ref[...]
    cb = specf_ref.shape[1]
    blk = 2 * FFT_N2
    rows = lax.broadcasted_iota(jnp.int32, (l, cb), 0)
    window = jnp.exp(rows.astype(F32) * (-1.0 / (l - 1)) * dl_ref[...])

    sig_ref[...] = (_dot(hdn, w3f_ref[...].astype(BF16)) + b3f_ref[...]) * window
    _fft_stage_a(sig_ref, ma_ref, specf_ref, l)
    sig_ref[...] = jnp.where(rows > 0, (_dot(hdn, w3b_ref[...].astype(BF16)) + b3b_ref[...]) * window, 0.0)
    _fft_stage_a(sig_ref, ma_ref, specb_ref, l)

    def blocks(starts):
        x = jnp.concatenate(
            [jnp.concatenate([specf_ref[pl.ds(st, blk), :], specb_ref[pl.ds(st, blk), :]], axis=0)
             for st in starts], axis=1)
        s = _dot(mb_ref[...], x).astype(BF16)
        for n, st in enumerate(starts):
            hf_ref[0, pl.ds(st, blk), :] = s[:, n * cb:(n + 1) * cb]

    def per_pair(i, carry):
        st = pl.multiple_of(i * 2 * blk, blk)
        blocks([st, st + blk])
        return carry

    lax.fori_loop(0, k1n // 2, per_pair, 0)
    if k1n % 2:
        blocks([(k1n - 1) * blk])


def _filt(zfeat, w1, b1, w2, b2, fq, w3, b3, deltas, ma_fwd, mb_filt, order, l):
    _, _, _, k1n, _ = _fft_dims(l)
    cb = HY_CB
    nblk = D_HYENA // cb
    srows = k1n * 2 * FFT_N2
    fcol = lambda o, c: (0, (2 * o) * nblk + c)
    bcol = lambda o, c: (0, (2 * o + 1) * nblk + c)
    return pl.pallas_call(
        functools.partial(_filt_body, l=l),
        grid=(order, nblk),
        in_specs=[
            _const_spec(zfeat.shape), _const_spec(w1.shape), _const_spec(b1.shape),
            _const_spec(w2.shape), _const_spec(b2.shape), _const_spec(fq.shape),
            pl.BlockSpec((w3.shape[0], cb), fcol), pl.BlockSpec((1, cb), fcol),
            pl.BlockSpec((w3.shape[0], cb), bcol), pl.BlockSpec((1, cb), bcol),
            pl.BlockSpec((1, cb), lambda o, c: (0, c)),
            _const_spec(ma_fwd.shape), _const_spec(mb_filt.shape),
        ],
        out_specs=pl.BlockSpec((1, srows, cb), lambda o, c: (o, 0, c)),
        out_shape=jax.ShapeDtypeStruct((order, srows, D_HYENA), BF16),
        scratch_shapes=[
            pltpu.VMEM((l, w2.shape[1]), BF16),
            pltpu.VMEM((l, cb), F32),
            pltpu.VMEM((srows, cb), BF16),
            pltpu.VMEM((srows, cb), BF16),
        ],
        compiler_params=_params("arbitrary", "arbitrary"),
        name="filt",
    )(zfeat, w1, b1, w2, b2, fq, w3, b3, w3, b3, deltas, ma_fwd, mb_filt)


CONV_ROWS = 512
HALO = 16


def _short_conv_chunk(u_ref, w_ref, b_ref, r0, l):
    c = u_ref.shape[2]
    zeros = jnp.zeros((HALO, c), F32)
    before = zeros if r0 == 0 else u_ref[0, r0 - HALO:r0, :].astype(F32)
    after = zeros if r0 + CONV_ROWS == l else u_ref[0, r0 + CONV_ROWS:r0 + CONV_ROWS + HALO, :].astype(F32)
    ext = jnp.concatenate([before, u_ref[0, r0:r0 + CONV_ROWS, :].astype(F32), after], axis=0)
    w = w_ref[...]
    n = ext.shape[0]
    prev = pltpu.roll(ext, 1, axis=0)[HALO:HALO + CONV_ROWS]
    nxt = pltpu.roll(ext, n - 1, axis=0)[HALO:HALO + CONV_ROWS]
    return prev * w[0:1] + ext[HALO:HALO + CONV_ROWS] * w[1:2] + nxt * w[2:3] + b_ref[...]


def _hyena_body(u0_ref, u1_ref, u2_ref, w0_ref, w1_ref, w2_ref, b0_ref, b1_ref, b2_ref, db_ref, hf_ref,
                ma_ref, mai_ref, dft_ref, o_ref, z_ref, y_ref, spec_ref, *, l, order):
    _, _, _, k1n, _ = _fft_dims(l)
    blk = 2 * FFT_N2
    gates = ((u1_ref, w1_ref, b1_ref), (u2_ref, w2_ref, b2_ref))

    for r0 in range(0, l, CONV_ROWS):
        z_ref[r0:r0 + CONV_ROWS, :] = _short_conv_chunk(u0_ref, w0_ref, b0_ref, r0, l)

    for o in range(order):
        _fft_stage_a(z_ref, ma_ref, spec_ref, l)

        cb = spec_ref.shape[1]
        groups = [list(range(i, min(i + 2, k1n))) for i in range(0, k1n, 2)]

        def stage_b(ks):
            br = jnp.concatenate([spec_ref[k * blk:k * blk + FFT_N2, :] for k in ks], axis=1)
            bi = jnp.concatenate([spec_ref[k * blk + FFT_N2:(k + 1) * blk, :] for k in ks], axis=1)
            k1 = _dot(dft_ref[2], br)
            return k1 - _dot(dft_ref[1], br + bi), k1 + _dot(dft_ref[0], bi - br)

        def filter_and_invert(ks, s, o=o):
            sr, si = s
            hr = jnp.concatenate([hf_ref[o, k * blk:k * blk + FFT_N2, :] for k in ks], axis=1).astype(F32)
            hi = jnp.concatenate([hf_ref[o, k * blk + FFT_N2:(k + 1) * blk, :] for k in ks], axis=1).astype(F32)
            yr = sr * hr - si * hi
            yi = sr * hi + si * hr
            k1 = _dot(dft_ref[3], yr.astype(BF16))
            cr = (k1 + _dot(dft_ref[1], (yr + yi).astype(BF16))).astype(BF16)
            ci = (k1 + _dot(dft_ref[0], (yi - yr).astype(BF16))).astype(BF16)
            for n, k in enumerate(ks):
                spec_ref[k * blk:k * blk + FFT_N2, :] = cr[:, n * cb:(n + 1) * cb]
                spec_ref[k * blk + FFT_N2:(k + 1) * blk, :] = ci[:, n * cb:(n + 1) * cb]

        s_next = stage_b(groups[0])
        for gi, ks in enumerate(groups):
            s_cur = s_next
            if gi + 1 < len(groups):
                s_next = stage_b(groups[gi + 1])
            filter_and_invert(ks, s_cur)

        def store_y(rows, y):
            y_ref[rows, :] = y

        _fft_stage_a_inv(spec_ref, mai_ref, store_y, l)

        u_ref, w_ref, b_ref = gates[o]
        dbias = db_ref[o:o + 1, :]
        for r0 in range(0, l, CONV_ROWS):
            rows = slice(r0, r0 + CONV_ROWS)
            znew = _short_conv_chunk(u_ref, w_ref, b_ref, r0, l) * (y_ref[rows, :] + z_ref[rows, :] * dbias)
            if o + 1 < order:
                z_ref[rows, :] = znew
            else:
                o_ref[0, rows, :] = znew.astype(BF16)


def _hyena(u, conv_w, conv_b, dbias, hf, consts, order):
    b, l, _ = u.shape
    _, _, _, k1n, _ = _fft_dims(l)
    cb = HY_CB
    nblk = D_HYENA // cb
    srows = k1n * 2 * FFT_N2
    ma_fwd, ma_inv, dft3 = consts
    upart = lambda p: pl.BlockSpec((1, l, cb), lambda c, bi: (bi, 0, p * nblk + c))
    wpart = lambda p: pl.BlockSpec((3, cb), lambda c, bi: (0, p * nblk + c))
    bpart = lambda p: pl.BlockSpec((1, cb), lambda c, bi: (0, p * nblk + c))
    return pl.pallas_call(
        functools.partial(_hyena_body, l=l, order=order),
        grid=(nblk, b),
        in_specs=[
            upart(0), upart(1), upart(2), wpart(0), wpart(1), wpart(2), bpart(0), bpart(1), bpart(2),
            pl.BlockSpec((order, cb), lambda c, bi: (0, c)),
            pl.BlockSpec((order, srows, cb), lambda c, bi: (0, 0, c), pipeline_mode=pl.Buffered(1)),
            _const_spec(ma_fwd.shape), _const_spec(ma_inv.shape),
            _const_spec(dft3.shape),
        ],
        out_specs=pl.BlockSpec((1, l, cb), lambda c, bi: (bi, 0, c)),
        out_shape=jax.ShapeDtypeStruct((b, l, D_HYENA), BF16),
        scratch_shapes=[
            pltpu.VMEM((l, cb), F32),
            pltpu.VMEM((l, cb), F32),
            pltpu.VMEM((srows, cb), BF16),
        ],
        compiler_params=_params("arbitrary", "arbitrary"),
        name="hyena",
    )(u, u, u, conv_w, conv_w, conv_w, conv_b, conv_b, conv_b, dbias, hf, ma_fwd, ma_inv, dft3)


ATT_TQ = MXU_DIM
ATT_KEY_CHUNK = 2 * MXU_DIM
ATT_BLOCKS = 8


def _attn_body(qc_ref, qn_ref, k_ref, kn_ref, kc_ref, kcn_ref, vt_ref, vct_ref, lam_ref, sg_ref, o_ref,
               sa_ref, sb_ref, ma_ref, mb_ref):
    tq = ATT_TQ
    lat = k_ref.shape[1]
    lv = lam_ref[...]
    lam = (jnp.exp(jnp.sum(lv[0:1] * lv[1:2], axis=-1, keepdims=True))
           - jnp.exp(jnp.sum(lv[2:3] * lv[3:4], axis=-1, keepdims=True)) + LAM_INIT)

    n_ctx = kc_ref.shape[1]
    chunks = [(False, r, r, ATT_KEY_CHUNK) for r in range(0, lat, ATT_KEY_CHUNK)] + [(True, 0, lat, n_ctx)]

    def stage(qt, keys_ref, ctx_keys_ref, s_new, m_new, s_old, m_old, hook=None):
        m_run = acc = denom = None
        feat = lax.broadcasted_iota(jnp.int32, qt.shape, 0)
        qz = jnp.concatenate([jnp.where(feat < HEAD_DIM, qt, jnp.zeros_like(qt)),
                              jnp.where(feat >= HEAD_DIM, qt, jnp.zeros_like(qt))], axis=1)
        if s_old is not None:
            m_prev = m_old[...]
        for n, (is_ctx, src, dst, rows) in enumerate(chunks):
            keys = ctx_keys_ref[0, src:src + rows, :] if is_ctx else keys_ref[0, src:src + rows, :]
            s = _dot(keys, qz)
            s_new[dst:dst + rows, :] = s
            cmax = jnp.max(s, axis=0, keepdims=True)
            m_run = cmax if m_run is None else jnp.maximum(m_run, cmax)
            if s_old is not None:
                p = jnp.exp2(s_old[dst:dst + rows, :] - m_prev)
                psum = jnp.sum(p, axis=0, keepdims=True)
                vals = vct_ref[0, :, src:src + rows] if is_ctx else vt_ref[0, :, src:src + rows]
                part = _dot(vals, p.astype(BF16))
                denom = psum if denom is None else denom + psum
                acc = part if acc is None else acc + part
            if n == 0 and hook is not None:
                hook()
        m_new[...] = m_run
        return acc, denom

    def finish(acc, denom):
        r = 1.0 / denom
        ot = acc[:, :tq] * r[:, :tq] - acc[:, tq:] * (lam * r[:, tq:])
        return (_rms(ot.T, sg_ref[...]) * (1.0 - LAM_INIT)).astype(BF16)

    @pl.when(pl.program_id(0) == 0)
    def _():
        stage(qc_ref[0, :, :tq], k_ref, kc_ref, sa_ref, ma_ref, None, None)

    bufs = ((sa_ref, ma_ref), (sb_ref, mb_ref))
    pending = None
    for n in range(ATT_BLOCKS):
        if n + 1 < ATT_BLOCKS:
            new = (qc_ref[0, :, (n + 1) * tq:(n + 2) * tq], k_ref, kc_ref)
        else:
            new = (qn_ref[0, :, :tq], kn_ref, kcn_ref)
        raw = stage(*new, *bufs[(n + 1) % 2], *bufs[n % 2], hook=pending)

        def pending(raw=raw, n=n):
            o_ref[0, n * tq:(n + 1) * tq, :] = finish(*raw)

    pending()


def _attn(qt, k, vt, kc, vct, lamv, subln_g):
    b, l, _ = k.shape
    c = kc.shape[1]
    dv = 2 * HEAD_DIM
    tq = ATT_TQ
    tstep = ATT_BLOCKS * tq
    assert l % tstep == 0 and ATT_BLOCKS % 2 == 0
    pairs = l // tstep
    steps = b * N_HEADS * pairs

    def split(j):
        return j // (N_HEADS * pairs), (j // pairs) % N_HEADS, j % pairs

    def cur(fn):
        return lambda j: fn(*split(j))

    def nxt(fn):
        return lambda j: fn(*split(jnp.minimum(j + 1, steps - 1)))

    q_map = lambda bi, h, i: (bi, h, i)
    k_map = lambda bi, h, i: (bi, 0, h)
    v_map = lambda bi, h, i: (bi, h, 0)
    return pl.pallas_call(
        _attn_body,
        grid=(steps,),
        in_specs=[
            pl.BlockSpec((1, dv, tstep), cur(q_map)), pl.BlockSpec((1, dv, tstep), nxt(q_map)),
            pl.BlockSpec((1, l, dv), cur(k_map)), pl.BlockSpec((1, l, dv), nxt(k_map)),
            pl.BlockSpec((1, c, dv), cur(k_map)), pl.BlockSpec((1, c, dv), nxt(k_map)),
            pl.BlockSpec((1, dv, l), cur(v_map)), pl.BlockSpec((1, dv, c), cur(v_map)),
            _const_spec(lamv.shape), _const_spec(subln_g.shape),
        ],
        out_specs=pl.BlockSpec((1, tstep, dv), cur(lambda bi, h, i: (bi, i, h))),
        out_shape=jax.ShapeDtypeStruct((b, l, N_HEADS * dv), BF16),
        scratch_shapes=[pltpu.VMEM((l + c, 2 * tq), F32), pltpu.VMEM((l + c, 2 * tq), F32),
                        pltpu.VMEM((1, 2 * tq), F32), pltpu.VMEM((1, 2 * tq), F32)],
        compiler_params=_params("arbitrary"),
        name="attn",
    )(qt, qt, k, k, kc, kc, vt, vct, lamv, subln_g)


def _post_body(x_ref, zh_ref, oa_ref, gate_ref, g1_ref, sh2_ref, sc2_ref, g2_ref,
               gpost_ref, gfpre_ref, gfpost_ref,
               whu_ref, wau_ref, wo_ref, wg_ref, wu_ref, wd_ref, o_ref, *, ff_chunks):
    d = x_ref.shape[2]
    half = x_ref.shape[1] // 2
    rows_a, rows_b = slice(0, half), slice(half, 2 * half)

    def lifts(rows):
        return _dot(zh_ref[0, rows, :], whu_ref[...]), _dot(oa_ref[0, rows, :], wau_ref[...])

    def out_proj(rows, y_hy, y_att):
        mixed = gate_ref[0, rows, :d].astype(F32) * y_hy + gate_ref[0, rows, d:].astype(F32) * y_att
        return _dot(mixed.astype(BF16), wo_ref[...])

    def residual_and_norm(rows, mixed):
        x1 = x_ref[0, rows, :] + g1_ref[0] * _rms(mixed, gpost_ref[...])
        return x1, (_rms(x1, gfpre_ref[...]) * (1.0 + sc2_ref[0]) + sh2_ref[0]).astype(BF16)

    def ffn_chunk(hf, f, chunk):
        c0, cw = chunk
        a = _dot(hf, wg_ref[:, c0:c0 + cw])
        t = (a * jax.nn.sigmoid(a) * _dot(hf, wu_ref[:, c0:c0 + cw])).astype(BF16)
        part = _dot(t, wd_ref[c0:c0 + cw, :])
        return part if f is None else f + part

    def finish(rows, x1, f):
        o_ref[0, rows, :] = x1 + g2_ref[0] * _rms(f, gfpost_ref[...])

    mixed_a = out_proj(rows_a, *lifts(rows_a))
    lifted_b = lifts(rows_b)
    x1_a, hf_a = residual_and_norm(rows_a, mixed_a)
    mixed_b = out_proj(rows_b, *lifted_b)
    f_a = ffn_chunk(hf_a, None, ff_chunks[0])
    x1_b, hf_b = residual_and_norm(rows_b, mixed_b)
    for chunk in ff_chunks[1:]:
        f_a = ffn_chunk(hf_a, f_a, chunk)
    f_b = ffn_chunk(hf_b, None, ff_chunks[0])
    finish(rows_a, x1_a, f_a)
    for chunk in ff_chunks[1:]:
        f_b = ffn_chunk(hf_b, f_b, chunk)
    finish(rows_b, x1_b, f_b)


def _post(x, zh, oa, gate, g1, sh2, sc2, g2, gpost, gfpre, gfpost, whu, wau, wo, wg, wu, wd):
    b, l, d = x.shape
    dff = wg.shape[1]
    tm = 512
    ff_chunk = 4 * MXU_DIM
    ff_chunks = tuple((c0, min(ff_chunk, dff - c0)) for c0 in range(0, dff, ff_chunk))
    tile = lambda bi, i: (bi, i, 0)
    per_b = lambda bi, i: (bi, 0, 0)
    return pl.pallas_call(
        functools.partial(_post_body, ff_chunks=ff_chunks),
        grid=(b, l // tm),
        in_specs=[
            pl.BlockSpec((1, tm, d), tile),
            pl.BlockSpec((1, tm, zh.shape[2]), tile),
            pl.BlockSpec((1, tm, oa.shape[2]), tile),
            pl.BlockSpec((1, tm, gate.shape[2]), tile),
            pl.BlockSpec((1, 1, d), per_b), pl.BlockSpec((1, 1, d), per_b),
            pl.BlockSpec((1, 1, d), per_b), pl.BlockSpec((1, 1, d), per_b),
            _const_spec((1, d)), _const_spec((1, d)), _const_spec((1, d)),
        ] + [_const_spec(w.shape) for w in (whu, wau, wo, wg, wu, wd)],
        out_specs=pl.BlockSpec((1, tm, d), tile),
        out_shape=jax.ShapeDtypeStruct((b, l, d), F32),
        compiler_params=_params("arbitrary", "arbitrary"),
        name="post",
    )(x, zh, oa, gate, g1, sh2, sc2, g2, gpost, gfpre, gfpost, whu, wau, wo, wg, wu, wd)


@functools.lru_cache(maxsize=None)
def _rope_tables(l):
    quarter = HEAD_DIM // 4
    inv = ROPE_BASE ** (-np.arange(quarter, dtype=np.float64) / quarter)
    t = np.arange(l)
    ang_r = (t // GRID_W)[:, None] * inv[None, :]
    ang_c = (t % GRID_W)[:, None] * inv[None, :]
    ang = np.concatenate([ang_r, ang_r, ang_c, ang_c], axis=1)
    reps = D_ATTN // HEAD_DIM
    cos = np.tile(np.cos(ang), (1, reps)).astype(np.float32)
    second = (np.arange(D_ATTN) % (2 * quarter)) >= quarter
    sin = (np.tile(np.sin(ang), (1, reps)) * np.where(second, 1.0, -1.0)[None, :]).astype(np.float32)
    return cos, sin


@functools.lru_cache(maxsize=None)
def _filter_features(l, emb, kpad):
    bands = (emb - 1) // 2
    t = np.linspace(0.0, 1.0, l, dtype=np.float32).astype(np.float64)[:, None]
    w = (2.0 * math.pi / l) * np.arange(l, dtype=np.float64)[:, None]
    fr = np.linspace(1e-4, bands - 1, bands, dtype=np.float32).astype(np.float64)[None, :]
    z = np.concatenate([t, np.cos(fr * w), -np.sin(fr * w)], axis=-1)
    z = np.pad(z, ((0, 0), (0, kpad - emb))).astype(np.float32)
    deltas = np.abs(np.linspace(math.log(FILTER_TARGET) / SLOW_DECAY_PCT,
                                math.log(FILTER_TARGET) / FAST_DECAY_PCT, D_HYENA, dtype=np.float32))
    return z, deltas[None, :].astype(np.float32)


def _pad_to(a, rows, cols, value=0.0):
    return jnp.pad(a, ((0, rows - a.shape[0]), (0, cols - a.shape[1])), constant_values=value)


def kernel(x, c, ctx, c_ctx, w_ada, b_ada, g_mix_pre, g_mix_post, g_ffn_pre, g_ffn_post, w_in, hy_conv_w, hy_conv_b, hy_f_w1, hy_f_b1, hy_f_w2, hy_f_b2, hy_f_w3, hy_f_b3, hy_f_freq, hy_bias, lambda_q1, lambda_k1, lambda_q2, lambda_k2, att_subln_g, w_hy_up, w_att_up, w_out, w_ffn_gate, w_ffn_up, w_ffn_down):
    b, l, d = x.shape
    n_ctx = ctx.shape[1]
    order = hy_bias.shape[1]
    off_q = (order + 1) * D_HYENA
    off_k = off_q + D_ATTN
    off_v = off_k + D_ATTN
    off_g = off_v + D_ATTN
    assert w_in.shape[0] == 1, "single-layer block"
    assert l % CONV_ROWS == 0 and l % 512 == 0

    pad_rows = -(-(b + 1) // 8) * 8
    cc = jnp.concatenate([c, c_ctx[None, :], jnp.zeros((pad_rows - b - 1, d), F32)], axis=0)
    ada = _ada(cc, w_ada[0], b_ada[0][None, :])
    sh1, sc1, g1, sh2, sc2, g2 = (ada[:b, i * d:(i + 1) * d][:, None, :] for i in range(6))
    csh1, csc1 = ada[b:b + 1, 0:d], ada[b:b + 1, d:2 * d]

    row = lambda v: v[0][None, :]
    w = w_in[0]
    cos, sin = _rope_tables(l)
    q_scale = HEAD_DIM ** -0.5 * math.log2(math.e)
    wq32, wv32 = lax.optimization_barrier((w[:, off_q:off_k], w[:, off_v:off_g]))
    wqt = (wq32 * q_scale).T.astype(BF16)
    wk = w[:, off_k:off_v].astype(BF16)
    wvt = wv32.T.astype(BF16)

    kc, vct = _ctx_kv(ctx, row(g_mix_pre), csh1, csc1, wk, wvt)

    u, qt, k, vt, gate = _proj(x, row(g_mix_pre), sh1, sc1, jnp.asarray(cos), jnp.asarray(sin),
                               jnp.asarray(cos.T), jnp.asarray(sin.T),
                               w[:, :off_q].astype(BF16), wqt, wk, wvt, w[:, off_g:].astype(BF16))

    consts = tuple(jnp.asarray(a).astype(BF16) for a in _fft_constants(l))
    zfeat, deltas = _filter_features(l, hy_f_w1.shape[1], LANES)
    hf = _filt(jnp.asarray(zfeat),
               _pad_to(hy_f_w1[0], LANES, LANES), _pad_to(row(hy_f_b1), 1, LANES),
               _pad_to(hy_f_w2[0], LANES, LANES), _pad_to(row(hy_f_b2), 1, LANES),
               _pad_to(row(hy_f_freq), 1, LANES, 1.0),
               _pad_to(hy_f_w3[0], LANES, hy_f_w3.shape[2]), row(hy_f_b3),
               jnp.asarray(deltas), consts[0], consts[3], order, l)

    zh = _hyena(u, hy_conv_w[0], row(hy_conv_b), hy_bias[0], hf, consts[:3], order)

    lamv = jnp.concatenate([lambda_q1, lambda_k1, lambda_q2, lambda_k2], axis=0)
    oa = _attn(qt, k, vt, kc, vct, lamv, row(att_subln_g))

    return _post(x, zh, oa, gate, g1, sh2, sc2, g2, row(g_mix_post), row(g_ffn_pre), row(g_ffn_post),
                 w_hy_up[0].astype(BF16), w_att_up[0].astype(BF16), w_out[0].astype(BF16),
                 w_ffn_gate[0].astype(BF16), w_ffn_up[0].astype(BF16), w_ffn_down[0].astype(BF16))
```

```python
import functools
import math

import numpy as np
import jax
import jax.numpy as jnp
from jax import lax
from jax.experimental import pallas as pl
from jax.experimental.pallas import tpu as pltpu

F32 = jnp.float32
BF16 = jnp.bfloat16
HIGHEST = lax.Precision.HIGHEST

EPS = 1e-6
GRID_W = 64
HEAD_DIM = 64
N_HEADS = 4
D_HYENA = 512
D_ATTN = N_HEADS * 2 * HEAD_DIM
ROPE_BASE = 10000.0
FILTER_TARGET = 1e-2
FAST_DECAY_PCT = 0.3
SLOW_DECAY_PCT = 1.5
LAM_INIT = 0.8 - 0.6 * math.exp(0.0)

LANES = 128
MXU_DIM = 256
VMEM_LIMIT_BYTES = 56 * 1024 * 1024

FFT_N2 = 256
FFT_GROUP = 16
HY_CB = 256
FFT_GROUP_UNROLL = 8


def _dot(a, b):
    return jnp.dot(a, b, preferred_element_type=F32)


def _dot_hi(a, b):
    return jnp.dot(a, b, preferred_element_type=F32, precision=HIGHEST)


def _dot_nt(a, b):
    return lax.dot_general(a, b, (((1,), (1,)), ((), ())), preferred_element_type=F32)


def _rms(x, g):
    return x * lax.rsqrt(jnp.mean(x * x, axis=-1, keepdims=True) + EPS) * g


def _const_spec(shape):
    zeros = (0,) * len(shape)
    return pl.BlockSpec(shape, lambda *_: zeros, pipeline_mode=pl.Buffered(1))


def _params(*semantics):
    return pltpu.CompilerParams(dimension_semantics=semantics, vmem_limit_bytes=VMEM_LIMIT_BYTES)


def _ada_body(c_ref, w_ref, b_ref, o_ref):
    c = c_ref[...]
    o_ref[...] = _dot_hi(c * jax.nn.sigmoid(c), w_ref[...]) + b_ref[...]


def _ada(cc, w_ada, b_ada):
    rows, d = cc.shape
    n = w_ada.shape[1]
    tn = 1024
    return pl.pallas_call(
        _ada_body,
        grid=(n // tn,),
        in_specs=[
            pl.BlockSpec((rows, d), lambda j: (0, 0)),
            pl.BlockSpec((d, tn), lambda j: (0, j)),
            pl.BlockSpec((1, tn), lambda j: (0, j)),
        ],
        out_specs=pl.BlockSpec((rows, tn), lambda j: (0, j)),
        out_shape=jax.ShapeDtypeStruct((rows, n), F32),
        compiler_params=_params("arbitrary"),
        name="ada",
    )(cc, w_ada, b_ada)


def _ctx_kv_body(x_ref, g_ref, sh_ref, sc_ref, wk_ref, wq_ref, wv_ref, k_ref, vt_ref, wqt_ref, wvt_ref):
    @pl.when(pl.program_id(0) == 0)
    def _():
        wqt_ref[...] = wq_ref[...].astype(F32).T.astype(BF16)
        wvt_ref[...] = wv_ref[...].astype(F32).T.astype(BF16)

    h = _rms(x_ref[0], g_ref[...]) * (1.0 + sc_ref[...]) + sh_ref[...]
    hb = h.astype(BF16)
    k_ref[0] = _dot(hb, wk_ref[...]).astype(BF16)
    vt_ref[0] = _dot_nt(wvt_ref[...], hb).astype(BF16)


def _ctx_kv(ctx, g, sh, sc, wk, wq, wv):
    b, n_ctx, d = ctx.shape
    n = wk.shape[1]
    bf = lambda shape: jax.ShapeDtypeStruct(shape, BF16)
    whole_t = pl.BlockSpec((n, d), lambda i: (0, 0))
    return pl.pallas_call(
        _ctx_kv_body,
        grid=(b,),
        in_specs=[
            pl.BlockSpec((1, n_ctx, d), lambda i: (i, 0, 0)),
            _const_spec((1, d)), _const_spec((1, d)), _const_spec((1, d)),
            _const_spec((d, n)), _const_spec((d, n)), _const_spec((d, n)),
        ],
        out_specs=[pl.BlockSpec((1, n_ctx, n), lambda i: (i, 0, 0)),
                   pl.BlockSpec((1, n, n_ctx), lambda i: (i, 0, 0)), whole_t, whole_t],
        out_shape=[bf((b, n_ctx, n)), bf((b, n, n_ctx)), bf((n, d)), bf((n, d))],
        compiler_params=_params("arbitrary"),
        name="ctx_kv",
    )(ctx, g, sh, sc, wk, wq, wv)


ROT = HEAD_DIM // 4


def _proj_body(x_ref, g_ref, sh_ref, sc_ref, cos_ref, sin_ref, cost_ref, sint_ref,
               wu_ref, wqt_ref, wk_ref, wvt_ref, wg_ref,
               u_ref, qt_ref, k_ref, vt_ref, gate_ref):
    half = x_ref.shape[1] // 2
    rows_a, rows_b = slice(0, half), slice(half, 2 * half)

    def normed(rows):
        return (_rms(x_ref[0, rows, :], g_ref[...]) * (1.0 + sc_ref[0]) + sh_ref[0]).astype(BF16)

    def hyena_in(rows, hb):
        u_ref[0, rows, :] = _dot(hb, wu_ref[...]).astype(BF16)

    def gates_and_keys(rows, hb):
        gate_ref[0, rows, :] = jax.nn.sigmoid(_dot(hb, wg_ref[...])).astype(BF16)
        kv = _dot(hb, wk_ref[...])
        lane = lax.broadcasted_iota(jnp.int32, (kv.shape[0], LANES), 1)
        first = (lane % (2 * ROT)) < ROT
        slabs = []
        for c0 in range(0, kv.shape[1], LANES):
            slab = kv[:, c0:c0 + LANES]
            slabs.append(jnp.where(first, pltpu.roll(slab, LANES - ROT, axis=1), pltpu.roll(slab, ROT, axis=1)))
        k_ref[0, rows, :] = (kv * cos_ref[rows, :]
                             + jnp.concatenate(slabs, axis=1) * sin_ref[rows, :]).astype(BF16)

    hb_a = normed(rows_a)
    hyena_in(rows_a, hb_a)
    hb_b = normed(rows_b)
    gates_and_keys(rows_a, hb_a)
    hyena_in(rows_b, hb_b)
    gates_and_keys(rows_b, hb_b)
    hb = jnp.concatenate([hb_a, hb_b], axis=0)
    qv = _dot_nt(wqt_ref[...], hb)
    partner = jnp.concatenate(
        [qv[r + ROT:r + 2 * ROT] if part == 0 else qv[r:r + ROT]
         for r in range(0, qv.shape[0], 2 * ROT) for part in (0, 1)], axis=0)
    qt_ref[0] = (qv * cost_ref[...] + partner * sint_ref[...]).astype(BF16)
    vt_ref[0] = _dot_nt(wvt_ref[...], hb).astype(BF16)


def _proj(x, g, sh, sc, cos, sin, cost, sint, wu, wqt, wk, wvt, wg):
    b, l, d = x.shape
    tm = 512
    tile = lambda i, bi: (bi, i, 0)
    tile_t = lambda i, bi: (bi, 0, i)
    per_b = lambda i, bi: (bi, 0, 0)
    bf = lambda shape: jax.ShapeDtypeStruct(shape, BF16)
    n_u, n_g = wu.shape[1], wg.shape[1]
    return pl.pallas_call(
        _proj_body,
        grid=(l // tm, b),
        in_specs=[
            pl.BlockSpec((1, tm, d), tile),
            _const_spec((1, d)),
            pl.BlockSpec((1, 1, d), per_b), pl.BlockSpec((1, 1, d), per_b),
            pl.BlockSpec((tm, D_ATTN), lambda i, bi: (i, 0)), pl.BlockSpec((tm, D_ATTN), lambda i, bi: (i, 0)),
            pl.BlockSpec((D_ATTN, tm), lambda i, bi: (0, i)), pl.BlockSpec((D_ATTN, tm), lambda i, bi: (0, i)),
        ] + [_const_spec(w.shape) for w in (wu, wqt, wk, wvt, wg)],
        out_specs=[pl.BlockSpec((1, tm, n_u), tile), pl.BlockSpec((1, D_ATTN, tm), tile_t),
                   pl.BlockSpec((1, tm, D_ATTN), tile), pl.BlockSpec((1, D_ATTN, tm), tile_t),
                   pl.BlockSpec((1, tm, n_g), tile)],
        out_shape=[bf((b, l, n_u)), bf((b, D_ATTN, l)), bf((b, l, D_ATTN)), bf((b, D_ATTN, l)), bf((b, l, n_g))],
        compiler_params=_params("arbitrary", "arbitrary"),
        name="proj",
    )(x, g, sh, sc, cos, sin, cost, sint, wu, wqt, wk, wvt, wg)


def _fft_dims(l):
    n = 2 * l
    n1 = n // FFT_N2
    h1 = n1 // 2
    k1 = h1 + 1
    ng = FFT_N2 // FFT_GROUP
    return n, n1, h1, k1, ng


@functools.lru_cache(maxsize=None)
def _fft_constants(l):
    n, n1, h1, k1n, ng = _fft_dims(l)
    g = FFT_GROUP
    ma = np.zeros((ng, k1n, 2, g, h1, g), np.float64)
    for gi in range(ng):
        for j in range(g):
            n2 = gi * g + j
            nn = FFT_N2 * np.arange(h1)[None, :] + n2
            theta = 2.0 * np.pi * ((np.arange(k1n)[:, None] * nn) % n) / n
            ma[gi, :, 0, j, :, j] = np.cos(theta)
            ma[gi, :, 1, j, :, j] = -np.sin(theta)
    weight = np.full((k1n,), 2.0)
    weight[0] = weight[-1] = 1.0
    ma_fwd = ma.reshape(ng, k1n * 2 * g, h1 * g)
    ma_inv = (ma * (weight / n)[None, :, None, None, None, None]).reshape(ng, k1n * 2 * g, h1 * g)
    ma_inv = np.transpose(ma_inv, (0, 2, 1))
    kpad = -(-ma_inv.shape[2] // LANES) * LANES
    ma_inv = np.pad(ma_inv, ((0, 0), (0, 0), (0, kpad - ma_inv.shape[2])))
    ang = 2.0 * np.pi * ((np.arange(FFT_N2)[:, None] * np.arange(FFT_N2)[None, :]) % FFT_N2) / FFT_N2
    fr, fi = np.cos(ang), -np.sin(ang)
    dft3 = np.stack([fr, fi, fr + fi, fr - fi])
    mb_filt = np.block([[fr, -fi, fr, -fi], [fi, fr, -fi, -fr]])
    return tuple(np.asarray(a, np.float32) for a in (ma_fwd, ma_inv, dft3, mb_filt))


def _fft_stage_a(src_ref, ma_ref, spec_ref, l):
    _, _, h1, k1n, ng = _fft_dims(l)
    g = FFT_GROUP

    def group(gi, carry):
        off = gi * g
        xg = jnp.concatenate(
            [src_ref[pl.ds(pl.multiple_of(FFT_N2 * i + off, g), g), :] for i in range(h1)], axis=0)
        bg = _dot(ma_ref[gi], xg.astype(BF16)).astype(BF16)
        for c in range(2 * k1n):
            spec_ref[pl.ds(pl.multiple_of(c * FFT_N2 + off, g), g), :] = bg[c * g:(c + 1) * g]
        return carry

    lax.fori_loop(0, ng, group, 0, unroll=FFT_GROUP_UNROLL)


def _fft_stage_a_inv(spec_ref, ma_inv_ref, emit, l):
    _, _, h1, k1n, ng = _fft_dims(l)
    g = FFT_GROUP
    kpad = ma_inv_ref.shape[2] - 2 * k1n * g

    def group(gi, carry):
        off = gi * g
        chunks = [spec_ref[pl.ds(pl.multiple_of(c * FFT_N2 + off, g), g), :] for c in range(2 * k1n)]
        if kpad:
            chunks.append(jnp.zeros((kpad, spec_ref.shape[1]), BF16))
        yg = _dot(ma_inv_ref[gi], jnp.concatenate(chunks, axis=0))
        for i in range(h1):
            emit(pl.ds(pl.multiple_of(FFT_N2 * i + off, g), g), yg[i * g:(i + 1) * g])
        return carry

    lax.fori_loop(0, ng, group, 0, unroll=FFT_GROUP_UNROLL)


def _filt_body(z_ref, w1_ref, b1_ref, w2_ref, b2_ref, fq_ref, w3f_ref, b3f_ref, w3b_ref, b3b_ref,
               dl_ref, ma_ref, mb_ref, hf_ref, hdn_ref, sig_ref, specf_ref, specb_ref, *, l):
    _, _, _, k1n, _ = _fft_dims(l)

    @pl.when((pl.program_id(0) == 0) & (pl.program_id(1) == 0))
    def _():
        fq = fq_ref[...]
        h1 = jnp.sin(fq * (_dot_hi(z_ref[...], w1_ref[...]) + b1_ref[...]))
        hdn_ref[...] = jnp.sin(fq * (_dot_hi(h1, w2_ref[...]) + b2_ref[...])).astype(BF16)

    hdn = hdn_ref[...]
    cb = specf_ref.shape[1]
    blk = 2 * FFT_N2
    rows = lax.broadcasted_iota(jnp.int32, (l, cb), 0)
    window = jnp.exp(rows.astype(F32) * (-1.0 / (l - 1)) * dl_ref[...])

    sig_ref[...] = (_dot(hdn, w3f_ref[...].astype(BF16)) + b3f_ref[...]) * window
    _fft_stage_a(sig_ref, ma_ref, specf_ref, l)
    sig_ref[...] = jnp.where(rows > 0, (_dot(hdn, w3b_ref[...].astype(BF16)) + b3b_ref[...]) * window, 0.0)
    _fft_stage_a(sig_ref, ma_ref, specb_ref, l)

    def blocks(starts):
        x = jnp.concatenate(
            [jnp.concatenate([specf_ref[pl.ds(st, blk), :], specb_ref[pl.ds(st, blk), :]], axis=0)
             for st in starts], axis=1)
        s = _dot(mb_ref[...], x).astype(BF16)
        for n, st in enumerate(starts):
            hf_ref[0, pl.ds(st, blk), :] = s[:, n * cb:(n + 1) * cb]

    def per_pair(i, carry):
        st = pl.multiple_of(i * 2 * blk, blk)
        blocks([st, st + blk])
        return carry

    lax.fori_loop(0, k1n // 2, per_pair, 0)
    if k1n % 2:
        blocks([(k1n - 1) * blk])


def _filt(zfeat, w1, b1, w2, b2, fq, w3, b3, deltas, ma_fwd, mb_filt, order, l):
    _, _, _, k1n, _ = _fft_dims(l)
    cb = HY_CB
    nblk = D_HYENA // cb
    srows = k1n * 2 * FFT_N2
    fcol = lambda o, c: (0, (2 * o) * nblk + c)
    bcol = lambda o, c: (0, (2 * o + 1) * nblk + c)
    return pl.pallas_call(
        functools.partial(_filt_body, l=l),
        grid=(order, nblk),
        in_specs=[
            _const_spec(zfeat.shape), _const_spec(w1.shape), _const_spec(b1.shape),
            _const_spec(w2.shape), _const_spec(b2.shape), _const_spec(fq.shape),
            pl.BlockSpec((w3.shape[0], cb), fcol), pl.BlockSpec((1, cb), fcol),
            pl.BlockSpec((w3.shape[0], cb), bcol), pl.BlockSpec((1, cb), bcol),
            pl.BlockSpec((1, cb), lambda o, c: (0, c)),
            _const_spec(ma_fwd.shape), _const_spec(mb_filt.shape),
        ],
        out_specs=pl.BlockSpec((1, srows, cb), lambda o, c: (o, 0, c)),
        out_shape=jax.ShapeDtypeStruct((order, srows, D_HYENA), BF16),
        scratch_shapes=[
            pltpu.VMEM((l, w2.shape[1]), BF16),
            pltpu.VMEM((l, cb), F32),
            pltpu.VMEM((srows, cb), BF16),
            pltpu.VMEM((srows, cb), BF16),
        ],
        compiler_params=_params("arbitrary", "arbitrary"),
        name="filt",
    )(zfeat, w1, b1, w2, b2, fq, w3, b3, w3, b3, deltas, ma_fwd, mb_filt)


CONV_ROWS = 512
HALO = 16


def _short_conv_chunk(u_ref, w_ref, b_ref, r0, l):
    c = u_ref.shape[2]
    zeros = jnp.zeros((HALO, c), F32)
    before = zeros if r0 == 0 else u_ref[0, r0 - HALO:r0, :].astype(F32)
    after = zeros if r0 + CONV_ROWS == l else u_ref[0, r0 + CONV_ROWS:r0 + CONV_ROWS + HALO, :].astype(F32)
    ext = jnp.concatenate([before, u_ref[0, r0:r0 + CONV_ROWS, :].astype(F32), after], axis=0)
    w = w_ref[...]
    n = ext.shape[0]
    prev = pltpu.roll(ext, 1, axis=0)[HALO:HALO + CONV_ROWS]
    nxt = pltpu.roll(ext, n - 1, axis=0)[HALO:HALO + CONV_ROWS]
    return prev * w[0:1] + ext[HALO:HALO + CONV_ROWS] * w[1:2] + nxt * w[2:3] + b_ref[...]


def _hyena_body(u0_ref, u1_ref, u2_ref, w0_ref, w1_ref, w2_ref, b0_ref, b1_ref, b2_ref, db_ref, hf_ref,
                ma_ref, mai_ref, dft_ref, o_ref, z_ref, y_ref, spec_ref, *, l, order):
    _, _, _, k1n, _ = _fft_dims(l)
    blk = 2 * FFT_N2
    gates = ((u1_ref, w1_ref, b1_ref), (u2_ref, w2_ref, b2_ref))

    for r0 in range(0, l, CONV_ROWS):
        z_ref[r0:r0 + CONV_ROWS, :] = _short_conv_chunk(u0_ref, w0_ref, b0_ref, r0, l)

    for o in range(order):
        _fft_stage_a(z_ref, ma_ref, spec_ref, l)

        cb = spec_ref.shape[1]
        groups = [list(range(i, min(i + 2, k1n))) for i in range(0, k1n, 2)]

        def stage_b(ks):
            br = jnp.concatenate([spec_ref[k * blk:k * blk + FFT_N2, :] for k in ks], axis=1)
            bi = jnp.concatenate([spec_ref[k * blk + FFT_N2:(k + 1) * blk, :] for k in ks], axis=1)
            k1 = _dot(dft_ref[2], br)
            return k1 - _dot(dft_ref[1], br + bi), k1 + _dot(dft_ref[0], bi - br)

        def filter_and_invert(ks, s, o=o):
            sr, si = s
            hr = jnp.concatenate([hf_ref[o, k * blk:k * blk + FFT_N2, :] for k in ks], axis=1).astype(F32)
            hi = jnp.concatenate([hf_ref[o, k * blk + FFT_N2:(k + 1) * blk, :] for k in ks], axis=1).astype(F32)
            yr = sr * hr - si * hi
            yi = sr * hi + si * hr
            k1 = _dot(dft_ref[3], yr.astype(BF16))
            cr = (k1 + _dot(dft_ref[1], (yr + yi).astype(BF16))).astype(BF16)
            ci = (k1 + _dot(dft_ref[0], (yi - yr).astype(BF16))).astype(BF16)
            for n, k in enumerate(ks):
                spec_ref[k * blk:k * blk + FFT_N2, :] = cr[:, n * cb:(n + 1) * cb]
                spec_ref[k * blk + FFT_N2:(k + 1) * blk, :] = ci[:, n * cb:(n + 1) * cb]

        s_next = stage_b(groups[0])
        for gi, ks in enumerate(groups):
            s_cur = s_next
            if gi + 1 < len(groups):
                s_next = stage_b(groups[gi + 1])
            filter_and_invert(ks, s_cur)

        def store_y(rows, y):
            y_ref[rows, :] = y

        _fft_stage_a_inv(spec_ref, mai_ref, store_y, l)

        u_ref, w_ref, b_ref = gates[o]
        dbias = db_ref[o:o + 1, :]
        for r0 in range(0, l, CONV_ROWS):
            rows = slice(r0, r0 + CONV_ROWS)
            znew = _short_conv_chunk(u_ref, w_ref, b_ref, r0, l) * (y_ref[rows, :] + z_ref[rows, :] * dbias)
            if o + 1 < order:
                z_ref[rows, :] = znew
            else:
                o_ref[0, rows, :] = znew.astype(BF16)


def _hyena(u, conv_w, conv_b, dbias, hf, consts, order):
    b, l, _ = u.shape
    _, _, _, k1n, _ = _fft_dims(l)
    cb = HY_CB
    nblk = D_HYENA // cb
    srows = k1n * 2 * FFT_N2
    ma_fwd, ma_inv, dft3 = consts
    upart = lambda p: pl.BlockSpec((1, l, cb), lambda c, bi: (bi, 0, p * nblk + c))
    wpart = lambda p: pl.BlockSpec((3, cb), lambda c, bi: (0, p * nblk + c))
    bpart = lambda p: pl.BlockSpec((1, cb), lambda c, bi: (0, p * nblk + c))
    return pl.pallas_call(
        functools.partial(_hyena_body, l=l, order=order),
        grid=(nblk, b),
        in_specs=[
            upart(0), upart(1), upart(2), wpart(0), wpart(1), wpart(2), bpart(0), bpart(1), bpart(2),
            pl.BlockSpec((order, cb), lambda c, bi: (0, c)),
            pl.BlockSpec((order, srows, cb), lambda c, bi: (0, 0, c), pipeline_mode=pl.Buffered(1)),
            _const_spec(ma_fwd.shape), _const_spec(ma_inv.shape),
            _const_spec(dft3.shape),
        ],
        out_specs=pl.BlockSpec((1, l, cb), lambda c, bi: (bi, 0, c)),
        out_shape=jax.ShapeDtypeStruct((b, l, D_HYENA), BF16),
        scratch_shapes=[
            pltpu.VMEM((l, cb), F32),
            pltpu.VMEM((l, cb), F32),
            pltpu.VMEM((srows, cb), BF16),
        ],
        compiler_params=_params("arbitrary", "arbitrary"),
        name="hyena",
    )(u, u, u, conv_w, conv_w, conv_w, conv_b, conv_b, conv_b, dbias, hf, ma_fwd, ma_inv, dft3)


ATT_TQ = MXU_DIM
ATT_KEY_CHUNK = 2 * MXU_DIM
ATT_BLOCKS = 8


def _attn_body(qc_ref, qn_ref, k_ref, kn_ref, kc_ref, kcn_ref, vt_ref, vct_ref, lam_ref, sg_ref, o_ref,
               sa_ref, sb_ref, ma_ref, mb_ref):
    tq = ATT_TQ
    lat = k_ref.shape[1]
    lv = lam_ref[...]
    lam = (jnp.exp(jnp.sum(lv[0:1] * lv[1:2], axis=-1, keepdims=True))
           - jnp.exp(jnp.sum(lv[2:3] * lv[3:4], axis=-1, keepdims=True)) + LAM_INIT)

    n_ctx = kc_ref.shape[1]
    chunks = [(False, r, r, ATT_KEY_CHUNK) for r in range(0, lat, ATT_KEY_CHUNK)] + [(True, 0, lat, n_ctx)]

    def stage(qt, keys_ref, ctx_keys_ref, s_new, m_new, s_old, m_old, hook=None):
        m_run = acc = denom = None
        feat = lax.broadcasted_iota(jnp.int32, qt.shape, 0)
        qz = jnp.concatenate([jnp.where(feat < HEAD_DIM, qt, jnp.zeros_like(qt)),
                              jnp.where(feat >= HEAD_DIM, qt, jnp.zeros_like(qt))], axis=1)
        if s_old is not None:
            m_prev = m_old[...]
        for n, (is_ctx, src, dst, rows) in enumerate(chunks):
            keys = ctx_keys_ref[0, src:src + rows, :] if is_ctx else keys_ref[0, src:src + rows, :]
            s = _dot(keys, qz)
            s_new[dst:dst + rows, :] = s
            cmax = jnp.max(s, axis=0, keepdims=True)
            m_run = cmax if m_run is None else jnp.maximum(m_run, cmax)
            if s_old is not None:
                p = jnp.exp2(s_old[dst:dst + rows, :] - m_prev)
                psum = jnp.sum(p, axis=0, keepdims=True)
                vals = vct_ref[0, :, src:src + rows] if is_ctx else vt_ref[0, :, src:src + rows]
                part = _dot(vals, p.astype(BF16))
                denom = psum if denom is None else denom + psum
                acc = part if acc is None else acc + part
            if n == 0 and hook is not None:
                hook()
        m_new[...] = m_run
        return acc, denom

    def finish(acc, denom):
        r = 1.0 / denom
        ot = acc[:, :tq] * r[:, :tq] - acc[:, tq:] * (lam * r[:, tq:])
        return (_rms(ot.T, sg_ref[...]) * (1.0 - LAM_INIT)).astype(BF16)

    @pl.when(pl.program_id(0) == 0)
    def _():
        stage(qc_ref[0, :, :tq], k_ref, kc_ref, sa_ref, ma_ref, None, None)

    bufs = ((sa_ref, ma_ref), (sb_ref, mb_ref))
    pending = None
    for n in range(ATT_BLOCKS):
        if n + 1 < ATT_BLOCKS:
            new = (qc_ref[0, :, (n + 1) * tq:(n + 2) * tq], k_ref, kc_ref)
        else:
            new = (qn_ref[0, :, :tq], kn_ref, kcn_ref)
        raw = stage(*new, *bufs[(n + 1) % 2], *bufs[n % 2], hook=pending)

        def pending(raw=raw, n=n):
            o_ref[0, n * tq:(n + 1) * tq, :] = finish(*raw)

    pending()


def _attn(qt, k, vt, kc, vct, lamv, subln_g):
    b, l, _ = k.shape
    c = kc.shape[1]
    dv = 2 * HEAD_DIM
    tq = ATT_TQ
    tstep = ATT_BLOCKS * tq
    assert l % tstep == 0 and ATT_BLOCKS % 2 == 0
    pairs = l // tstep
    steps = b * N_HEADS * pairs

    def split(j):
        return j // (N_HEADS * pairs), (j // pairs) % N_HEADS, j % pairs

    def cur(fn):
        return lambda j: fn(*split(j))

    def nxt(fn):
        return lambda j: fn(*split(jnp.minimum(j + 1, steps - 1)))

    q_map = lambda bi, h, i: (bi, h, i)
    k_map = lambda bi, h, i: (bi, 0, h)
    v_map = lambda bi, h, i: (bi, h, 0)
    return pl.pallas_call(
        _attn_body,
        grid=(steps,),
        in_specs=[
            pl.BlockSpec((1, dv, tstep), cur(q_map)), pl.BlockSpec((1, dv, tstep), nxt(q_map)),
            pl.BlockSpec((1, l, dv), cur(k_map)), pl.BlockSpec((1, l, dv), nxt(k_map)),
            pl.BlockSpec((1, c, dv), cur(k_map)), pl.BlockSpec((1, c, dv), nxt(k_map)),
            pl.BlockSpec((1, dv, l), cur(v_map)), pl.BlockSpec((1, dv, c), cur(v_map)),
            _const_spec(lamv.shape), _const_spec(subln_g.shape),
        ],
        out_specs=pl.BlockSpec((1, tstep, dv), cur(lambda bi, h, i: (bi, i, h))),
        out_shape=jax.ShapeDtypeStruct((b, l, N_HEADS * dv), BF16),
        scratch_shapes=[pltpu.VMEM((l + c, 2 * tq), F32), pltpu.VMEM((l + c, 2 * tq), F32),
                        pltpu.VMEM((1, 2 * tq), F32), pltpu.VMEM((1, 2 * tq), F32)],
        compiler_params=_params("arbitrary"),
        name="attn",
    )(qt, qt, k, k, kc, kc, vt, vct, lamv, subln_g)


def _post_body(x_ref, zh_ref, oa_ref, gate_ref, g1_ref, sh2_ref, sc2_ref, g2_ref,
               gpost_ref, gfpre_ref, gfpost_ref,
               whu_ref, wau_ref, wo_ref, wg_ref, wu_ref, wd_ref, o_ref, *, ff_chunks):
    d = x_ref.shape[2]
    half = x_ref.shape[1] // 2
    rows_a, rows_b = slice(0, half), slice(half, 2 * half)

    def lifts(rows):
        return _dot(zh_ref[0, rows, :], whu_ref[...]), _dot(oa_ref[0, rows, :], wau_ref[...])

    def out_proj(rows, y_hy, y_att):
        mixed = gate_ref[0, rows, :d].astype(F32) * y_hy + gate_ref[0, rows, d:].astype(F32) * y_att
        return _dot(mixed.astype(BF16), wo_ref[...])

    def residual_and_norm(rows, mixed):
        x1 = x_ref[0, rows, :] + g1_ref[0] * _rms(mixed, gpost_ref[...])
        return x1, (_rms(x1, gfpre_ref[...]) * (1.0 + sc2_ref[0]) + sh2_ref[0]).astype(BF16)

    def ffn_chunk(hf, f, chunk):
        c0, cw = chunk
        a = _dot(hf, wg_ref[:, c0:c0 + cw])
        t = (a * jax.nn.sigmoid(a) * _dot(hf, wu_ref[:, c0:c0 + cw])).astype(BF16)
        part = _dot(t, wd_ref[c0:c0 + cw, :])
        return part if f is None else f + part

    def finish(rows, x1, f):
        o_ref[0, rows, :] = x1 + g2_ref[0] * _rms(f, gfpost_ref[...])

    mixed_a = out_proj(rows_a, *lifts(rows_a))
    lifted_b = lifts(rows_b)
    x1_a, hf_a = residual_and_norm(rows_a, mixed_a)
    mixed_b = out_proj(rows_b, *lifted_b)
    f_a = ffn_chunk(hf_a, None, ff_chunks[0])
    x1_b, hf_b = residual_and_norm(rows_b, mixed_b)
    for chunk in ff_chunks[1:]:
        f_a = ffn_chunk(hf_a, f_a, chunk)
    f_b = ffn_chunk(hf_b, None, ff_chunks[0])
    finish(rows_a, x1_a, f_a)
    for chunk in ff_chunks[1:]:
        f_b = ffn_chunk(hf_b, f_b, chunk)
    finish(rows_b, x1_b, f_b)


def _post(x, zh, oa, gate, g1, sh2, sc2, g2, gpost, gfpre, gfpost, whu, wau, wo, wg, wu, wd):
    b, l, d = x.shape
    dff = wg.shape[1]
    tm = 512
    ff_chunk = 4 * MXU_DIM
    ff_chunks = tuple((c0, min(ff_chunk, dff - c0)) for c0 in range(0, dff, ff_chunk))
    tile = lambda bi, i: (bi, i, 0)
    per_b = lambda bi, i: (bi, 0, 0)
    return pl.pallas_call(
        functools.partial(_post_body, ff_chunks=ff_chunks),
        grid=(b, l // tm),
        in_specs=[
            pl.BlockSpec((1, tm, d), tile),
            pl.BlockSpec((1, tm, zh.shape[2]), tile),
            pl.BlockSpec((1, tm, oa.shape[2]), tile),
            pl.BlockSpec((1, tm, gate.shape[2]), tile),
            pl.BlockSpec((1, 1, d), per_b), pl.BlockSpec((1, 1, d), per_b),
            pl.BlockSpec((1, 1, d), per_b), pl.BlockSpec((1, 1, d), per_b),
            _const_spec((1, d)), _const_spec((1, d)), _const_spec((1, d)),
        ] + [_const_spec(w.shape) for w in (whu, wau, wo, wg, wu, wd)],
        out_specs=pl.BlockSpec((1, tm, d), tile),
        out_shape=jax.ShapeDtypeStruct((b, l, d), F32),
        compiler_params=_params("arbitrary", "arbitrary"),
        name="post",
    )(x, zh, oa, gate, g1, sh2, sc2, g2, gpost, gfpre, gfpost, whu, wau, wo, wg, wu, wd)


@functools.lru_cache(maxsize=None)
def _rope_tables(l):
    quarter = HEAD_DIM // 4
    inv = ROPE_BASE ** (-np.arange(quarter, dtype=np.float64) / quarter)
    t = np.arange(l)
    ang_r = (t // GRID_W)[:, None] * inv[None, :]
    ang_c = (t % GRID_W)[:, None] * inv[None, :]
    ang = np.concatenate([ang_r, ang_r, ang_c, ang_c], axis=1)
    reps = D_ATTN // HEAD_DIM
    cos = np.tile(np.cos(ang), (1, reps)).astype(np.float32)
    second = (np.arange(D_ATTN) % (2 * quarter)) >= quarter
    sin = (np.tile(np.sin(ang), (1, reps)) * np.where(second, 1.0, -1.0)[None, :]).astype(np.float32)
    return cos, sin


@functools.lru_cache(maxsize=None)
def _filter_features(l, emb, kpad):
    bands = (emb - 1) // 2
    t = np.linspace(0.0, 1.0, l, dtype=np.float32).astype(np.float64)[:, None]
    w = (2.0 * math.pi / l) * np.arange(l, dtype=np.float64)[:, None]
    fr = np.linspace(1e-4, bands - 1, bands, dtype=np.float32).astype(np.float64)[None, :]
    z = np.concatenate([t, np.cos(fr * w), -np.sin(fr * w)], axis=-1)
    z = np.pad(z, ((0, 0), (0, kpad - emb))).astype(np.float32)
    deltas = np.abs(np.linspace(math.log(FILTER_TARGET) / SLOW_DECAY_PCT,
                                math.log(FILTER_TARGET) / FAST_DECAY_PCT, D_HYENA, dtype=np.float32))
    return z, deltas[None, :].astype(np.float32)


def _pad_to(a, rows, cols, value=0.0):
    return jnp.pad(a, ((0, rows - a.shape[0]), (0, cols - a.shape[1])), constant_values=value)


def kernel(x, c, ctx, c_ctx, w_ada, b_ada, g_mix_pre, g_mix_post, g_ffn_pre, g_ffn_post, w_in, hy_conv_w, hy_conv_b, hy_f_w1, hy_f_b1, hy_f_w2, hy_f_b2, hy_f_w3, hy_f_b3, hy_f_freq, hy_bias, lambda_q1, lambda_k1, lambda_q2, lambda_k2, att_subln_g, w_hy_up, w_att_up, w_out, w_ffn_gate, w_ffn_up, w_ffn_down):
    b, l, d = x.shape
    n_ctx = ctx.shape[1]
    order = hy_bias.shape[1]
    off_q = (order + 1) * D_HYENA
    off_k = off_q + D_ATTN
    off_v = off_k + D_ATTN
    off_g = off_v + D_ATTN
    assert w_in.shape[0] == 1, "single-layer block"
    assert l % CONV_ROWS == 0 and l % 512 == 0

    pad_rows = -(-(b + 1) // 8) * 8
    cc = jnp.concatenate([c, c_ctx[None, :], jnp.zeros((pad_rows - b - 1, d), F32)], axis=0)
    ada = _ada(cc, w_ada[0], b_ada[0][None, :])
    sh1, sc1, g1, sh2, sc2, g2 = (ada[:b, i * d:(i + 1) * d][:, None, :] for i in range(6))
    csh1, csc1 = ada[b:b + 1, 0:d], ada[b:b + 1, d:2 * d]

    row = lambda v: v[0][None, :]
    w = w_in[0]
    cos, sin = _rope_tables(l)
    q_scale = HEAD_DIM ** -0.5 * math.log2(math.e)
    wq = (w[:, off_q:off_k] * q_scale).astype(BF16)
    wk = w[:, off_k:off_v].astype(BF16)
    wv = w[:, off_v:off_g].astype(BF16)

    kc, vct, wqt, wvt = _ctx_kv(ctx, row(g_mix_pre), csh1, csc1, wk, wq, wv)

    u, qt, k, vt, gate = _proj(x, row(g_mix_pre), sh1, sc1, jnp.asarray(cos), jnp.asarray(sin),
                               jnp.asarray(cos.T), jnp.asarray(sin.T),
                               w[:, :off_q].astype(BF16), wqt, wk, wvt, w[:, off_g:].astype(BF16))

    consts = tuple(jnp.asarray(a).astype(BF16) for a in _fft_constants(l))
    zfeat, deltas = _filter_features(l, hy_f_w1.shape[1], LANES)
    hf = _filt(jnp.asarray(zfeat),
               _pad_to(hy_f_w1[0], LANES, LANES), _pad_to(row(hy_f_b1), 1, LANES),
               _pad_to(hy_f_w2[0], LANES, LANES), _pad_to(row(hy_f_b2), 1, LANES),
               _pad_to(row(hy_f_freq), 1, LANES, 1.0),
               _pad_to(hy_f_w3[0], LANES, hy_f_w3.shape[2]), row(hy_f_b3),
               jnp.asarray(deltas), consts[0], consts[3], order, l)

    zh = _hyena(u, hy_conv_w[0], row(hy_conv_b), hy_bias[0], hf, consts[:3], order)

    lamv = jnp.concatenate([lambda_q1, lambda_k1, lambda_q2, lambda_k2], axis=0)
    oa = _attn(qt, k, vt, kc, vct, lamv, row(att_subln_g))

    return _post(x, zh, oa, gate, g1, sh2, sc2, g2, row(g_mix_post), row(g_ffn_pre), row(g_ffn_post),
                 w_hy_up[0].astype(BF16), w_att_up[0].astype(BF16), w_out[0].astype(BF16),
                 w_ffn_gate[0].astype(BF16), w_ffn_up[0].astype(BF16), w_ffn_down[0].astype(BF16))
```

```python
import functools
import math

import numpy as np
import jax
import jax.numpy as jnp
from jax import lax
from jax.experimental import pallas as pl
from jax.experimental.pallas import tpu as pltpu

F32 = jnp.float32
BF16 = jnp.bfloat16
HIGHEST = lax.Precision.HIGHEST

EPS = 1e-6
GRID_W = 64
HEAD_DIM = 64
N_HEADS = 4
D_HYENA = 512
D_ATTN = N_HEADS * 2 * HEAD_DIM
ROPE_BASE = 10000.0
FILTER_TARGET = 1e-2
FAST_DECAY_PCT = 0.3
SLOW_DECAY_PCT = 1.5
LAM_INIT = 0.8 - 0.6 * math.exp(0.0)

LANES = 128
MXU_DIM = 256
VMEM_LIMIT_BYTES = 56 * 1024 * 1024

FFT_N2 = 256
FFT_GROUP = 16
HY_CB = 256
FFT_GROUP_UNROLL = 8


def _dot(a, b):
    return jnp.dot(a, b, preferred_element_type=F32)


def _dot_hi(a, b):
    return jnp.dot(a, b, preferred_element_type=F32, precision=HIGHEST)


def _dot_nt(a, b):
    return lax.dot_general(a, b, (((1,), (1,)), ((), ())), preferred_element_type=F32)


def _rms(x, g):
    return x * lax.rsqrt(jnp.mean(x * x, axis=-1, keepdims=True) + EPS) * g


def _const_spec(shape):
    zeros = (0,) * len(shape)
    return pl.BlockSpec(shape, lambda *_: zeros, pipeline_mode=pl.Buffered(1))


def _params(*semantics):
    return pltpu.CompilerParams(dimension_semantics=semantics, vmem_limit_bytes=VMEM_LIMIT_BYTES)


def _ada_body(c_ref, w_ref, b_ref, o_ref):
    c = c_ref[...]
    o_ref[...] = _dot_hi(c * jax.nn.sigmoid(c), w_ref[...]) + b_ref[...]


def _ada(cc, w_ada, b_ada):
    rows, d = cc.shape
    n = w_ada.shape[1]
    tn = 1024
    return pl.pallas_call(
        _ada_body,
        grid=(n // tn,),
        in_specs=[
            pl.BlockSpec((rows, d), lambda j: (0, 0)),
            pl.BlockSpec((d, tn), lambda j: (0, j)),
            pl.BlockSpec((1, tn), lambda j: (0, j)),
        ],
        out_specs=pl.BlockSpec((rows, tn), lambda j: (0, j)),
        out_shape=jax.ShapeDtypeStruct((rows, n), F32),
        compiler_params=_params("arbitrary"),
        name="ada",
    )(cc, w_ada, b_ada)


def _ctx_kv_body(x_ref, g_ref, sh_ref, sc_ref, wk_ref, wq_ref, wv_ref, k_ref, vt_ref, wqt_ref, wvt_ref):
    @pl.when(pl.program_id(0) == 0)
    def _():
        wqt_ref[...] = wq_ref[...].astype(F32).T.astype(BF16)
        wvt_ref[...] = wv_ref[...].astype(F32).T.astype(BF16)

    h = _rms(x_ref[0], g_ref[...]) * (1.0 + sc_ref[...]) + sh_ref[...]
    hb = h.astype(BF16)
    k_ref[0] = _dot(hb, wk_ref[...]).astype(BF16)
    vt_ref[0] = _dot_nt(wvt_ref[...], hb).astype(BF16)


def _ctx_kv(ctx, g, sh, sc, wk, wq, wv):
    b, n_ctx, d = ctx.shape
    n = wk.shape[1]
    bf = lambda shape: jax.ShapeDtypeStruct(shape, BF16)
    whole_t = pl.BlockSpec((n, d), lambda i: (0, 0))
    return pl.pallas_call(
        _ctx_kv_body,
        grid=(b,),
        in_specs=[
            pl.BlockSpec((1, n_ctx, d), lambda i: (i, 0, 0)),
            _const_spec((1, d)), _const_spec((1, d)), _const_spec((1, d)),
            _const_spec((d, n)), _const_spec((d, n)), _const_spec((d, n)),
        ],
        out_specs=[pl.BlockSpec((1, n_ctx, n), lambda i: (i, 0, 0)),
                   pl.BlockSpec((1, n, n_ctx), lambda i: (i, 0, 0)), whole_t, whole_t],
        out_shape=[bf((b, n_ctx, n)), bf((b, n, n_ctx)), bf((n, d)), bf((n, d))],
        compiler_params=_params("arbitrary"),
        name="ctx_kv",
    )(ctx, g, sh, sc, wk, wq, wv)


ROT = HEAD_DIM // 4
PROJ_HALO = 16


def _proj_body(x_ref, xp_ref, xn_ref, g_ref, sh_ref, sc_ref, cos_ref, sin_ref, cost_ref, sint_ref,
               cw_ref, cb_ref, wu_ref, wqt_ref, wk_ref, wvt_ref, wg_ref,
               u_ref, qt_ref, k_ref, vt_ref, gate_ref):
    tm = x_ref.shape[1]
    half = tm // 2
    rows_a, rows_b = slice(0, half), slice(half, 2 * half)

    def modulated(x):
        return (_rms(x, g_ref[...]) * (1.0 + sc_ref[0]) + sh_ref[0]).astype(BF16)

    def normed(rows):
        return modulated(x_ref[0, rows, :])

    def gates_and_keys(rows, hb):
        gate_ref[0, rows, :] = jax.nn.sigmoid(_dot(hb, wg_ref[...])).astype(BF16)
        kv = _dot(hb, wk_ref[...])
        lane = lax.broadcasted_iota(jnp.int32, (kv.shape[0], LANES), 1)
        first = (lane % (2 * ROT)) < ROT
        slabs = []
        for c0 in range(0, kv.shape[1], LANES):
            slab = kv[:, c0:c0 + LANES]
            slabs.append(jnp.where(first, pltpu.roll(slab, LANES - ROT, axis=1), pltpu.roll(slab, ROT, axis=1)))
        k_ref[0, rows, :] = (kv * cos_ref[rows, :]
                             + jnp.concatenate(slabs, axis=1) * sin_ref[rows, :]).astype(BF16)

    i = pl.program_id(0)
    hb_a = normed(rows_a)
    u_a = _dot(jnp.concatenate([modulated(xp_ref[0]), hb_a], axis=0), wu_ref[...])
    hb_b = normed(rows_b)
    gates_and_keys(rows_a, hb_a)
    u_b = _dot(jnp.concatenate([hb_b, modulated(xn_ref[0])], axis=0), wu_ref[...])
    keep_prev = (i > 0).astype(F32)
    keep_next = (i < pl.num_programs(0) - 1).astype(F32)

    def short_conv(cols):
        ext = jnp.concatenate([u_a[:PROJ_HALO, cols] * keep_prev, u_a[PROJ_HALO:, cols], u_b[:half, cols],
                               u_b[half:, cols] * keep_next], axis=0)
        n = ext.shape[0]
        cw = cw_ref[:, cols]
        prev = pltpu.roll(ext, 1, axis=0)[PROJ_HALO:PROJ_HALO + tm]
        nxt = pltpu.roll(ext, n - 1, axis=0)[PROJ_HALO:PROJ_HALO + tm]
        u_ref[0, :, cols] = (prev * cw[0:1] + ext[PROJ_HALO:PROJ_HALO + tm] * cw[1:2] + nxt * cw[2:3]
                             + cb_ref[:, cols]).astype(BF16)

    third = u_a.shape[1] // 3
    short_conv(slice(0, third))
    gates_and_keys(rows_b, hb_b)
    short_conv(slice(third, 2 * third))
    hb = jnp.concatenate([hb_a, hb_b], axis=0)
    qv = _dot_nt(wqt_ref[...], hb)
    partner = jnp.concatenate(
        [qv[r + ROT:r + 2 * ROT] if part == 0 else qv[r:r + ROT]
         for r in range(0, qv.shape[0], 2 * ROT) for part in (0, 1)], axis=0)
    qt_ref[0] = (qv * cost_ref[...] + partner * sint_ref[...]).astype(BF16)
    short_conv(slice(2 * third, 3 * third))
    vt_ref[0] = _dot_nt(wvt_ref[...], hb).astype(BF16)


def _proj(x, g, sh, sc, cos, sin, cost, sint, conv_w, conv_b, wu, wqt, wk, wvt, wg):
    b, l, d = x.shape
    tm = 512
    per_halo = tm // PROJ_HALO
    n_halo = l // PROJ_HALO
    tile = lambda i, bi: (bi, i, 0)
    tile_t = lambda i, bi: (bi, 0, i)
    per_b = lambda i, bi: (bi, 0, 0)
    before = lambda i, bi: (bi, jnp.maximum(i * per_halo - 1, 0), 0)
    after = lambda i, bi: (bi, jnp.minimum((i + 1) * per_halo, n_halo - 1), 0)
    bf = lambda shape: jax.ShapeDtypeStruct(shape, BF16)
    n_u, n_g = wu.shape[1], wg.shape[1]
    return pl.pallas_call(
        _proj_body,
        grid=(l // tm, b),
        in_specs=[
            pl.BlockSpec((1, tm, d), tile),
            pl.BlockSpec((1, PROJ_HALO, d), before), pl.BlockSpec((1, PROJ_HALO, d), after),
            _const_spec((1, d)),
            pl.BlockSpec((1, 1, d), per_b), pl.BlockSpec((1, 1, d), per_b),
            pl.BlockSpec((tm, D_ATTN), lambda i, bi: (i, 0)), pl.BlockSpec((tm, D_ATTN), lambda i, bi: (i, 0)),
            pl.BlockSpec((D_ATTN, tm), lambda i, bi: (0, i)), pl.BlockSpec((D_ATTN, tm), lambda i, bi: (0, i)),
        ] + [_const_spec(w.shape) for w in (conv_w, conv_b, wu, wqt, wk, wvt, wg)],
        out_specs=[pl.BlockSpec((1, tm, n_u), tile), pl.BlockSpec((1, D_ATTN, tm), tile_t),
                   pl.BlockSpec((1, tm, D_ATTN), tile), pl.BlockSpec((1, D_ATTN, tm), tile_t),
                   pl.BlockSpec((1, tm, n_g), tile)],
        out_shape=[bf((b, l, n_u)), bf((b, D_ATTN, l)), bf((b, l, D_ATTN)), bf((b, D_ATTN, l)), bf((b, l, n_g))],
        compiler_params=_params("arbitrary", "arbitrary"),
        name="proj",
    )(x, x, x, g, sh, sc, cos, sin, cost, sint, conv_w, conv_b, wu, wqt, wk, wvt, wg)


def _fft_dims(l):
    n = 2 * l
    n1 = n // FFT_N2
    h1 = n1 // 2
    k1 = h1 + 1
    ng = FFT_N2 // FFT_GROUP
    return n, n1, h1, k1, ng


@functools.lru_cache(maxsize=None)
def _fft_constants(l):
    n, n1, h1, k1n, ng = _fft_dims(l)
    g = FFT_GROUP
    ma = np.zeros((ng, k1n, 2, g, h1, g), np.float64)
    for gi in range(ng):
        for j in range(g):
            n2 = gi * g + j
            nn = FFT_N2 * np.arange(h1)[None, :] + n2
            theta = 2.0 * np.pi * ((np.arange(k1n)[:, None] * nn) % n) / n
            ma[gi, :, 0, j, :, j] = np.cos(theta)
            ma[gi, :, 1, j, :, j] = -np.sin(theta)
    weight = np.full((k1n,), 2.0)
    weight[0] = weight[-1] = 1.0
    ma_fwd = ma.reshape(ng, k1n * 2 * g, h1 * g)
    ma_inv = (ma * (weight / n)[None, :, None, None, None, None]).reshape(ng, k1n * 2 * g, h1 * g)
    ma_inv = np.transpose(ma_inv, (0, 2, 1))
    kpad = -(-ma_inv.shape[2] // LANES) * LANES
    ma_inv = np.pad(ma_inv, ((0, 0), (0, 0), (0, kpad - ma_inv.shape[2])))
    ang = 2.0 * np.pi * ((np.arange(FFT_N2)[:, None] * np.arange(FFT_N2)[None, :]) % FFT_N2) / FFT_N2
    fr, fi = np.cos(ang), -np.sin(ang)
    dft3 = np.stack([fr, fi, fr + fi, fr - fi])
    mb_filt = np.block([[fr, -fi, fr, -fi], [fi, fr, -fi, -fr]])
    return tuple(np.asarray(a, np.float32) for a in (ma_fwd, ma_inv, dft3, mb_filt))


def _fft_stage_a(src_ref, ma_ref, spec_ref, l):
    _, _, h1, k1n, ng = _fft_dims(l)
    g = FFT_GROUP

    def group(gi, carry):
        off = gi * g
        xg = jnp.concatenate(
            [src_ref[pl.ds(pl.multiple_of(FFT_N2 * i + off, g), g), :] for i in range(h1)], axis=0)
        bg = _dot(ma_ref[gi], xg.astype(BF16)).astype(BF16)
        for c in range(2 * k1n):
            spec_ref[pl.ds(pl.multiple_of(c * FFT_N2 + off, g), g), :] = bg[c * g:(c + 1) * g]
        return carry

    lax.fori_loop(0, ng, group, 0, unroll=FFT_GROUP_UNROLL)


def _fft_stage_a_inv(spec_ref, ma_inv_ref, emit, l):
    _, _, h1, k1n, ng = _fft_dims(l)
    g = FFT_GROUP
    kpad = ma_inv_ref.shape[2] - 2 * k1n * g

    def group(gi, carry):
        off = gi * g
        chunks = [spec_ref[pl.ds(pl.multiple_of(c * FFT_N2 + off, g), g), :] for c in range(2 * k1n)]
        if kpad:
            chunks.append(jnp.zeros((kpad, spec_ref.shape[1]), BF16))
        yg = _dot(ma_inv_ref[gi], jnp.concatenate(chunks, axis=0))
        for i in range(h1):
            emit(pl.ds(pl.multiple_of(FFT_N2 * i + off, g), g), yg[i * g:(i + 1) * g])
        return carry

    lax.fori_loop(0, ng, group, 0, unroll=FFT_GROUP_UNROLL)


def _filt_body(z_ref, w1_ref, b1_ref, w2_ref, b2_ref, fq_ref, w3f_ref, b3f_ref, w3b_ref, b3b_ref,
               dl_ref, ma_ref, mb_ref, hf_ref, hdn_ref, sig_ref, specf_ref, specb_ref, *, l):
    _, _, _, k1n, _ = _fft_dims(l)

    @pl.when((pl.program_id(0) == 0) & (pl.program_id(1) == 0))
    def _():
        fq = fq_ref[...]
        h1 = jnp.sin(fq * (_dot_hi(z_ref[...], w1_ref[...]) + b1_ref[...]))
        hdn_ref[...] = jnp.sin(fq * (_dot_hi(h1, w2_ref[...]) + b2_ref[...])).astype(BF16)

    hdn = hdn_ref[...]
    cb = specf_ref.shape[1]
    blk = 2 * FFT_N2
    rows = lax.broadcasted_iota(jnp.int32, (l, cb), 0)
    window = jnp.exp(rows.astype(F32) * (-1.0 / (l - 1)) * dl_ref[...])

    sig_ref[...] = (_dot(hdn, w3f_ref[...].astype(BF16)) + b3f_ref[...]) * window
    _fft_stage_a(sig_ref, ma_ref, specf_ref, l)
    sig_ref[...] = jnp.where(rows > 0, (_dot(hdn, w3b_ref[...].astype(BF16)) + b3b_ref[...]) * window, 0.0)
    _fft_stage_a(sig_ref, ma_ref, specb_ref, l)

    def blocks(starts):
        x = jnp.concatenate(
            [jnp.concatenate([specf_ref[pl.ds(st, blk), :], specb_ref[pl.ds(st, blk), :]], axis=0)
             for st in starts], axis=1)
        s = _dot(mb_ref[...], x).astype(BF16)
        for n, st in enumerate(starts):
            hf_ref[0, pl.ds(st, blk), :] = s[:, n * cb:(n + 1) * cb]

    def per_pair(i, carry):
        st = pl.multiple_of(i * 2 * blk, blk)
        blocks([st, st + blk])
        return carry

    lax.fori_loop(0, k1n // 2, per_pair, 0)
    if k1n % 2:
        blocks([(k1n - 1) * blk])


def _filt(zfeat, w1, b1, w2, b2, fq, w3, b3, deltas, ma_fwd, mb_filt, order, l):
    _, _, _, k1n, _ = _fft_dims(l)
    cb = HY_CB
    nblk = D_HYENA // cb
    srows = k1n * 2 * FFT_N2
    fcol = lambda o, c: (0, (2 * o) * nblk + c)
    bcol = lambda o, c: (0, (2 * o + 1) * nblk + c)
    return pl.pallas_call(
        functools.partial(_filt_body, l=l),
        grid=(order, nblk),
        in_specs=[
            _const_spec(zfeat.shape), _const_spec(w1.shape), _const_spec(b1.shape),
            _const_spec(w2.shape), _const_spec(b2.shape), _const_spec(fq.shape),
            pl.BlockSpec((w3.shape[0], cb), fcol), pl.BlockSpec((1, cb), fcol),
            pl.BlockSpec((w3.shape[0], cb), bcol), pl.BlockSpec((1, cb), bcol),
            pl.BlockSpec((1, cb), lambda o, c: (0, c)),
            _const_spec(ma_fwd.shape), _const_spec(mb_filt.shape),
        ],
        out_specs=pl.BlockSpec((1, srows, cb), lambda o, c: (o, 0, c)),
        out_shape=jax.ShapeDtypeStruct((order, srows, D_HYENA), BF16),
        scratch_shapes=[
            pltpu.VMEM((l, w2.shape[1]), BF16),
            pltpu.VMEM((l, cb), F32),
            pltpu.VMEM((srows, cb), BF16),
            pltpu.VMEM((srows, cb), BF16),
        ],
        compiler_params=_params("arbitrary", "arbitrary"),
        name="filt",
    )(zfeat, w1, b1, w2, b2, fq, w3, b3, w3, b3, deltas, ma_fwd, mb_filt)


HY_ROWS = 512


def _hyena_body(u0_ref, u1_ref, u2_ref, db_ref, hf_ref,
                ma_ref, mai_ref, dft_ref, o_ref, z_ref, y_ref, spec_ref, *, l, order):
    _, _, _, k1n, _ = _fft_dims(l)
    blk = 2 * FFT_N2
    gates = (u1_ref, u2_ref)

    for r0 in range(0, l, HY_ROWS):
        z_ref[r0:r0 + HY_ROWS, :] = u0_ref[0, r0:r0 + HY_ROWS, :].astype(F32)

    for o in range(order):
        _fft_stage_a(z_ref, ma_ref, spec_ref, l)

        cb = spec_ref.shape[1]
        groups = [list(range(i, min(i + 2, k1n))) for i in range(0, k1n, 2)]

        def stage_b(ks):
            br = jnp.concatenate([spec_ref[k * blk:k * blk + FFT_N2, :] for k in ks], axis=1)
            bi = jnp.concatenate([spec_ref[k * blk + FFT_N2:(k + 1) * blk, :] for k in ks], axis=1)
            k1 = _dot(dft_ref[2], br)
            return k1 - _dot(dft_ref[1], br + bi), k1 + _dot(dft_ref[0], bi - br)

        def filter_and_invert(ks, s, o=o):
            sr, si = s
            hr = jnp.concatenate([hf_ref[o, k * blk:k * blk + FFT_N2, :] for k in ks], axis=1).astype(F32)
            hi = jnp.concatenate([hf_ref[o, k * blk + FFT_N2:(k + 1) * blk, :] for k in ks], axis=1).astype(F32)
            yr = sr * hr - si * hi
            yi = sr * hi + si * hr
            k1 = _dot(dft_ref[3], yr.astype(BF16))
            cr = (k1 + _dot(dft_ref[1], (yr + yi).astype(BF16))).astype(BF16)
            ci = (k1 + _dot(dft_ref[0], (yi - yr).astype(BF16))).astype(BF16)
            for n, k in enumerate(ks):
                spec_ref[k * blk:k * blk + FFT_N2, :] = cr[:, n * cb:(n + 1) * cb]
                spec_ref[k * blk + FFT_N2:(k + 1) * blk, :] = ci[:, n * cb:(n + 1) * cb]

        s_next = stage_b(groups[0])
        for gi, ks in enumerate(groups):
            s_cur = s_next
            if gi + 1 < len(groups):
                s_next = stage_b(groups[gi + 1])
            filter_and_invert(ks, s_cur)

        def store_y(rows, y):
            y_ref[rows, :] = y

        _fft_stage_a_inv(spec_ref, mai_ref, store_y, l)

        dbias = db_ref[o:o + 1, :]
        for r0 in range(0, l, HY_ROWS):
            rows = slice(r0, r0 + HY_ROWS)
            znew = gates[o][0, rows, :].astype(F32) * (y_ref[rows, :] + z_ref[rows, :] * dbias)
            if o + 1 < order:
                z_ref[rows, :] = znew
            else:
                o_ref[0, rows, :] = znew.astype(BF16)


def _hyena(u, dbias, hf, consts, order):
    b, l, _ = u.shape
    _, _, _, k1n, _ = _fft_dims(l)
    cb = HY_CB
    nblk = D_HYENA // cb
    srows = k1n * 2 * FFT_N2
    ma_fwd, ma_inv, dft3 = consts
    upart = lambda p: pl.BlockSpec((1, l, cb), lambda c, bi: (bi, 0, p * nblk + c))
    return pl.pallas_call(
        functools.partial(_hyena_body, l=l, order=order),
        grid=(nblk, b),
        in_specs=[
            upart(0), upart(1), upart(2),
            pl.BlockSpec((order, cb), lambda c, bi: (0, c)),
            pl.BlockSpec((order, srows, cb), lambda c, bi: (0, 0, c), pipeline_mode=pl.Buffered(1)),
            _const_spec(ma_fwd.shape), _const_spec(ma_inv.shape),
            _const_spec(dft3.shape),
        ],
        out_specs=pl.BlockSpec((1, l, cb), lambda c, bi: (bi, 0, c)),
        out_shape=jax.ShapeDtypeStruct((b, l, D_HYENA), BF16),
        scratch_shapes=[
            pltpu.VMEM((l, cb), F32),
            pltpu.VMEM((l, cb), F32),
            pltpu.VMEM((srows, cb), BF16),
        ],
        compiler_params=_params("arbitrary", "arbitrary"),
        name="hyena",
    )(u, u, u, dbias, hf, ma_fwd, ma_inv, dft3)


ATT_TQ = MXU_DIM
ATT_KEY_CHUNK = 2 * MXU_DIM
ATT_BLOCKS = 8


def _attn_body(qc_ref, qn_ref, k_ref, kn_ref, kc_ref, kcn_ref, vt_ref, vct_ref, lam_ref, sg_ref, o_ref,
               sa_ref, sb_ref, ma_ref, mb_ref):
    tq = ATT_TQ
    lat = k_ref.shape[1]
    lv = lam_ref[...]
    lam = (jnp.exp(jnp.sum(lv[0:1] * lv[1:2], axis=-1, keepdims=True))
           - jnp.exp(jnp.sum(lv[2:3] * lv[3:4], axis=-1, keepdims=True)) + LAM_INIT)

    n_ctx = kc_ref.shape[1]
    chunks = [(False, r, r, ATT_KEY_CHUNK) for r in range(0, lat, ATT_KEY_CHUNK)] + [(True, 0, lat, n_ctx)]

    def stage(qt, keys_ref, ctx_keys_ref, s_new, m_new, s_old, m_old, hook=None):
        m_run = acc = denom = None
        feat = lax.broadcasted_iota(jnp.int32, qt.shape, 0)
        qz = jnp.concatenate([jnp.where(feat < HEAD_DIM, qt, jnp.zeros_like(qt)),
                              jnp.where(feat >= HEAD_DIM, qt, jnp.zeros_like(qt))], axis=1)
        if s_old is not None:
            m_prev = m_old[...]
        for n, (is_ctx, src, dst, rows) in enumerate(chunks):
            keys = ctx_keys_ref[0, src:src + rows, :] if is_ctx else keys_ref[0, src:src + rows, :]
            s = _dot(keys, qz)
            s_new[dst:dst + rows, :] = s
            cmax = jnp.max(s, axis=0, keepdims=True)
            m_run = cmax if m_run is None else jnp.maximum(m_run, cmax)
            if s_old is not None:
                p = jnp.exp2(s_old[dst:dst + rows, :] - m_prev)
                psum = jnp.sum(p, axis=0, keepdims=True)
                vals = vct_ref[0, :, src:src + rows] if is_ctx else vt_ref[0, :, src:src + rows]
                part = _dot(vals, p.astype(BF16))
                denom = psum if denom is None else denom + psum
                acc = part if acc is None else acc + part
            if n == 0 and hook is not None:
                hook()
        m_new[...] = m_run
        return acc, denom

    def finish(acc, denom):
        r = 1.0 / denom
        ot = acc[:, :tq] * r[:, :tq] - acc[:, tq:] * (lam * r[:, tq:])
        return (_rms(ot.T, sg_ref[...]) * (1.0 - LAM_INIT)).astype(BF16)

    @pl.when(pl.program_id(0) == 0)
    def _():
        stage(qc_ref[0, :, :tq], k_ref, kc_ref, sa_ref, ma_ref, None, None)

    bufs = ((sa_ref, ma_ref), (sb_ref, mb_ref))
    pending = None
    for n in range(ATT_BLOCKS):
        if n + 1 < ATT_BLOCKS:
            new = (qc_ref[0, :, (n + 1) * tq:(n + 2) * tq], k_ref, kc_ref)
        else:
            new = (qn_ref[0, :, :tq], kn_ref, kcn_ref)
        raw = stage(*new, *bufs[(n + 1) % 2], *bufs[n % 2], hook=pending)

        def pending(raw=raw, n=n):
            o_ref[0, n * tq:(n + 1) * tq, :] = finish(*raw)

    pending()


def _attn(qt, k, vt, kc, vct, lamv, subln_g):
    b, l, _ = k.shape
    c = kc.shape[1]
    dv = 2 * HEAD_DIM
    tq = ATT_TQ
    tstep = ATT_BLOCKS * tq
    assert l % tstep == 0 and ATT_BLOCKS % 2 == 0
    pairs = l // tstep
    steps = b * N_HEADS * pairs

    def split(j):
        return j // (N_HEADS * pairs), (j // pairs) % N_HEADS, j % pairs

    def cur(fn):
        return lambda j: fn(*split(j))

    def nxt(fn):
        return lambda j: fn(*split(jnp.minimum(j + 1, steps - 1)))

    q_map = lambda bi, h, i: (bi, h, i)
    k_map = lambda bi, h, i: (bi, 0, h)
    v_map = lambda bi, h, i: (bi, h, 0)
    return pl.pallas_call(
        _attn_body,
        grid=(steps,),
        in_specs=[
            pl.BlockSpec((1, dv, tstep), cur(q_map)), pl.BlockSpec((1, dv, tstep), nxt(q_map)),
            pl.BlockSpec((1, l, dv), cur(k_map)), pl.BlockSpec((1, l, dv), nxt(k_map)),
            pl.BlockSpec((1, c, dv), cur(k_map)), pl.BlockSpec((1, c, dv), nxt(k_map)),
            pl.BlockSpec((1, dv, l), cur(v_map)), pl.BlockSpec((1, dv, c), cur(v_map)),
            _const_spec(lamv.shape), _const_spec(subln_g.shape),
        ],
        out_specs=pl.BlockSpec((1, tstep, dv), cur(lambda bi, h, i: (bi, i, h))),
        out_shape=jax.ShapeDtypeStruct((b, l, N_HEADS * dv), BF16),
        scratch_shapes=[pltpu.VMEM((l + c, 2 * tq), F32), pltpu.VMEM((l + c, 2 * tq), F32),
                        pltpu.VMEM((1, 2 * tq), F32), pltpu.VMEM((1, 2 * tq), F32)],
        compiler_params=_params("arbitrary"),
        name="attn",
    )(qt, qt, k, k, kc, kc, vt, vct, lamv, subln_g)


def _post_body(x_ref, zh_ref, oa_ref, gate_ref, g1_ref, sh2_ref, sc2_ref, g2_ref,
               gpost_ref, gfpre_ref, gfpost_ref,
               whu_ref, wau_ref, wo_ref, wg_ref, wu_ref, wd_ref, o_ref, *, ff_chunks):
    d = x_ref.shape[2]
    half = x_ref.shape[1] // 2
    rows_a, rows_b = slice(0, half), slice(half, 2 * half)

    def lifts(rows):
        return _dot(zh_ref[0, rows, :], whu_ref[...]), _dot(oa_ref[0, rows, :], wau_ref[...])

    def out_proj(rows, y_hy, y_att):
        mixed = gate_ref[0, rows, :d].astype(F32) * y_hy + gate_ref[0, rows, d:].astype(F32) * y_att
        return _dot(mixed.astype(BF16), wo_ref[...])

    def residual_and_norm(rows, mixed):
        x1 = x_ref[0, rows, :] + g1_ref[0] * _rms(mixed, gpost_ref[...])
        return x1, (_rms(x1, gfpre_ref[...]) * (1.0 + sc2_ref[0]) + sh2_ref[0]).astype(BF16)

    def ffn_chunk(hf, f, chunk):
        c0, cw = chunk
        a = _dot(hf, wg_ref[:, c0:c0 + cw])
        t = (a * jax.nn.sigmoid(a) * _dot(hf, wu_ref[:, c0:c0 + cw])).astype(BF16)
        part = _dot(t, wd_ref[c0:c0 + cw, :])
        return part if f is None else f + part

    def finish(rows, x1, f):
        o_ref[0, rows, :] = x1 + g2_ref[0] * _rms(f, gfpost_ref[...])

    mixed_a = out_proj(rows_a, *lifts(rows_a))
    lifted_b = lifts(rows_b)
    x1_a, hf_a = residual_and_norm(rows_a, mixed_a)
    mixed_b = out_proj(rows_b, *lifted_b)
    f_a = ffn_chunk(hf_a, None, ff_chunks[0])
    x1_b, hf_b = residual_and_norm(rows_b, mixed_b)
    for chunk in ff_chunks[1:]:
        f_a = ffn_chunk(hf_a, f_a, chunk)
    f_b = ffn_chunk(hf_b, None, ff_chunks[0])
    finish(rows_a, x1_a, f_a)
    for chunk in ff_chunks[1:]:
        f_b = ffn_chunk(hf_b, f_b, chunk)
    finish(rows_b, x1_b, f_b)


def _post(x, zh, oa, gate, g1, sh2, sc2, g2, gpost, gfpre, gfpost, whu, wau, wo, wg, wu, wd):
    b, l, d = x.shape
    dff = wg.shape[1]
    tm = 512
    ff_chunk = 4 * MXU_DIM
    ff_chunks = tuple((c0, min(ff_chunk, dff - c0)) for c0 in range(0, dff, ff_chunk))
    tile = lambda bi, i: (bi, i, 0)
    per_b = lambda bi, i: (bi, 0, 0)
    return pl.pallas_call(
        functools.partial(_post_body, ff_chunks=ff_chunks),
        grid=(b, l // tm),
        in_specs=[
            pl.BlockSpec((1, tm, d), tile),
            pl.BlockSpec((1, tm, zh.shape[2]), tile),
            pl.BlockSpec((1, tm, oa.shape[2]), tile),
            pl.BlockSpec((1, tm, gate.shape[2]), tile),
            pl.BlockSpec((1, 1, d), per_b), pl.BlockSpec((1, 1, d), per_b),
            pl.BlockSpec((1, 1, d), per_b), pl.BlockSpec((1, 1, d), per_b),
            _const_spec((1, d)), _const_spec((1, d)), _const_spec((1, d)),
        ] + [_const_spec(w.shape) for w in (whu, wau, wo, wg, wu, wd)],
        out_specs=pl.BlockSpec((1, tm, d), tile),
        out_shape=jax.ShapeDtypeStruct((b, l, d), F32),
        compiler_params=_params("arbitrary", "arbitrary"),
        name="post",
    )(x, zh, oa, gate, g1, sh2, sc2, g2, gpost, gfpre, gfpost, whu, wau, wo, wg, wu, wd)


@functools.lru_cache(maxsize=None)
def _rope_tables(l):
    quarter = HEAD_DIM // 4
    inv = ROPE_BASE ** (-np.arange(quarter, dtype=np.float64) / quarter)
    t = np.arange(l)
    ang_r = (t // GRID_W)[:, None] * inv[None, :]
    ang_c = (t % GRID_W)[:, None] * inv[None, :]
    ang = np.concatenate([ang_r, ang_r, ang_c, ang_c], axis=1)
    reps = D_ATTN // HEAD_DIM
    cos = np.tile(np.cos(ang), (1, reps)).astype(np.float32)
    second = (np.arange(D_ATTN) % (2 * quarter)) >= quarter
    sin = (np.tile(np.sin(ang), (1, reps)) * np.where(second, 1.0, -1.0)[None, :]).astype(np.float32)
    return cos, sin


@functools.lru_cache(maxsize=None)
def _filter_features(l, emb, kpad):
    bands = (emb - 1) // 2
    t = np.linspace(0.0, 1.0, l, dtype=np.float32).astype(np.float64)[:, None]
    w = (2.0 * math.pi / l) * np.arange(l, dtype=np.float64)[:, None]
    fr = np.linspace(1e-4, bands - 1, bands, dtype=np.float32).astype(np.float64)[None, :]
    z = np.concatenate([t, np.cos(fr * w), -np.sin(fr * w)], axis=-1)
    z = np.pad(z, ((0, 0), (0, kpad - emb))).astype(np.float32)
    deltas = np.abs(np.linspace(math.log(FILTER_TARGET) / SLOW_DECAY_PCT,
                                math.log(FILTER_TARGET) / FAST_DECAY_PCT, D_HYENA, dtype=np.float32))
    return z, deltas[None, :].astype(np.float32)


def _pad_to(a, rows, cols, value=0.0):
    return jnp.pad(a, ((0, rows - a.shape[0]), (0, cols - a.shape[1])), constant_values=value)


def kernel(x, c, ctx, c_ctx, w_ada, b_ada, g_mix_pre, g_mix_post, g_ffn_pre, g_ffn_post, w_in, hy_conv_w, hy_conv_b, hy_f_w1, hy_f_b1, hy_f_w2, hy_f_b2, hy_f_w3, hy_f_b3, hy_f_freq, hy_bias, lambda_q1, lambda_k1, lambda_q2, lambda_k2, att_subln_g, w_hy_up, w_att_up, w_out, w_ffn_gate, w_ffn_up, w_ffn_down):
    b, l, d = x.shape
    n_ctx = ctx.shape[1]
    order = hy_bias.shape[1]
    off_q = (order + 1) * D_HYENA
    off_k = off_q + D_ATTN
    off_v = off_k + D_ATTN
    off_g = off_v + D_ATTN
    assert w_in.shape[0] == 1, "single-layer block"
    assert l % HY_ROWS == 0 and l % 512 == 0

    pad_rows = -(-(b + 1) // 8) * 8
    cc = jnp.concatenate([c, c_ctx[None, :], jnp.zeros((pad_rows - b - 1, d), F32)], axis=0)
    ada = _ada(cc, w_ada[0], b_ada[0][None, :])
    sh1, sc1, g1, sh2, sc2, g2 = (ada[:b, i * d:(i + 1) * d][:, None, :] for i in range(6))
    csh1, csc1 = ada[b:b + 1, 0:d], ada[b:b + 1, d:2 * d]

    row = lambda v: v[0][None, :]
    w = w_in[0]
    cos, sin = _rope_tables(l)
    q_scale = HEAD_DIM ** -0.5 * math.log2(math.e)
    wq = (w[:, off_q:off_k] * q_scale).astype(BF16)
    wk = w[:, off_k:off_v].astype(BF16)
    wv = w[:, off_v:off_g].astype(BF16)

    kc, vct, wqt, wvt = _ctx_kv(ctx, row(g_mix_pre), csh1, csc1, wk, wq, wv)

    u, qt, k, vt, gate = _proj(x, row(g_mix_pre), sh1, sc1, jnp.asarray(cos), jnp.asarray(sin),
                               jnp.asarray(cos.T), jnp.asarray(sin.T), hy_conv_w[0], row(hy_conv_b),
                               w[:, :off_q].astype(BF16), wqt, wk, wvt, w[:, off_g:].astype(BF16))

    consts = tuple(jnp.asarray(a).astype(BF16) for a in _fft_constants(l))
    zfeat, deltas = _filter_features(l, hy_f_w1.shape[1], LANES)
    hf = _filt(jnp.asarray(zfeat),
               _pad_to(hy_f_w1[0], LANES, LANES), _pad_to(row(hy_f_b1), 1, LANES),
               _pad_to(hy_f_w2[0], LANES, LANES), _pad_to(row(hy_f_b2), 1, LANES),
               _pad_to(row(hy_f_freq), 1, LANES, 1.0),
               _pad_to(hy_f_w3[0], LANES, hy_f_w3.shape[2]), row(hy_f_b3),
               jnp.asarray(deltas), consts[0], consts[3], order, l)

    zh = _hyena(u, hy_bias[0], hf, consts[:3], order)

    lamv = jnp.concatenate([lambda_q1, lambda_k1, lambda_q2, lambda_k2], axis=0)
    oa = _attn(qt, k, vt, kc, vct, lamv, row(att_subln_g))

    return _post(x, zh, oa, gate, g1, sh2, sc2, g2, row(g_mix_post), row(g_ffn_pre), row(g_ffn_post),
                 w_hy_up[0].astype(BF16), w_att_up[0].astype(BF16), w_out[0].astype(BF16),
                 w_ffn_gate[0].astype(BF16), w_ffn_up[0].astype(BF16), w_ffn_down[0].astype(BF16))
```

```python
import functools
import math

import numpy as np
import jax
import jax.numpy as jnp
from jax import lax
from jax.experimental import pallas as pl
from jax.experimental.pallas import tpu as pltpu

F32 = jnp.float32
BF16 = jnp.bfloat16

EPS = 1e-6
GRID_W = 64
HEAD_DIM = 64
N_HEADS = 4
D_HYENA = 512
D_ATTN = N_HEADS * 2 * HEAD_DIM
ROPE_BASE = 10000.0
FILTER_TARGET = 1e-2
FAST_DECAY_PCT = 0.3
SLOW_DECAY_PCT = 1.5
LAM_INIT = 0.8 - 0.6 * math.exp(0.0)

LANES = 128
MXU_DIM = 256
VMEM_LIMIT_BYTES = 56 * 1024 * 1024

FFT_N2 = 256
FFT_GROUP = 16
HY_CB = 256
FFT_GROUP_UNROLL = 8


def _dot(a, b):
    return jnp.dot(a, b, preferred_element_type=F32)


def _dot_x3(a, b):
    a_hi = a.astype(BF16)
    a_lo = (a - a_hi.astype(F32)).astype(BF16)
    b_hi = b.astype(BF16)
    b_lo = (b - b_hi.astype(F32)).astype(BF16)
    return _dot(a_hi, b_hi) + _dot(a_lo, b_hi) + _dot(a_hi, b_lo)


def _dot_nt(a, b):
    return lax.dot_general(a, b, (((1,), (1,)), ((), ())), preferred_element_type=F32)


def _rms(x, g):
    return x * lax.rsqrt(jnp.mean(x * x, axis=-1, keepdims=True) + EPS) * g


def _const_spec(shape):
    zeros = (0,) * len(shape)
    return pl.BlockSpec(shape, lambda *_: zeros, pipeline_mode=pl.Buffered(1))


def _params(*semantics):
    return pltpu.CompilerParams(dimension_semantics=semantics, vmem_limit_bytes=VMEM_LIMIT_BYTES)


def _ada_body(c_ref, w_ref, b_ref, o_ref):
    c = c_ref[...]
    s = c * jax.nn.sigmoid(c)
    w = w_ref[...]
    s_hi = s.astype(BF16)
    s_lo = (s - s_hi.astype(F32)).astype(BF16)
    w_hi = w.astype(BF16)
    w_lo = (w - w_hi.astype(F32)).astype(BF16)
    rows = s.shape[0]
    both = _dot(jnp.concatenate([s_hi, s_lo], axis=0), w_hi)
    o_ref[...] = both[:rows] + both[rows:] + _dot(s_hi, w_lo) + b_ref[...]


def _ada(cc, w_ada, b_ada):
    rows, d = cc.shape
    n = w_ada.shape[1]
    tn = 1024
    return pl.pallas_call(
        _ada_body,
        grid=(n // tn,),
        in_specs=[
            pl.BlockSpec((rows, d), lambda j: (0, 0)),
            pl.BlockSpec((d, tn), lambda j: (0, j)),
            pl.BlockSpec((1, tn), lambda j: (0, j)),
        ],
        out_specs=pl.BlockSpec((rows, tn), lambda j: (0, j)),
        out_shape=jax.ShapeDtypeStruct((rows, n), F32),
        compiler_params=_params("arbitrary"),
        name="ada",
    )(cc, w_ada, b_ada)


def _ctx_kv_body(x_ref, g_ref, sh_ref, sc_ref, wk_ref, wq_ref, wv_ref, k_ref, vt_ref, wqt_ref, wvt_ref):
    @pl.when(pl.program_id(0) == 0)
    def _():
        wqt_ref[...] = wq_ref[...].astype(F32).T.astype(BF16)
        wvt_ref[...] = wv_ref[...].astype(F32).T.astype(BF16)

    h = _rms(x_ref[0], g_ref[...]) * (1.0 + sc_ref[...]) + sh_ref[...]
    hb = h.astype(BF16)
    k_ref[0] = _dot(hb, wk_ref[...]).astype(BF16)
    vt_ref[0] = _dot_nt(wvt_ref[...], hb).astype(BF16)


def _ctx_kv(ctx, g, sh, sc, wk, wq, wv):
    b, n_ctx, d = ctx.shape
    n = wk.shape[1]
    bf = lambda shape: jax.ShapeDtypeStruct(shape, BF16)
    whole_t = pl.BlockSpec((n, d), lambda i: (0, 0))
    return pl.pallas_call(
        _ctx_kv_body,
        grid=(b,),
        in_specs=[
            pl.BlockSpec((1, n_ctx, d), lambda i: (i, 0, 0)),
            _const_spec((1, d)), _const_spec((1, d)), _const_spec((1, d)),
            _const_spec((d, n)), _const_spec((d, n)), _const_spec((d, n)),
        ],
        out_specs=[pl.BlockSpec((1, n_ctx, n), lambda i: (i, 0, 0)),
                   pl.BlockSpec((1, n, n_ctx), lambda i: (i, 0, 0)), whole_t, whole_t],
        out_shape=[bf((b, n_ctx, n)), bf((b, n, n_ctx)), bf((n, d)), bf((n, d))],
        compiler_params=_params("arbitrary"),
        name="ctx_kv",
    )(ctx, g, sh, sc, wk, wq, wv)


ROT = HEAD_DIM // 4
PROJ_HALO = 16


def _proj_body(x_ref, xp_ref, xn_ref, g_ref, sh_ref, sc_ref, cos_ref, sin_ref, cost_ref, sint_ref,
               cw_ref, cb_ref, wu_ref, wqt_ref, wk_ref, wvt_ref, wg_ref,
               u_ref, qt_ref, k_ref, vt_ref, gate_ref):
    tm = x_ref.shape[1]
    half = tm // 2
    rows_a, rows_b = slice(0, half), slice(half, 2 * half)

    def modulated(x):
        return (_rms(x, g_ref[...]) * (1.0 + sc_ref[0]) + sh_ref[0]).astype(BF16)

    def normed(rows):
        return modulated(x_ref[0, rows, :])

    def gates_and_keys(rows, hb):
        gate_ref[0, rows, :] = jax.nn.sigmoid(_dot(hb, wg_ref[...])).astype(BF16)
        kv = _dot(hb, wk_ref[...])
        lane = lax.broadcasted_iota(jnp.int32, (kv.shape[0], LANES), 1)
        first = (lane % (2 * ROT)) < ROT
        slabs = []
        for c0 in range(0, kv.shape[1], LANES):
            slab = kv[:, c0:c0 + LANES]
            slabs.append(jnp.where(first, pltpu.roll(slab, LANES - ROT, axis=1), pltpu.roll(slab, ROT, axis=1)))
        k_ref[0, rows, :] = (kv * cos_ref[rows, :]
                             + jnp.concatenate(slabs, axis=1) * sin_ref[rows, :]).astype(BF16)

    i = pl.program_id(0)
    hb_a = normed(rows_a)
    u_a = _dot(jnp.concatenate([modulated(xp_ref[0]), hb_a], axis=0), wu_ref[...])
    hb_b = normed(rows_b)
    gates_and_keys(rows_a, hb_a)
    u_b = _dot(jnp.concatenate([hb_b, modulated(xn_ref[0])], axis=0), wu_ref[...])
    keep_prev = (i > 0).astype(F32)
    keep_next = (i < pl.num_programs(0) - 1).astype(F32)

    def short_conv(cols):
        ext = jnp.concatenate([u_a[:PROJ_HALO, cols] * keep_prev, u_a[PROJ_HALO:, cols], u_b[:half, cols],
                               u_b[half:, cols] * keep_next], axis=0)
        n = ext.shape[0]
        cw = cw_ref[:, cols]
        prev = pltpu.roll(ext, 1, axis=0)[PROJ_HALO:PROJ_HALO + tm]
        nxt = pltpu.roll(ext, n - 1, axis=0)[PROJ_HALO:PROJ_HALO + tm]
        u_ref[0, :, cols] = (prev * cw[0:1] + ext[PROJ_HALO:PROJ_HALO + tm] * cw[1:2] + nxt * cw[2:3]
                             + cb_ref[:, cols]).astype(BF16)

    third = u_a.shape[1] // 3
    short_conv(slice(0, third))
    gates_and_keys(rows_b, hb_b)
    short_conv(slice(third, 2 * third))
    hb = jnp.concatenate([hb_a, hb_b], axis=0)
    qv = _dot_nt(wqt_ref[...], hb)
    partner = jnp.concatenate(
        [qv[r + ROT:r + 2 * ROT] if part == 0 else qv[r:r + ROT]
         for r in range(0, qv.shape[0], 2 * ROT) for part in (0, 1)], axis=0)
    qt_ref[0] = (qv * cost_ref[...] + partner * sint_ref[...]).astype(BF16)
    short_conv(slice(2 * third, 3 * third))
    vt_ref[0] = _dot_nt(wvt_ref[...], hb).astype(BF16)


def _proj(x, g, sh, sc, cos, sin, cost, sint, conv_w, conv_b, wu, wqt, wk, wvt, wg):
    b, l, d = x.shape
    tm = 512
    per_halo = tm // PROJ_HALO
    n_halo = l // PROJ_HALO
    tile = lambda i, bi: (bi, i, 0)
    tile_t = lambda i, bi: (bi, 0, i)
    per_b = lambda i, bi: (bi, 0, 0)
    before = lambda i, bi: (bi, jnp.maximum(i * per_halo - 1, 0), 0)
    after = lambda i, bi: (bi, jnp.minimum((i + 1) * per_halo, n_halo - 1), 0)
    bf = lambda shape: jax.ShapeDtypeStruct(shape, BF16)
    n_u, n_g = wu.shape[1], wg.shape[1]
    return pl.pallas_call(
        _proj_body,
        grid=(l // tm, b),
        in_specs=[
            pl.BlockSpec((1, tm, d), tile),
            pl.BlockSpec((1, PROJ_HALO, d), before), pl.BlockSpec((1, PROJ_HALO, d), after),
            _const_spec((1, d)),
            pl.BlockSpec((1, 1, d), per_b), pl.BlockSpec((1, 1, d), per_b),
            pl.BlockSpec((tm, D_ATTN), lambda i, bi: (i, 0)), pl.BlockSpec((tm, D_ATTN), lambda i, bi: (i, 0)),
            pl.BlockSpec((D_ATTN, tm), lambda i, bi: (0, i)), pl.BlockSpec((D_ATTN, tm), lambda i, bi: (0, i)),
        ] + [_const_spec(w.shape) for w in (conv_w, conv_b, wu, wqt, wk, wvt, wg)],
        out_specs=[pl.BlockSpec((1, tm, n_u), tile), pl.BlockSpec((1, D_ATTN, tm), tile_t),
                   pl.BlockSpec((1, tm, D_ATTN), tile), pl.BlockSpec((1, D_ATTN, tm), tile_t),
                   pl.BlockSpec((1, tm, n_g), tile)],
        out_shape=[bf((b, l, n_u)), bf((b, D_ATTN, l)), bf((b, l, D_ATTN)), bf((b, D_ATTN, l)), bf((b, l, n_g))],
        compiler_params=_params("arbitrary", "arbitrary"),
        name="proj",
    )(x, x, x, g, sh, sc, cos, sin, cost, sint, conv_w, conv_b, wu, wqt, wk, wvt, wg)


def _fft_dims(l):
    n = 2 * l
    n1 = n // FFT_N2
    h1 = n1 // 2
    k1 = h1 + 1
    ng = FFT_N2 // FFT_GROUP
    return n, n1, h1, k1, ng


@functools.lru_cache(maxsize=None)
def _fft_constants(l):
    n, n1, h1, k1n, ng = _fft_dims(l)
    g = FFT_GROUP
    ma = np.zeros((ng, k1n, 2, g, h1, g), np.float64)
    for gi in range(ng):
        for j in range(g):
            n2 = gi * g + j
            nn = FFT_N2 * np.arange(h1)[None, :] + n2
            theta = 2.0 * np.pi * ((np.arange(k1n)[:, None] * nn) % n) / n
            ma[gi, :, 0, j, :, j] = np.cos(theta)
            ma[gi, :, 1, j, :, j] = -np.sin(theta)
    weight = np.full((k1n,), 2.0)
    weight[0] = weight[-1] = 1.0
    ma_fwd = ma.reshape(ng, k1n * 2 * g, h1 * g)
    ma_inv = (ma * (weight / n)[None, :, None, None, None, None]).reshape(ng, k1n * 2 * g, h1 * g)
    ma_inv = np.transpose(ma_inv, (0, 2, 1))
    kpad = -(-ma_inv.shape[2] // LANES) * LANES
    ma_inv = np.pad(ma_inv, ((0, 0), (0, 0), (0, kpad - ma_inv.shape[2])))
    ang = 2.0 * np.pi * ((np.arange(FFT_N2)[:, None] * np.arange(FFT_N2)[None, :]) % FFT_N2) / FFT_N2
    fr, fi = np.cos(ang), -np.sin(ang)
    dft3 = np.stack([fr, fi, fr + fi, fr - fi])
    mb_filt = np.block([[fr, -fi, fr, -fi], [fi, fr, -fi, -fr]])
    return tuple(np.asarray(a, np.float32) for a in (ma_fwd, ma_inv, dft3, mb_filt))


def _fft_stage_a(src_ref, ma_ref, spec_ref, l):
    _, _, h1, k1n, ng = _fft_dims(l)
    g = FFT_GROUP

    def group(gi, carry):
        off = gi * g
        xg = jnp.concatenate(
            [src_ref[pl.ds(pl.multiple_of(FFT_N2 * i + off, g), g), :] for i in range(h1)], axis=0)
        bg = _dot(ma_ref[gi], xg.astype(BF16)).astype(BF16)
        for c in range(2 * k1n):
            spec_ref[pl.ds(pl.multiple_of(c * FFT_N2 + off, g), g), :] = bg[c * g:(c + 1) * g]
        return carry

    lax.fori_loop(0, ng, group, 0, unroll=FFT_GROUP_UNROLL)


def _fft_stage_a_inv(spec_ref, ma_inv_ref, emit, l):
    _, _, h1, k1n, ng = _fft_dims(l)
    g = FFT_GROUP
    kpad = ma_inv_ref.shape[2] - 2 * k1n * g

    def group(gi, carry):
        off = gi * g
        chunks = [spec_ref[pl.ds(pl.multiple_of(c * FFT_N2 + off, g), g), :] for c in range(2 * k1n)]
        if kpad:
            chunks.append(jnp.zeros((kpad, spec_ref.shape[1]), BF16))
        yg = _dot(ma_inv_ref[gi], jnp.concatenate(chunks, axis=0))
        for i in range(h1):
            emit(pl.ds(pl.multiple_of(FFT_N2 * i + off, g), g), yg[i * g:(i + 1) * g])
        return carry

    lax.fori_loop(0, ng, group, 0, unroll=FFT_GROUP_UNROLL)


def _filt_body(z_ref, w1_ref, b1_ref, w2_ref, b2_ref, fq_ref, w3f_ref, b3f_ref, w3b_ref, b3b_ref,
               dl_ref, ma_ref, mb_ref, hf_ref, hdn_ref, sig_ref, specf_ref, specb_ref, *, l):
    _, _, _, k1n, _ = _fft_dims(l)

    @pl.when((pl.program_id(0) == 0) & (pl.program_id(1) == 0))
    def _():
        fq = fq_ref[...]
        h1 = jnp.sin(fq * (_dot_x3(z_ref[...], w1_ref[...]) + b1_ref[...]))
        hdn_ref[...] = jnp.sin(fq * (_dot_x3(h1, w2_ref[...]) + b2_ref[...])).astype(BF16)

    hdn = hdn_ref[...]
    cb = specf_ref.shape[1]
    blk = 2 * FFT_N2
    rows = lax.broadcasted_iota(jnp.int32, (l, cb), 0)
    window = jnp.exp(rows.astype(F32) * (-1.0 / (l - 1)) * dl_ref[...])

    sig_ref[...] = (_dot(hdn, w3f_ref[...].astype(BF16)) + b3f_ref[...]) * window
    _fft_stage_a(sig_ref, ma_ref, specf_ref, l)
    sig_ref[...] = jnp.where(rows > 0, (_dot(hdn, w3b_ref[...].astype(BF16)) + b3b_ref[...]) * window, 0.0)
    _fft_stage_a(sig_ref, ma_ref, specb_ref, l)

    def blocks(starts):
        x = jnp.concatenate(
            [jnp.concatenate([specf_ref[pl.ds(st, blk), :], specb_ref[pl.ds(st, blk), :]], axis=0)
             for st in starts], axis=1)
        s = _dot(mb_ref[...], x).astype(BF16)
        for n, st in enumerate(starts):
            hf_ref[0, pl.ds(st, blk), :] = s[:, n * cb:(n + 1) * cb]

    def per_pair(i, carry):
        st = pl.multiple_of(i * 2 * blk, blk)
        blocks([st, st + blk])
        return carry

    lax.fori_loop(0, k1n // 2, per_pair, 0)
    if k1n % 2:
        blocks([(k1n - 1) * blk])


def _filt(zfeat, w1, b1, w2, b2, fq, w3, b3, deltas, ma_fwd, mb_filt, order, l):
    _, _, _, k1n, _ = _fft_dims(l)
    cb = HY_CB
    nblk = D_HYENA // cb
    srows = k1n * 2 * FFT_N2
    fcol = lambda o, c: (0, (2 * o) * nblk + c)
    bcol = lambda o, c: (0, (2 * o + 1) * nblk + c)
    return pl.pallas_call(
        functools.partial(_filt_body, l=l),
        grid=(order, nblk),
        in_specs=[
            _const_spec(zfeat.shape), _const_spec(w1.shape), _const_spec(b1.shape),
            _const_spec(w2.shape), _const_spec(b2.shape), _const_spec(fq.shape),
            pl.BlockSpec((w3.shape[0], cb), fcol), pl.BlockSpec((1, cb), fcol),
            pl.BlockSpec((w3.shape[0], cb), bcol), pl.BlockSpec((1, cb), bcol),
            pl.BlockSpec((1, cb), lambda o, c: (0, c)),
            _const_spec(ma_fwd.shape), _const_spec(mb_filt.shape),
        ],
        out_specs=pl.BlockSpec((1, srows, cb), lambda o, c: (o, 0, c)),
        out_shape=jax.ShapeDtypeStruct((order, srows, D_HYENA), BF16),
        scratch_shapes=[
            pltpu.VMEM((l, w2.shape[1]), BF16),
            pltpu.VMEM((l, cb), F32),
            pltpu.VMEM((srows, cb), BF16),
            pltpu.VMEM((srows, cb), BF16),
        ],
        compiler_params=_params("arbitrary", "arbitrary"),
        name="filt",
    )(zfeat, w1, b1, w2, b2, fq, w3, b3, w3, b3, deltas, ma_fwd, mb_filt)


HY_ROWS = 512


def _hyena_body(u0_ref, u1_ref, u2_ref, db_ref, hf_ref,
                ma_ref, mai_ref, dft_ref, o_ref, z_ref, y_ref, spec_ref, *, l, order):
    _, _, _, k1n, _ = _fft_dims(l)
    blk = 2 * FFT_N2
    gates = (u1_ref, u2_ref)

    for r0 in range(0, l, HY_ROWS):
        z_ref[r0:r0 + HY_ROWS, :] = u0_ref[0, r0:r0 + HY_ROWS, :].astype(F32)

    for o in range(order):
        _fft_stage_a(z_ref, ma_ref, spec_ref, l)

        cb = spec_ref.shape[1]
        groups = [list(range(i, min(i + 2, k1n))) for i in range(0, k1n, 2)]

        def stage_b(ks):
            br = jnp.concatenate([spec_ref[k * blk:k * blk + FFT_N2, :] for k in ks], axis=1)
            bi = jnp.concatenate([spec_ref[k * blk + FFT_N2:(k + 1) * blk, :] for k in ks], axis=1)
            k1 = _dot(dft_ref[2], br)
            return k1 - _dot(dft_ref[1], br + bi), k1 + _dot(dft_ref[0], bi - br)

        def filter_and_invert(ks, s, o=o):
            sr, si = s
            hr = jnp.concatenate([hf_ref[o, k * blk:k * blk + FFT_N2, :] for k in ks], axis=1).astype(F32)
            hi = jnp.concatenate([hf_ref[o, k * blk + FFT_N2:(k + 1) * blk, :] for k in ks], axis=1).astype(F32)
            yr = sr * hr - si * hi
            yi = sr * hi + si * hr
            k1 = _dot(dft_ref[3], yr.astype(BF16))
            cr = (k1 + _dot(dft_ref[1], (yr + yi).astype(BF16))).astype(BF16)
            ci = (k1 + _dot(dft_ref[0], (yi - yr).astype(BF16))).astype(BF16)
            for n, k in enumerate(ks):
                spec_ref[k * blk:k * blk + FFT_N2, :] = cr[:, n * cb:(n + 1) * cb]
                spec_ref[k * blk + FFT_N2:(k + 1) * blk, :] = ci[:, n * cb:(n + 1) * cb]

        s_next = stage_b(groups[0])
        for gi, ks in enumerate(groups):
            s_cur = s_next
            if gi + 1 < len(groups):
                s_next = stage_b(groups[gi + 1])
            filter_and_invert(ks, s_cur)

        def store_y(rows, y):
            y_ref[rows, :] = y

        _fft_stage_a_inv(spec_ref, mai_ref, store_y, l)

        dbias = db_ref[o:o + 1, :]
        for r0 in range(0, l, HY_ROWS):
            rows = slice(r0, r0 + HY_ROWS)
            znew = gates[o][0, rows, :].astype(F32) * (y_ref[rows, :] + z_ref[rows, :] * dbias)
            if o + 1 < order:
                z_ref[rows, :] = znew
            else:
                o_ref[0, rows, :] = znew.astype(BF16)


def _hyena(u, dbias, hf, consts, order):
    b, l, _ = u.shape
    _, _, _, k1n, _ = _fft_dims(l)
    cb = HY_CB
    nblk = D_HYENA // cb
    srows = k1n * 2 * FFT_N2
    ma_fwd, ma_inv, dft3 = consts
    upart = lambda p: pl.BlockSpec((1, l, cb), lambda c, bi: (bi, 0, p * nblk + c))
    return pl.pallas_call(
        functools.partial(_hyena_body, l=l, order=order),
        grid=(nblk, b),
        in_specs=[
            upart(0), upart(1), upart(2),
            pl.BlockSpec((order, cb), lambda c, bi: (0, c)),
            pl.BlockSpec((order, srows, cb), lambda c, bi: (0, 0, c), pipeline_mode=pl.Buffered(1)),
            _const_spec(ma_fwd.shape), _const_spec(ma_inv.shape),
            _const_spec(dft3.shape),
        ],
        out_specs=pl.BlockSpec((1, l, cb), lambda c, bi: (bi, 0, c)),
        out_shape=jax.ShapeDtypeStruct((b, l, D_HYENA), BF16),
        scratch_shapes=[
            pltpu.VMEM((l, cb), F32),
            pltpu.VMEM((l, cb), F32),
            pltpu.VMEM((srows, cb), BF16),
        ],
        compiler_params=_params("arbitrary", "arbitrary"),
        name="hyena",
    )(u, u, u, dbias, hf, ma_fwd, ma_inv, dft3)


ATT_TQ = MXU_DIM
ATT_KEY_CHUNK = 2 * MXU_DIM
ATT_BLOCKS = 8


def _attn_body(qc_ref, qn_ref, k_ref, kn_ref, kc_ref, kcn_ref, vt_ref, vct_ref, lam_ref, sg_ref, o_ref,
               sa_ref, sb_ref, ma_ref, mb_ref):
    tq = ATT_TQ
    lat = k_ref.shape[1]
    lv = lam_ref[...]
    lam = (jnp.exp(jnp.sum(lv[0:1] * lv[1:2], axis=-1, keepdims=True))
           - jnp.exp(jnp.sum(lv[2:3] * lv[3:4], axis=-1, keepdims=True)) + LAM_INIT)

    n_ctx = kc_ref.shape[1]
    chunks = [(False, r, r, ATT_KEY_CHUNK) for r in range(0, lat, ATT_KEY_CHUNK)] + [(True, 0, lat, n_ctx)]

    def stage(qt, keys_ref, ctx_keys_ref, s_new, m_new, s_old, m_old, hook=None):
        m_run = acc = denom = None
        feat = lax.broadcasted_iota(jnp.int32, qt.shape, 0)
        qz = jnp.concatenate([jnp.where(feat < HEAD_DIM, qt, jnp.zeros_like(qt)),
                              jnp.where(feat >= HEAD_DIM, qt, jnp.zeros_like(qt))], axis=1)
        if s_old is not None:
            m_prev = m_old[...]
        for n, (is_ctx, src, dst, rows) in enumerate(chunks):
            keys = ctx_keys_ref[0, src:src + rows, :] if is_ctx else keys_ref[0, src:src + rows, :]
            s = _dot(keys, qz)
            s_new[dst:dst + rows, :] = s
            cmax = jnp.max(s, axis=0, keepdims=True)
            m_run = cmax if m_run is None else jnp.maximum(m_run, cmax)
            if s_old is not None:
                p = jnp.exp2(s_old[dst:dst + rows, :] - m_prev)
                psum = jnp.sum(p, axis=0, keepdims=True)
                vals = vct_ref[0, :, src:src + rows] if is_ctx else vt_ref[0, :, src:src + rows]
                part = _dot(vals, p.astype(BF16))
                denom = psum if denom is None else denom + psum
                acc = part if acc is None else acc + part
            if n == 0 and hook is not None:
                hook()
        m_new[...] = m_run
        return acc, denom

    def finish(acc, denom):
        r = 1.0 / denom
        ot = acc[:, :tq] * r[:, :tq] - acc[:, tq:] * (lam * r[:, tq:])
        return (_rms(ot.T, sg_ref[...]) * (1.0 - LAM_INIT)).astype(BF16)

    @pl.when(pl.program_id(0) == 0)
    def _():
        stage(qc_ref[0, :, :tq], k_ref, kc_ref, sa_ref, ma_ref, None, None)

    bufs = ((sa_ref, ma_ref), (sb_ref, mb_ref))
    pending = None
    for n in range(ATT_BLOCKS):
        if n + 1 < ATT_BLOCKS:
            new = (qc_ref[0, :, (n + 1) * tq:(n + 2) * tq], k_ref, kc_ref)
        else:
            new = (qn_ref[0, :, :tq], kn_ref, kcn_ref)
        raw = stage(*new, *bufs[(n + 1) % 2], *bufs[n % 2], hook=pending)

        def pending(raw=raw, n=n):
            o_ref[0, n * tq:(n + 1) * tq, :] = finish(*raw)

    pending()


def _attn(qt, k, vt, kc, vct, lamv, subln_g):
    b, l, _ = k.shape
    c = kc.shape[1]
    dv = 2 * HEAD_DIM
    tq = ATT_TQ
    tstep = ATT_BLOCKS * tq
    assert l % tstep == 0 and ATT_BLOCKS % 2 == 0
    pairs = l // tstep
    steps = b * N_HEADS * pairs

    def split(j):
        return j // (N_HEADS * pairs), (j // pairs) % N_HEADS, j % pairs

    def cur(fn):
        return lambda j: fn(*split(j))

    def nxt(fn):
        return lambda j: fn(*split(jnp.minimum(j + 1, steps - 1)))

    q_map = lambda bi, h, i: (bi, h, i)
    k_map = lambda bi, h, i: (bi, 0, h)
    v_map = lambda bi, h, i: (bi, h, 0)
    return pl.pallas_call(
        _attn_body,
        grid=(steps,),
        in_specs=[
            pl.BlockSpec((1, dv, tstep), cur(q_map)), pl.BlockSpec((1, dv, tstep), nxt(q_map)),
            pl.BlockSpec((1, l, dv), cur(k_map)), pl.BlockSpec((1, l, dv), nxt(k_map)),
            pl.BlockSpec((1, c, dv), cur(k_map)), pl.BlockSpec((1, c, dv), nxt(k_map)),
            pl.BlockSpec((1, dv, l), cur(v_map)), pl.BlockSpec((1, dv, c), cur(v_map)),
            _const_spec(lamv.shape), _const_spec(subln_g.shape),
        ],
        out_specs=pl.BlockSpec((1, tstep, dv), cur(lambda bi, h, i: (bi, i, h))),
        out_shape=jax.ShapeDtypeStruct((b, l, N_HEADS * dv), BF16),
        scratch_shapes=[pltpu.VMEM((l + c, 2 * tq), F32), pltpu.VMEM((l + c, 2 * tq), F32),
                        pltpu.VMEM((1, 2 * tq), F32), pltpu.VMEM((1, 2 * tq), F32)],
        compiler_params=_params("arbitrary"),
        name="attn",
    )(qt, qt, k, k, kc, kc, vt, vct, lamv, subln_g)


def _post_body(x_ref, zh_ref, oa_ref, gate_ref, g1_ref, sh2_ref, sc2_ref, g2_ref,
               gpost_ref, gfpre_ref, gfpost_ref,
               whu_ref, wau_ref, wo_ref, wg_ref, wu_ref, wd_ref, o_ref, *, ff_chunks):
    d = x_ref.shape[2]
    half = x_ref.shape[1] // 2
    rows_a, rows_b = slice(0, half), slice(half, 2 * half)

    def lifts(rows):
        return _dot(zh_ref[0, rows, :], whu_ref[...]), _dot(oa_ref[0, rows, :], wau_ref[...])

    def out_proj(rows, y_hy, y_att):
        mixed = gate_ref[0, rows, :d].astype(F32) * y_hy + gate_ref[0, rows, d:].astype(F32) * y_att
        return _dot(mixed.astype(BF16), wo_ref[...])

    def residual_and_norm(rows, mixed):
        x1 = x_ref[0, rows, :] + g1_ref[0] * _rms(mixed, gpost_ref[...])
        return x1, (_rms(x1, gfpre_ref[...]) * (1.0 + sc2_ref[0]) + sh2_ref[0]).astype(BF16)

    def ffn_chunk(hf, f, chunk):
        c0, cw = chunk
        a = _dot(hf, wg_ref[:, c0:c0 + cw])
        t = (a * jax.nn.sigmoid(a) * _dot(hf, wu_ref[:, c0:c0 + cw])).astype(BF16)
        part = _dot(t, wd_ref[c0:c0 + cw, :])
        return part if f is None else f + part

    def finish(rows, x1, f):
        o_ref[0, rows, :] = x1 + g2_ref[0] * _rms(f, gfpost_ref[...])

    mixed_a = out_proj(rows_a, *lifts(rows_a))
    lifted_b = lifts(rows_b)
    x1_a, hf_a = residual_and_norm(rows_a, mixed_a)
    mixed_b = out_proj(rows_b, *lifted_b)
    f_a = ffn_chunk(hf_a, None, ff_chunks[0])
    x1_b, hf_b = residual_and_norm(rows_b, mixed_b)
    for chunk in ff_chunks[1:]:
        f_a = ffn_chunk(hf_a, f_a, chunk)
    f_b = ffn_chunk(hf_b, None, ff_chunks[0])
    finish(rows_a, x1_a, f_a)
    for chunk in ff_chunks[1:]:
        f_b = ffn_chunk(hf_b, f_b, chunk)
    finish(rows_b, x1_b, f_b)


def _post(x, zh, oa, gate, g1, sh2, sc2, g2, gpost, gfpre, gfpost, whu, wau, wo, wg, wu, wd):
    b, l, d = x.shape
    dff = wg.shape[1]
    tm = 512
    ff_chunk = 4 * MXU_DIM
    ff_chunks = tuple((c0, min(ff_chunk, dff - c0)) for c0 in range(0, dff, ff_chunk))
    tile = lambda bi, i: (bi, i, 0)
    per_b = lambda bi, i: (bi, 0, 0)
    return pl.pallas_call(
        functools.partial(_post_body, ff_chunks=ff_chunks),
        grid=(b, l // tm),
        in_specs=[
            pl.BlockSpec((1, tm, d), tile),
            pl.BlockSpec((1, tm, zh.shape[2]), tile),
            pl.BlockSpec((1, tm, oa.shape[2]), tile),
            pl.BlockSpec((1, tm, gate.shape[2]), tile),
            pl.BlockSpec((1, 1, d), per_b), pl.BlockSpec((1, 1, d), per_b),
            pl.BlockSpec((1, 1, d), per_b), pl.BlockSpec((1, 1, d), per_b),
            _const_spec((1, d)), _const_spec((1, d)), _const_spec((1, d)),
        ] + [_const_spec(w.shape) for w in (whu, wau, wo, wg, wu, wd)],
        out_specs=pl.BlockSpec((1, tm, d), tile),
        out_shape=jax.ShapeDtypeStruct((b, l, d), F32),
        compiler_params=_params("arbitrary", "arbitrary"),
        name="post",
    )(x, zh, oa, gate, g1, sh2, sc2, g2, gpost, gfpre, gfpost, whu, wau, wo, wg, wu, wd)


@functools.lru_cache(maxsize=None)
def _rope_tables(l):
    quarter = HEAD_DIM // 4
    inv = ROPE_BASE ** (-np.arange(quarter, dtype=np.float64) / quarter)
    t = np.arange(l)
    ang_r = (t // GRID_W)[:, None] * inv[None, :]
    ang_c = (t % GRID_W)[:, None] * inv[None, :]
    ang = np.concatenate([ang_r, ang_r, ang_c, ang_c], axis=1)
    reps = D_ATTN // HEAD_DIM
    cos = np.tile(np.cos(ang), (1, reps)).astype(np.float32)
    second = (np.arange(D_ATTN) % (2 * quarter)) >= quarter
    sin = (np.tile(np.sin(ang), (1, reps)) * np.where(second, 1.0, -1.0)[None, :]).astype(np.float32)
    return cos, sin


@functools.lru_cache(maxsize=None)
def _filter_features(l, emb, kpad):
    bands = (emb - 1) // 2
    t = np.linspace(0.0, 1.0, l, dtype=np.float32).astype(np.float64)[:, None]
    w = (2.0 * math.pi / l) * np.arange(l, dtype=np.float64)[:, None]
    fr = np.linspace(1e-4, bands - 1, bands, dtype=np.float32).astype(np.float64)[None, :]
    z = np.concatenate([t, np.cos(fr * w), -np.sin(fr * w)], axis=-1)
    z = np.pad(z, ((0, 0), (0, kpad - emb))).astype(np.float32)
    deltas = np.abs(np.linspace(math.log(FILTER_TARGET) / SLOW_DECAY_PCT,
                                math.log(FILTER_TARGET) / FAST_DECAY_PCT, D_HYENA, dtype=np.float32))
    return z, deltas[None, :].astype(np.float32)


def _pad_to(a, rows, cols, value=0.0):
    return jnp.pad(a, ((0, rows - a.shape[0]), (0, cols - a.shape[1])), constant_values=value)


def kernel(x, c, ctx, c_ctx, w_ada, b_ada, g_mix_pre, g_mix_post, g_ffn_pre, g_ffn_post, w_in, hy_conv_w, hy_conv_b, hy_f_w1, hy_f_b1, hy_f_w2, hy_f_b2, hy_f_w3, hy_f_b3, hy_f_freq, hy_bias, lambda_q1, lambda_k1, lambda_q2, lambda_k2, att_subln_g, w_hy_up, w_att_up, w_out, w_ffn_gate, w_ffn_up, w_ffn_down):
    b, l, d = x.shape
    order = hy_bias.shape[1]
    off_q = (order + 1) * D_HYENA
    off_k = off_q + D_ATTN
    off_v = off_k + D_ATTN
    off_g = off_v + D_ATTN
    assert w_in.shape[0] == 1, "single-layer block"
    assert l % HY_ROWS == 0 and l % 512 == 0

    pad_rows = -(-(b + 1) // 8) * 8
    cc = jnp.concatenate([c, c_ctx[None, :], jnp.zeros((pad_rows - b - 1, d), F32)], axis=0)
    ada = _ada(cc, w_ada[0], b_ada[0][None, :])
    sh1, sc1, g1, sh2, sc2, g2 = (ada[:b, i * d:(i + 1) * d][:, None, :] for i in range(6))
    csh1, csc1 = ada[b:b + 1, 0:d], ada[b:b + 1, d:2 * d]

    row = lambda v: v[0][None, :]
    w = w_in[0]
    cos, sin = _rope_tables(l)
    q_scale = HEAD_DIM ** -0.5 * math.log2(math.e)
    wq = (w[:, off_q:off_k] * q_scale).astype(BF16)
    wk = w[:, off_k:off_v].astype(BF16)
    wv = w[:, off_v:off_g].astype(BF16)

    kc, vct, wqt, wvt = _ctx_kv(ctx, row(g_mix_pre), csh1, csc1, wk, wq, wv)

    u, qt, k, vt, gate = _proj(x, row(g_mix_pre), sh1, sc1, jnp.asarray(cos), jnp.asarray(sin),
                               jnp.asarray(cos.T), jnp.asarray(sin.T), hy_conv_w[0], row(hy_conv_b),
                               w[:, :off_q].astype(BF16), wqt, wk, wvt, w[:, off_g:].astype(BF16))

    consts = tuple(jnp.asarray(a).astype(BF16) for a in _fft_constants(l))
    zfeat, deltas = _filter_features(l, hy_f_w1.shape[1], LANES)
    hf = _filt(jnp.asarray(zfeat),
               _pad_to(hy_f_w1[0], LANES, LANES), _pad_to(row(hy_f_b1), 1, LANES),
               _pad_to(hy_f_w2[0], LANES, LANES), _pad_to(row(hy_f_b2), 1, LANES),
               _pad_to(row(hy_f_freq), 1, LANES, 1.0),
               _pad_to(hy_f_w3[0], LANES, hy_f_w3.shape[2]), row(hy_f_b3),
               jnp.asarray(deltas), consts[0], consts[3], order, l)

    zh = _hyena(u, hy_bias[0], hf, consts[:3], order)

    lamv = jnp.concatenate([lambda_q1, lambda_k1, lambda_q2, lambda_k2], axis=0)
    oa = _attn(qt, k, vt, kc, vct, lamv, row(att_subln_g))

    return _post(x, zh, oa, gate, g1, sh2, sc2, g2, row(g_mix_post), row(g_ffn_pre), row(g_ffn_post),
                 w_hy_up[0].astype(BF16), w_att_up[0].astype(BF16), w_out[0].astype(BF16),
                 w_ffn_gate[0].astype(BF16), w_ffn_up[0].astype(BF16), w_ffn_down[0].astype(BF16))
```

```python
import functools
import math

import numpy as np
import jax
import jax.numpy as jnp
from jax import lax
from jax.experimental import pallas as pl
from jax.experimental.pallas import tpu as pltpu

F32 = jnp.float32
BF16 = jnp.bfloat16

EPS = 1e-6
GRID_W = 64
HEAD_DIM = 64
N_HEADS = 4
D_HYENA = 512
D_ATTN = N_HEADS * 2 * HEAD_DIM
ROPE_BASE = 10000.0
FILTER_TARGET = 1e-2
FAST_DECAY_PCT = 0.3
SLOW_DECAY_PCT = 1.5
LAM_INIT = 0.8 - 0.6 * math.exp(0.0)

LANES = 128
MXU_DIM = 256
VMEM_LIMIT_BYTES = 56 * 1024 * 1024

FFT_N2 = 256
FFT_GROUP = 16
HY_CB = 256
FFT_GROUP_UNROLL = 16


def _dot(a, b):
    return jnp.dot(a, b, preferred_element_type=F32)


def _dot_x3(a, b):
    a_hi = a.astype(BF16)
    a_lo = (a - a_hi.astype(F32)).astype(BF16)
    b_hi = b.astype(BF16)
    b_lo = (b - b_hi.astype(F32)).astype(BF16)
    return _dot(a_hi, b_hi) + _dot(a_lo, b_hi) + _dot(a_hi, b_lo)


def _dot_nt(a, b):
    return lax.dot_general(a, b, (((1,), (1,)), ((), ())), preferred_element_type=F32)


def _rms(x, g):
    return x * lax.rsqrt(jnp.mean(x * x, axis=-1, keepdims=True) + EPS) * g


def _const_spec(shape):
    zeros = (0,) * len(shape)
    return pl.BlockSpec(shape, lambda *_: zeros, pipeline_mode=pl.Buffered(1))


def _params(*semantics):
    return pltpu.CompilerParams(dimension_semantics=semantics, vmem_limit_bytes=VMEM_LIMIT_BYTES)


def _ada_body(c_ref, w_ref, b_ref, o_ref):
    c = c_ref[...]
    s = c * jax.nn.sigmoid(c)
    w = w_ref[...]
    s_hi = s.astype(BF16)
    s_lo = (s - s_hi.astype(F32)).astype(BF16)
    w_hi = w.astype(BF16)
    w_lo = (w - w_hi.astype(F32)).astype(BF16)
    rows = s.shape[0]
    both = _dot(jnp.concatenate([s_hi, s_lo], axis=0), w_hi)
    o_ref[...] = both[:rows] + both[rows:] + _dot(s_hi, w_lo) + b_ref[...]


def _ada(cc, w_ada, b_ada):
    rows, d = cc.shape
    n = w_ada.shape[1]
    tn = 1024
    return pl.pallas_call(
        _ada_body,
        grid=(n // tn,),
        in_specs=[
            pl.BlockSpec((rows, d), lambda j: (0, 0)),
            pl.BlockSpec((d, tn), lambda j: (0, j)),
            pl.BlockSpec((1, tn), lambda j: (0, j)),
        ],
        out_specs=pl.BlockSpec((rows, tn), lambda j: (0, j)),
        out_shape=jax.ShapeDtypeStruct((rows, n), F32),
        compiler_params=_params("arbitrary"),
        name="ada",
    )(cc, w_ada, b_ada)


def _ctx_kv_body(x_ref, g_ref, sh_ref, sc_ref, wk_ref, wq_ref, wv_ref, k_ref, vt_ref, wqt_ref, wvt_ref):
    @pl.when(pl.program_id(0) == 0)
    def _():
        wqt_ref[...] = wq_ref[...].astype(F32).T.astype(BF16)
        wvt_ref[...] = wv_ref[...].astype(F32).T.astype(BF16)

    h = _rms(x_ref[0], g_ref[...]) * (1.0 + sc_ref[...]) + sh_ref[...]
    hb = h.astype(BF16)
    k_ref[0] = _dot(hb, wk_ref[...]).astype(BF16)
    vt_ref[0] = _dot_nt(wvt_ref[...], hb).astype(BF16)


def _ctx_kv(ctx, g, sh, sc, wk, wq, wv):
    b, n_ctx, d = ctx.shape
    n = wk.shape[1]
    bf = lambda shape: jax.ShapeDtypeStruct(shape, BF16)
    whole_t = pl.BlockSpec((n, d), lambda i: (0, 0))
    return pl.pallas_call(
        _ctx_kv_body,
        grid=(b,),
        in_specs=[
            pl.BlockSpec((1, n_ctx, d), lambda i: (i, 0, 0)),
            _const_spec((1, d)), _const_spec((1, d)), _const_spec((1, d)),
            _const_spec((d, n)), _const_spec((d, n)), _const_spec((d, n)),
        ],
        out_specs=[pl.BlockSpec((1, n_ctx, n), lambda i: (i, 0, 0)),
                   pl.BlockSpec((1, n, n_ctx), lambda i: (i, 0, 0)), whole_t, whole_t],
        out_shape=[bf((b, n_ctx, n)), bf((b, n, n_ctx)), bf((n, d)), bf((n, d))],
        compiler_params=_params("arbitrary"),
        name="ctx_kv",
    )(ctx, g, sh, sc, wk, wq, wv)


ROT = HEAD_DIM // 4
PROJ_HALO = 16


def _proj_body(x_ref, xp_ref, xn_ref, g_ref, sh_ref, sc_ref, cos_ref, sin_ref, cost_ref, sint_ref,
               cw_ref, cb_ref, wu_ref, wqt_ref, wk_ref, wvt_ref, wg_ref,
               u_ref, qt_ref, k_ref, vt_ref, gate_ref):
    tm = x_ref.shape[1]
    half = tm // 2
    rows_a, rows_b = slice(0, half), slice(half, 2 * half)

    def modulated(x):
        return (_rms(x, g_ref[...]) * (1.0 + sc_ref[0]) + sh_ref[0]).astype(BF16)

    def normed(rows):
        return modulated(x_ref[0, rows, :])

    def gates_and_keys(rows, hb):
        gate_ref[0, rows, :] = jax.nn.sigmoid(_dot(hb, wg_ref[...])).astype(BF16)
        kv = _dot(hb, wk_ref[...])
        lane = lax.broadcasted_iota(jnp.int32, (kv.shape[0], LANES), 1)
        first = (lane % (2 * ROT)) < ROT
        slabs = []
        for c0 in range(0, kv.shape[1], LANES):
            slab = kv[:, c0:c0 + LANES]
            slabs.append(jnp.where(first, pltpu.roll(slab, LANES - ROT, axis=1), pltpu.roll(slab, ROT, axis=1)))
        k_ref[0, rows, :] = (kv * cos_ref[rows, :]
                             + jnp.concatenate(slabs, axis=1) * sin_ref[rows, :]).astype(BF16)

    i = pl.program_id(0)
    hb_a = normed(rows_a)
    u_a = _dot(jnp.concatenate([modulated(xp_ref[0]), hb_a], axis=0), wu_ref[...])
    hb_b = normed(rows_b)
    gates_and_keys(rows_a, hb_a)
    u_b = _dot(jnp.concatenate([hb_b, modulated(xn_ref[0])], axis=0), wu_ref[...])
    keep_prev = (i > 0).astype(F32)
    keep_next = (i < pl.num_programs(0) - 1).astype(F32)

    def short_conv(cols):
        ext = jnp.concatenate([u_a[:PROJ_HALO, cols] * keep_prev, u_a[PROJ_HALO:, cols], u_b[:half, cols],
                               u_b[half:, cols] * keep_next], axis=0)
        n = ext.shape[0]
        cw = cw_ref[:, cols]
        prev = pltpu.roll(ext, 1, axis=0)[PROJ_HALO:PROJ_HALO + tm]
        nxt = pltpu.roll(ext, n - 1, axis=0)[PROJ_HALO:PROJ_HALO + tm]
        u_ref[0, :, cols] = (prev * cw[0:1] + ext[PROJ_HALO:PROJ_HALO + tm] * cw[1:2] + nxt * cw[2:3]
                             + cb_ref[:, cols]).astype(BF16)

    third = u_a.shape[1] // 3
    short_conv(slice(0, third))
    gates_and_keys(rows_b, hb_b)
    short_conv(slice(third, 2 * third))
    hb = jnp.concatenate([hb_a, hb_b], axis=0)
    qv = _dot_nt(wqt_ref[...], hb)
    partner = jnp.concatenate(
        [qv[r + ROT:r + 2 * ROT] if part == 0 else qv[r:r + ROT]
         for r in range(0, qv.shape[0], 2 * ROT) for part in (0, 1)], axis=0)
    qt_ref[0] = (qv * cost_ref[...] + partner * sint_ref[...]).astype(BF16)
    short_conv(slice(2 * third, 3 * third))
    vt_ref[0] = _dot_nt(wvt_ref[...], hb).astype(BF16)


def _proj(x, g, sh, sc, cos, sin, cost, sint, conv_w, conv_b, wu, wqt, wk, wvt, wg):
    b, l, d = x.shape
    tm = 512
    per_halo = tm // PROJ_HALO
    n_halo = l // PROJ_HALO
    tile = lambda i, bi: (bi, i, 0)
    tile_t = lambda i, bi: (bi, 0, i)
    per_b = lambda i, bi: (bi, 0, 0)
    before = lambda i, bi: (bi, jnp.maximum(i * per_halo - 1, 0), 0)
    after = lambda i, bi: (bi, jnp.minimum((i + 1) * per_halo, n_halo - 1), 0)
    bf = lambda shape: jax.ShapeDtypeStruct(shape, BF16)
    n_u, n_g = wu.shape[1], wg.shape[1]
    return pl.pallas_call(
        _proj_body,
        grid=(l // tm, b),
        in_specs=[
            pl.BlockSpec((1, tm, d), tile),
            pl.BlockSpec((1, PROJ_HALO, d), before), pl.BlockSpec((1, PROJ_HALO, d), after),
            _const_spec((1, d)),
            pl.BlockSpec((1, 1, d), per_b), pl.BlockSpec((1, 1, d), per_b),
            pl.BlockSpec((tm, D_ATTN), lambda i, bi: (i, 0)), pl.BlockSpec((tm, D_ATTN), lambda i, bi: (i, 0)),
            pl.BlockSpec((D_ATTN, tm), lambda i, bi: (0, i)), pl.BlockSpec((D_ATTN, tm), lambda i, bi: (0, i)),
        ] + [_const_spec(w.shape) for w in (conv_w, conv_b, wu, wqt, wk, wvt, wg)],
        out_specs=[pl.BlockSpec((1, tm, n_u), tile), pl.BlockSpec((1, D_ATTN, tm), tile_t),
                   pl.BlockSpec((1, tm, D_ATTN), tile), pl.BlockSpec((1, D_ATTN, tm), tile_t),
                   pl.BlockSpec((1, tm, n_g), tile)],
        out_shape=[bf((b, l, n_u)), bf((b, D_ATTN, l)), bf((b, l, D_ATTN)), bf((b, D_ATTN, l)), bf((b, l, n_g))],
        compiler_params=_params("arbitrary", "arbitrary"),
        name="proj",
    )(x, x, x, g, sh, sc, cos, sin, cost, sint, conv_w, conv_b, wu, wqt, wk, wvt, wg)


def _fft_dims(l):
    n = 2 * l
    n1 = n // FFT_N2
    h1 = n1 // 2
    k1 = h1 + 1
    ng = FFT_N2 // FFT_GROUP
    return n, n1, h1, k1, ng


@functools.lru_cache(maxsize=None)
def _fft_constants(l):
    n, n1, h1, k1n, ng = _fft_dims(l)
    g = FFT_GROUP
    ma = np.zeros((ng, k1n, 2, g, h1, g), np.float64)
    for gi in range(ng):
        for j in range(g):
            n2 = gi * g + j
            nn = FFT_N2 * np.arange(h1)[None, :] + n2
            theta = 2.0 * np.pi * ((np.arange(k1n)[:, None] * nn) % n) / n
            ma[gi, :, 0, j, :, j] = np.cos(theta)
            ma[gi, :, 1, j, :, j] = -np.sin(theta)
    weight = np.full((k1n,), 2.0)
    weight[0] = weight[-1] = 1.0
    ma_fwd = ma.reshape(ng, k1n * 2 * g, h1 * g)
    ma_inv = (ma * (weight / n)[None, :, None, None, None, None]).reshape(ng, k1n * 2 * g, h1 * g)
    ma_inv = np.transpose(ma_inv, (0, 2, 1))
    kpad = -(-ma_inv.shape[2] // LANES) * LANES
    ma_inv = np.pad(ma_inv, ((0, 0), (0, 0), (0, kpad - ma_inv.shape[2])))
    ang = 2.0 * np.pi * ((np.arange(FFT_N2)[:, None] * np.arange(FFT_N2)[None, :]) % FFT_N2) / FFT_N2
    fr, fi = np.cos(ang), -np.sin(ang)
    dft3 = np.stack([fr, fi, fr + fi, fr - fi])
    mb_filt = np.block([[fr, -fi, fr, -fi], [fi, fr, -fi, -fr]])
    return tuple(np.asarray(a, np.float32) for a in (ma_fwd, ma_inv, dft3, mb_filt))


def _fft_stage_a(src_ref, ma_ref, spec_ref, l):
    _, _, h1, k1n, ng = _fft_dims(l)
    g = FFT_GROUP

    def group(gi, carry):
        off = gi * g
        xg = jnp.concatenate(
            [src_ref[pl.ds(pl.multiple_of(FFT_N2 * i + off, g), g), :] for i in range(h1)], axis=0)
        bg = _dot(ma_ref[gi], xg.astype(BF16)).astype(BF16)
        for c in range(2 * k1n):
            spec_ref[pl.ds(pl.multiple_of(c * FFT_N2 + off, g), g), :] = bg[c * g:(c + 1) * g]
        return carry

    lax.fori_loop(0, ng, group, 0, unroll=FFT_GROUP_UNROLL)


def _fft_stage_a_inv(spec_ref, ma_inv_ref, emit, l):
    _, _, h1, k1n, ng = _fft_dims(l)
    g = FFT_GROUP
    kpad = ma_inv_ref.shape[2] - 2 * k1n * g

    def group(gi, carry):
        off = gi * g
        chunks = [spec_ref[pl.ds(pl.multiple_of(c * FFT_N2 + off, g), g), :] for c in range(2 * k1n)]
        if kpad:
            chunks.append(jnp.zeros((kpad, spec_ref.shape[1]), BF16))
        yg = _dot(ma_inv_ref[gi], jnp.concatenate(chunks, axis=0))
        for i in range(h1):
            emit(pl.ds(pl.multiple_of(FFT_N2 * i + off, g), g), yg[i * g:(i + 1) * g])
        return carry

    lax.fori_loop(0, ng, group, 0, unroll=FFT_GROUP_UNROLL)


def _filt_body(z_ref, w1_ref, b1_ref, w2_ref, b2_ref, fq_ref, w3f_ref, b3f_ref, w3b_ref, b3b_ref,
               dl_ref, ma_ref, mb_ref, hf_ref, hdn_ref, sig_ref, specf_ref, specb_ref, *, l):
    _, _, _, k1n, _ = _fft_dims(l)

    @pl.when((pl.program_id(0) == 0) & (pl.program_id(1) == 0))
    def _():
        fq = fq_ref[...]
        h1 = jnp.sin(fq * (_dot_x3(z_ref[...], w1_ref[...]) + b1_ref[...]))
        hdn_ref[...] = jnp.sin(fq * (_dot_x3(h1, w2_ref[...]) + b2_ref[...])).astype(BF16)

    hdn = hdn_ref[...]
    cb = specf_ref.shape[1]
    blk = 2 * FFT_N2
    rows = lax.broadcasted_iota(jnp.int32, (l, cb), 0)
    window = jnp.exp(rows.astype(F32) * (-1.0 / (l - 1)) * dl_ref[...])

    sig_ref[...] = (_dot(hdn, w3f_ref[...].astype(BF16)) + b3f_ref[...]) * window
    _fft_stage_a(sig_ref, ma_ref, specf_ref, l)
    sig_ref[...] = jnp.where(rows > 0, (_dot(hdn, w3b_ref[...].astype(BF16)) + b3b_ref[...]) * window, 0.0)
    _fft_stage_a(sig_ref, ma_ref, specb_ref, l)

    def blocks(starts):
        x = jnp.concatenate(
            [jnp.concatenate([specf_ref[pl.ds(st, blk), :], specb_ref[pl.ds(st, blk), :]], axis=0)
             for st in starts], axis=1)
        s = _dot(mb_ref[...], x).astype(BF16)
        for n, st in enumerate(starts):
            hf_ref[0, pl.ds(st, blk), :] = s[:, n * cb:(n + 1) * cb]

    def per_pair(i, carry):
        st = pl.multiple_of(i * 2 * blk, blk)
        blocks([st, st + blk])
        return carry

    lax.fori_loop(0, k1n // 2, per_pair, 0)
    if k1n % 2:
        blocks([(k1n - 1) * blk])


def _filt(zfeat, w1, b1, w2, b2, fq, w3, b3, deltas, ma_fwd, mb_filt, order, l):
    _, _, _, k1n, _ = _fft_dims(l)
    cb = HY_CB
    nblk = D_HYENA // cb
    srows = k1n * 2 * FFT_N2
    fcol = lambda o, c: (0, (2 * o) * nblk + c)
    bcol = lambda o, c: (0, (2 * o + 1) * nblk + c)
    return pl.pallas_call(
        functools.partial(_filt_body, l=l),
        grid=(order, nblk),
        in_specs=[
            _const_spec(zfeat.shape), _const_spec(w1.shape), _const_spec(b1.shape),
            _const_spec(w2.shape), _const_spec(b2.shape), _const_spec(fq.shape),
            pl.BlockSpec((w3.shape[0], cb), fcol), pl.BlockSpec((1, cb), fcol),
            pl.BlockSpec((w3.shape[0], cb), bcol), pl.BlockSpec((1, cb), bcol),
            pl.BlockSpec((1, cb), lambda o, c: (0, c)),
            _const_spec(ma_fwd.shape), _const_spec(mb_filt.shape),
        ],
        out_specs=pl.BlockSpec((1, srows, cb), lambda o, c: (o, 0, c)),
        out_shape=jax.ShapeDtypeStruct((order, srows, D_HYENA), BF16),
        scratch_shapes=[
            pltpu.VMEM((l, w2.shape[1]), BF16),
            pltpu.VMEM((l, cb), F32),
            pltpu.VMEM((srows, cb), BF16),
            pltpu.VMEM((srows, cb), BF16),
        ],
        compiler_params=_params("arbitrary", "arbitrary"),
        name="filt",
    )(zfeat, w1, b1, w2, b2, fq, w3, b3, w3, b3, deltas, ma_fwd, mb_filt)


HY_ROWS = 512


def _hyena_body(u0_ref, u1_ref, u2_ref, db_ref, hf_ref,
                ma_ref, mai_ref, dft_ref, o_ref, z_ref, y_ref, spec_ref, *, l, order):
    _, _, _, k1n, _ = _fft_dims(l)
    blk = 2 * FFT_N2
    gates = (u1_ref, u2_ref)

    for r0 in range(0, l, HY_ROWS):
        z_ref[r0:r0 + HY_ROWS, :] = u0_ref[0, r0:r0 + HY_ROWS, :].astype(F32)

    for o in range(order):
        _fft_stage_a(z_ref, ma_ref, spec_ref, l)

        cb = spec_ref.shape[1]
        groups = [list(range(i, min(i + 2, k1n))) for i in range(0, k1n, 2)]

        def stage_b(ks):
            br = jnp.concatenate([spec_ref[k * blk:k * blk + FFT_N2, :] for k in ks], axis=1)
            bi = jnp.concatenate([spec_ref[k * blk + FFT_N2:(k + 1) * blk, :] for k in ks], axis=1)
            k1 = _dot(dft_ref[2], br)
            return k1 - _dot(dft_ref[1], br + bi), k1 + _dot(dft_ref[0], bi - br)

        def filter_and_invert(ks, s, o=o):
            sr, si = s
            hr = jnp.concatenate([hf_ref[o, k * blk:k * blk + FFT_N2, :] for k in ks], axis=1).astype(F32)
            hi = jnp.concatenate([hf_ref[o, k * blk + FFT_N2:(k + 1) * blk, :] for k in ks], axis=1).astype(F32)
            yr = sr * hr - si * hi
            yi = sr * hi + si * hr
            k1 = _dot(dft_ref[3], yr.astype(BF16))
            cr = (k1 + _dot(dft_ref[1], (yr + yi).astype(BF16))).astype(BF16)
            ci = (k1 + _dot(dft_ref[0], (yi - yr).astype(BF16))).astype(BF16)
            for n, k in enumerate(ks):
                spec_ref[k * blk:k * blk + FFT_N2, :] = cr[:, n * cb:(n + 1) * cb]
                spec_ref[k * blk + FFT_N2:(k + 1) * blk, :] = ci[:, n * cb:(n + 1) * cb]

        s_next = stage_b(groups[0])
        for gi, ks in enumerate(groups):
            s_cur = s_next
            if gi + 1 < len(groups):
                s_next = stage_b(groups[gi + 1])
            filter_and_invert(ks, s_cur)

        def store_y(rows, y):
            y_ref[rows, :] = y

        _fft_stage_a_inv(spec_ref, mai_ref, store_y, l)

        dbias = db_ref[o:o + 1, :]
        for r0 in range(0, l, HY_ROWS):
            rows = slice(r0, r0 + HY_ROWS)
            znew = gates[o][0, rows, :].astype(F32) * (y_ref[rows, :] + z_ref[rows, :] * dbias)
            if o + 1 < order:
                z_ref[rows, :] = znew
            else:
                o_ref[0, rows, :] = znew.astype(BF16)


def _hyena(u, dbias, hf, consts, order):
    b, l, _ = u.shape
    _, _, _, k1n, _ = _fft_dims(l)
    cb = HY_CB
    nblk = D_HYENA // cb
    srows = k1n * 2 * FFT_N2
    ma_fwd, ma_inv, dft3 = consts
    upart = lambda p: pl.BlockSpec((1, l, cb), lambda c, bi: (bi, 0, p * nblk + c))
    return pl.pallas_call(
        functools.partial(_hyena_body, l=l, order=order),
        grid=(nblk, b),
        in_specs=[
            upart(0), upart(1), upart(2),
            pl.BlockSpec((order, cb), lambda c, bi: (0, c)),
            pl.BlockSpec((order, srows, cb), lambda c, bi: (0, 0, c), pipeline_mode=pl.Buffered(1)),
            _const_spec(ma_fwd.shape), _const_spec(ma_inv.shape),
            _const_spec(dft3.shape),
        ],
        out_specs=pl.BlockSpec((1, l, cb), lambda c, bi: (bi, 0, c)),
        out_shape=jax.ShapeDtypeStruct((b, l, D_HYENA), BF16),
        scratch_shapes=[
            pltpu.VMEM((l, cb), F32),
            pltpu.VMEM((l, cb), F32),
            pltpu.VMEM((srows, cb), BF16),
        ],
        compiler_params=_params("arbitrary", "arbitrary"),
        name="hyena",
    )(u, u, u, dbias, hf, ma_fwd, ma_inv, dft3)


ATT_TQ = MXU_DIM
ATT_KEY_CHUNK = 2 * MXU_DIM
ATT_BLOCKS = 8


def _attn_body(qc_ref, qn_ref, k_ref, kn_ref, kc_ref, kcn_ref, vt_ref, vct_ref, lam_ref, sg_ref, o_ref,
               sa_ref, sb_ref, ma_ref, mb_ref):
    tq = ATT_TQ
    lat = k_ref.shape[1]
    lv = lam_ref[...]
    lam = (jnp.exp(jnp.sum(lv[0:1] * lv[1:2], axis=-1, keepdims=True))
           - jnp.exp(jnp.sum(lv[2:3] * lv[3:4], axis=-1, keepdims=True)) + LAM_INIT)

    n_ctx = kc_ref.shape[1]
    chunks = [(False, r, r, ATT_KEY_CHUNK) for r in range(0, lat, ATT_KEY_CHUNK)] + [(True, 0, lat, n_ctx)]

    def stage(qt, keys_ref, ctx_keys_ref, s_new, m_new, s_old, m_old, hook=None):
        m_run = acc = denom = None
        feat = lax.broadcasted_iota(jnp.int32, qt.shape, 0)
        qz = jnp.concatenate([jnp.where(feat < HEAD_DIM, qt, jnp.zeros_like(qt)),
                              jnp.where(feat >= HEAD_DIM, qt, jnp.zeros_like(qt))], axis=1)
        if s_old is not None:
            m_prev = m_old[...]
        for n, (is_ctx, src, dst, rows) in enumerate(chunks):
            keys = ctx_keys_ref[0, src:src + rows, :] if is_ctx else keys_ref[0, src:src + rows, :]
            s = _dot(keys, qz)
            s_new[dst:dst + rows, :] = s
            cmax = jnp.max(s, axis=0, keepdims=True)
            m_run = cmax if m_run is None else jnp.maximum(m_run, cmax)
            if s_old is not None:
                p = jnp.exp2(s_old[dst:dst + rows, :] - m_prev)
                psum = jnp.sum(p, axis=0, keepdims=True)
                vals = vct_ref[0, :, src:src + rows] if is_ctx else vt_ref[0, :, src:src + rows]
                part = _dot(vals, p.astype(BF16))
                denom = psum if denom is None else denom + psum
                acc = part if acc is None else acc + part
            if n == 0 and hook is not None:
                hook()
        m_new[...] = m_run
        return acc, denom

    def finish(acc, denom):
        r = 1.0 / denom
        ot = acc[:, :tq] * r[:, :tq] - acc[:, tq:] * (lam * r[:, tq:])
        return (_rms(ot.T, sg_ref[...]) * (1.0 - LAM_INIT)).astype(BF16)

    @pl.when(pl.program_id(0) == 0)
    def _():
        stage(qc_ref[0, :, :tq], k_ref, kc_ref, sa_ref, ma_ref, None, None)

    bufs = ((sa_ref, ma_ref), (sb_ref, mb_ref))
    pending = None
    for n in range(ATT_BLOCKS):
        if n + 1 < ATT_BLOCKS:
            new = (qc_ref[0, :, (n + 1) * tq:(n + 2) * tq], k_ref, kc_ref)
        else:
            new = (qn_ref[0, :, :tq], kn_ref, kcn_ref)
        raw = stage(*new, *bufs[(n + 1) % 2], *bufs[n % 2], hook=pending)

        def pending(raw=raw, n=n):
            o_ref[0, n * tq:(n + 1) * tq, :] = finish(*raw)

    pending()


def _attn(qt, k, vt, kc, vct, lamv, subln_g):
    b, l, _ = k.shape
    c = kc.shape[1]
    dv = 2 * HEAD_DIM
    tq = ATT_TQ
    tstep = ATT_BLOCKS * tq
    assert l % tstep == 0 and ATT_BLOCKS % 2 == 0
    pairs = l // tstep
    steps = b * N_HEADS * pairs

    def split(j):
        return j // (N_HEADS * pairs), (j // pairs) % N_HEADS, j % pairs

    def cur(fn):
        return lambda j: fn(*split(j))

    def nxt(fn):
        return lambda j: fn(*split(jnp.minimum(j + 1, steps - 1)))

    q_map = lambda bi, h, i: (bi, h, i)
    k_map = lambda bi, h, i: (bi, 0, h)
    v_map = lambda bi, h, i: (bi, h, 0)
    return pl.pallas_call(
        _attn_body,
        grid=(steps,),
        in_specs=[
            pl.BlockSpec((1, dv, tstep), cur(q_map)), pl.BlockSpec((1, dv, tstep), nxt(q_map)),
            pl.BlockSpec((1, l, dv), cur(k_map)), pl.BlockSpec((1, l, dv), nxt(k_map)),
            pl.BlockSpec((1, c, dv), cur(k_map)), pl.BlockSpec((1, c, dv), nxt(k_map)),
            pl.BlockSpec((1, dv, l), cur(v_map)), pl.BlockSpec((1, dv, c), cur(v_map)),
            _const_spec(lamv.shape), _const_spec(subln_g.shape),
        ],
        out_specs=pl.BlockSpec((1, tstep, dv), cur(lambda bi, h, i: (bi, i, h))),
        out_shape=jax.ShapeDtypeStruct((b, l, N_HEADS * dv), BF16),
        scratch_shapes=[pltpu.VMEM((l + c, 2 * tq), F32), pltpu.VMEM((l + c, 2 * tq), F32),
                        pltpu.VMEM((1, 2 * tq), F32), pltpu.VMEM((1, 2 * tq), F32)],
        compiler_params=_params("arbitrary"),
        name="attn",
    )(qt, qt, k, k, kc, kc, vt, vct, lamv, subln_g)


def _post_body(x_ref, zh_ref, oa_ref, gate_ref, g1_ref, sh2_ref, sc2_ref, g2_ref,
               gpost_ref, gfpre_ref, gfpost_ref,
               whu_ref, wau_ref, wo_ref, wg_ref, wu_ref, wd_ref, o_ref, *, ff_chunks):
    d = x_ref.shape[2]
    half = x_ref.shape[1] // 2
    rows_a, rows_b = slice(0, half), slice(half, 2 * half)

    def lifts(rows):
        return _dot(zh_ref[0, rows, :], whu_ref[...]), _dot(oa_ref[0, rows, :], wau_ref[...])

    def out_proj(rows, y_hy, y_att):
        mixed = gate_ref[0, rows, :d].astype(F32) * y_hy + gate_ref[0, rows, d:].astype(F32) * y_att
        return _dot(mixed.astype(BF16), wo_ref[...])

    def residual_and_norm(rows, mixed):
        x1 = x_ref[0, rows, :] + g1_ref[0] * _rms(mixed, gpost_ref[...])
        return x1, (_rms(x1, gfpre_ref[...]) * (1.0 + sc2_ref[0]) + sh2_ref[0]).astype(BF16)

    def ffn_chunk(hf, f, chunk):
        c0, cw = chunk
        a = _dot(hf, wg_ref[:, c0:c0 + cw])
        t = (a * jax.nn.sigmoid(a) * _dot(hf, wu_ref[:, c0:c0 + cw])).astype(BF16)
        part = _dot(t, wd_ref[c0:c0 + cw, :])
        return part if f is None else f + part

    def finish(rows, x1, f):
        o_ref[0, rows, :] = x1 + g2_ref[0] * _rms(f, gfpost_ref[...])

    mixed_a = out_proj(rows_a, *lifts(rows_a))
    lifted_b = lifts(rows_b)
    x1_a, hf_a = residual_and_norm(rows_a, mixed_a)
    mixed_b = out_proj(rows_b, *lifted_b)
    f_a = ffn_chunk(hf_a, None, ff_chunks[0])
    x1_b, hf_b = residual_and_norm(rows_b, mixed_b)
    for chunk in ff_chunks[1:]:
        f_a = ffn_chunk(hf_a, f_a, chunk)
    f_b = ffn_chunk(hf_b, None, ff_chunks[0])
    finish(rows_a, x1_a, f_a)
    for chunk in ff_chunks[1:]:
        f_b = ffn_chunk(hf_b, f_b, chunk)
    finish(rows_b, x1_b, f_b)


def _post(x, zh, oa, gate, g1, sh2, sc2, g2, gpost, gfpre, gfpost, whu, wau, wo, wg, wu, wd):
    b, l, d = x.shape
    dff = wg.shape[1]
    tm = 512
    ff_chunk = 4 * MXU_DIM
    ff_chunks = tuple((c0, min(ff_chunk, dff - c0)) for c0 in range(0, dff, ff_chunk))
    tile = lambda bi, i: (bi, i, 0)
    per_b = lambda bi, i: (bi, 0, 0)
    return pl.pallas_call(
        functools.partial(_post_body, ff_chunks=ff_chunks),
        grid=(b, l // tm),
        in_specs=[
            pl.BlockSpec((1, tm, d), tile),
            pl.BlockSpec((1, tm, zh.shape[2]), tile),
            pl.BlockSpec((1, tm, oa.shape[2]), tile),
            pl.BlockSpec((1, tm, gate.shape[2]), tile),
            pl.BlockSpec((1, 1, d), per_b), pl.BlockSpec((1, 1, d), per_b),
            pl.BlockSpec((1, 1, d), per_b), pl.BlockSpec((1, 1, d), per_b),
            _const_spec((1, d)), _const_spec((1, d)), _const_spec((1, d)),
        ] + [_const_spec(w.shape) for w in (whu, wau, wo, wg, wu, wd)],
        out_specs=pl.BlockSpec((1, tm, d), tile),
        out_shape=jax.ShapeDtypeStruct((b, l, d), F32),
        compiler_params=_params("arbitrary", "arbitrary"),
        name="post",
    )(x, zh, oa, gate, g1, sh2, sc2, g2, gpost, gfpre, gfpost, whu, wau, wo, wg, wu, wd)


@functools.lru_cache(maxsize=None)
def _rope_tables(l):
    quarter = HEAD_DIM // 4
    inv = ROPE_BASE ** (-np.arange(quarter, dtype=np.float64) / quarter)
    t = np.arange(l)
    ang_r = (t // GRID_W)[:, None] * inv[None, :]
    ang_c = (t % GRID_W)[:, None] * inv[None, :]
    ang = np.concatenate([ang_r, ang_r, ang_c, ang_c], axis=1)
    reps = D_ATTN // HEAD_DIM
    cos = np.tile(np.cos(ang), (1, reps)).astype(np.float32)
    second = (np.arange(D_ATTN) % (2 * quarter)) >= quarter
    sin = (np.tile(np.sin(ang), (1, reps)) * np.where(second, 1.0, -1.0)[None, :]).astype(np.float32)
    return cos, sin


@functools.lru_cache(maxsize=None)
def _filter_features(l, emb, kpad):
    bands = (emb - 1) // 2
    t = np.linspace(0.0, 1.0, l, dtype=np.float32).astype(np.float64)[:, None]
    w = (2.0 * math.pi / l) * np.arange(l, dtype=np.float64)[:, None]
    fr = np.linspace(1e-4, bands - 1, bands, dtype=np.float32).astype(np.float64)[None, :]
    z = np.concatenate([t, np.cos(fr * w), -np.sin(fr * w)], axis=-1)
    z = np.pad(z, ((0, 0), (0, kpad - emb))).astype(np.float32)
    deltas = np.abs(np.linspace(math.log(FILTER_TARGET) / SLOW_DECAY_PCT,
                                math.log(FILTER_TARGET) / FAST_DECAY_PCT, D_HYENA, dtype=np.float32))
    return z, deltas[None, :].astype(np.float32)


def _pad_to(a, rows, cols, value=0.0):
    return jnp.pad(a, ((0, rows - a.shape[0]), (0, cols - a.shape[1])), constant_values=value)


def kernel(x, c, ctx, c_ctx, w_ada, b_ada, g_mix_pre, g_mix_post, g_ffn_pre, g_ffn_post, w_in, hy_conv_w, hy_conv_b, hy_f_w1, hy_f_b1, hy_f_w2, hy_f_b2, hy_f_w3, hy_f_b3, hy_f_freq, hy_bias, lambda_q1, lambda_k1, lambda_q2, lambda_k2, att_subln_g, w_hy_up, w_att_up, w_out, w_ffn_gate, w_ffn_up, w_ffn_down):
    b, l, d = x.shape
    order = hy_bias.shape[1]
    off_q = (order + 1) * D_HYENA
    off_k = off_q + D_ATTN
    off_v = off_k + D_ATTN
    off_g = off_v + D_ATTN
    assert w_in.shape[0] == 1, "single-layer block"
    assert l % HY_ROWS == 0 and l % 512 == 0

    pad_rows = -(-(b + 1) // 8) * 8
    cc = jnp.concatenate([c, c_ctx[None, :], jnp.zeros((pad_rows - b - 1, d), F32)], axis=0)
    ada = _ada(cc, w_ada[0], b_ada[0][None, :])
    sh1, sc1, g1, sh2, sc2, g2 = (ada[:b, i * d:(i + 1) * d][:, None, :] for i in range(6))
    csh1, csc1 = ada[b:b + 1, 0:d], ada[b:b + 1, d:2 * d]

    row = lambda v: v[0][None, :]
    w = w_in[0]
    cos, sin = _rope_tables(l)
    q_scale = HEAD_DIM ** -0.5 * math.log2(math.e)
    wq = (w[:, off_q:off_k] * q_scale).astype(BF16)
    wk = w[:, off_k:off_v].astype(BF16)
    wv = w[:, off_v:off_g].astype(BF16)

    kc, vct, wqt, wvt = _ctx_kv(ctx, row(g_mix_pre), csh1, csc1, wk, wq, wv)

    u, qt, k, vt, gate = _proj(x, row(g_mix_pre), sh1, sc1, jnp.asarray(cos), jnp.asarray(sin),
                               jnp.asarray(cos.T), jnp.asarray(sin.T), hy_conv_w[0], row(hy_conv_b),
                               w[:, :off_q].astype(BF16), wqt, wk, wvt, w[:, off_g:].astype(BF16))

    consts = tuple(jnp.asarray(a).astype(BF16) for a in _fft_constants(l))
    zfeat, deltas = _filter_features(l, hy_f_w1.shape[1], LANES)
    hf = _filt(jnp.asarray(zfeat),
               _pad_to(hy_f_w1[0], LANES, LANES), _pad_to(row(hy_f_b1), 1, LANES),
               _pad_to(hy_f_w2[0], LANES, LANES), _pad_to(row(hy_f_b2), 1, LANES),
               _pad_to(row(hy_f_freq), 1, LANES, 1.0),
               _pad_to(hy_f_w3[0], LANES, hy_f_w3.shape[2]), row(hy_f_b3),
               jnp.asarray(deltas), consts[0], consts[3], order, l)

    zh = _hyena(u, hy_bias[0], hf, consts[:3], order)

    lamv = jnp.concatenate([lambda_q1, lambda_k1, lambda_q2, lambda_k2], axis=0)
    oa = _attn(qt, k, vt, kc, vct, lamv, row(att_subln_g))

    return _post(x, zh, oa, gate, g1, sh2, sc2, g2, row(g_mix_post), row(g_ffn_pre), row(g_ffn_post),
                 w_hy_up[0].astype(BF16), w_att_up[0].astype(BF16), w_out[0].astype(BF16),
                 w_ffn_gate[0].astype(BF16), w_ffn_up[0].astype(BF16), w_ffn_down[0].astype(BF16))
```

```python
import functools
import math

import numpy as np
import jax
import jax.numpy as jnp
from jax import lax
from jax.experimental import pallas as pl
from jax.experimental.pallas import tpu as pltpu

F32 = jnp.float32
BF16 = jnp.bfloat16

EPS = 1e-6
GRID_W = 64
HEAD_DIM = 64
N_HEADS = 4
D_HYENA = 512
D_ATTN = N_HEADS * 2 * HEAD_DIM
ROPE_BASE = 10000.0
FILTER_TARGET = 1e-2
FAST_DECAY_PCT = 0.3
SLOW_DECAY_PCT = 1.5
LAM_INIT = 0.8 - 0.6 * math.exp(0.0)

LANES = 128
MXU_DIM = 256
VMEM_LIMIT_BYTES = 56 * 1024 * 1024

FFT_N2 = 256
FFT_GROUP = 16
HY_CB = 256
FFT_GROUP_UNROLL = 16
FILT_GROUP_UNROLL = 8


def _dot(a, b):
    return jnp.dot(a, b, preferred_element_type=F32)


def _dot_x3(a, b):
    a_hi = a.astype(BF16)
    a_lo = (a - a_hi.astype(F32)).astype(BF16)
    b_hi = b.astype(BF16)
    b_lo = (b - b_hi.astype(F32)).astype(BF16)
    return _dot(a_hi, b_hi) + _dot(a_lo, b_hi) + _dot(a_hi, b_lo)


def _dot_nt(a, b):
    return lax.dot_general(a, b, (((1,), (1,)), ((), ())), preferred_element_type=F32)


def _rms(x, g):
    return x * lax.rsqrt(jnp.mean(x * x, axis=-1, keepdims=True) + EPS) * g


def _const_spec(shape):
    zeros = (0,) * len(shape)
    return pl.BlockSpec(shape, lambda *_: zeros, pipeline_mode=pl.Buffered(1))


def _params(*semantics):
    return pltpu.CompilerParams(dimension_semantics=semantics, vmem_limit_bytes=VMEM_LIMIT_BYTES)


def _ada_body(c_ref, w_ref, b_ref, o_ref):
    c = c_ref[...]
    s = c * jax.nn.sigmoid(c)
    w = w_ref[...]
    s_hi = s.astype(BF16)
    s_lo = (s - s_hi.astype(F32)).astype(BF16)
    w_hi = w.astype(BF16)
    w_lo = (w - w_hi.astype(F32)).astype(BF16)
    rows = s.shape[0]
    both = _dot(jnp.concatenate([s_hi, s_lo], axis=0), w_hi)
    o_ref[...] = both[:rows] + both[rows:] + _dot(s_hi, w_lo) + b_ref[...]


def _ada(cc, w_ada, b_ada):
    rows, d = cc.shape
    n = w_ada.shape[1]
    tn = 1024
    return pl.pallas_call(
        _ada_body,
        grid=(n // tn,),
        in_specs=[
            pl.BlockSpec((rows, d), lambda j: (0, 0)),
            pl.BlockSpec((d, tn), lambda j: (0, j)),
            pl.BlockSpec((1, tn), lambda j: (0, j)),
        ],
        out_specs=pl.BlockSpec((rows, tn), lambda j: (0, j)),
        out_shape=jax.ShapeDtypeStruct((rows, n), F32),
        compiler_params=_params("arbitrary"),
        name="ada",
    )(cc, w_ada, b_ada)


def _ctx_kv_body(x_ref, g_ref, sh_ref, sc_ref, wk_ref, wq_ref, wv_ref, k_ref, vt_ref, wqt_ref, wvt_ref):
    @pl.when(pl.program_id(0) == 0)
    def _():
        wqt_ref[...] = wq_ref[...].astype(F32).T.astype(BF16)
        wvt_ref[...] = wv_ref[...].astype(F32).T.astype(BF16)

    h = _rms(x_ref[0], g_ref[...]) * (1.0 + sc_ref[...]) + sh_ref[...]
    hb = h.astype(BF16)
    k_ref[0] = _dot(hb, wk_ref[...]).astype(BF16)
    vt_ref[0] = _dot_nt(wvt_ref[...], hb).astype(BF16)


def _ctx_kv(ctx, g, sh, sc, wk, wq, wv):
    b, n_ctx, d = ctx.shape
    n = wk.shape[1]
    bf = lambda shape: jax.ShapeDtypeStruct(shape, BF16)
    whole_t = pl.BlockSpec((n, d), lambda i: (0, 0))
    return pl.pallas_call(
        _ctx_kv_body,
        grid=(b,),
        in_specs=[
            pl.BlockSpec((1, n_ctx, d), lambda i: (i, 0, 0)),
            _const_spec((1, d)), _const_spec((1, d)), _const_spec((1, d)),
            _const_spec((d, n)), _const_spec((d, n)), _const_spec((d, n)),
        ],
        out_specs=[pl.BlockSpec((1, n_ctx, n), lambda i: (i, 0, 0)),
                   pl.BlockSpec((1, n, n_ctx), lambda i: (i, 0, 0)), whole_t, whole_t],
        out_shape=[bf((b, n_ctx, n)), bf((b, n, n_ctx)), bf((n, d)), bf((n, d))],
        compiler_params=_params("arbitrary"),
        name="ctx_kv",
    )(ctx, g, sh, sc, wk, wq, wv)


ROT = HEAD_DIM // 4
PROJ_HALO = 16


def _proj_body(x_ref, xp_ref, xn_ref, g_ref, sh_ref, sc_ref, cos_ref, sin_ref, cost_ref, sint_ref,
               cw_ref, cb_ref, wu_ref, wqt_ref, wk_ref, wvt_ref, wg_ref,
               u_ref, qt_ref, k_ref, vt_ref, gate_ref):
    tm = x_ref.shape[1]
    half = tm // 2
    rows_a, rows_b = slice(0, half), slice(half, 2 * half)

    def modulated(x):
        return (_rms(x, g_ref[...]) * (1.0 + sc_ref[0]) + sh_ref[0]).astype(BF16)

    def normed(rows):
        return modulated(x_ref[0, rows, :])

    def gates_and_keys(rows, hb):
        gate_ref[0, rows, :] = jax.nn.sigmoid(_dot(hb, wg_ref[...])).astype(BF16)
        kv = _dot(hb, wk_ref[...])
        lane = lax.broadcasted_iota(jnp.int32, (kv.shape[0], LANES), 1)
        first = (lane % (2 * ROT)) < ROT
        slabs = []
        for c0 in range(0, kv.shape[1], LANES):
            slab = kv[:, c0:c0 + LANES]
            slabs.append(jnp.where(first, pltpu.roll(slab, LANES - ROT, axis=1), pltpu.roll(slab, ROT, axis=1)))
        k_ref[0, rows, :] = (kv * cos_ref[rows, :]
                             + jnp.concatenate(slabs, axis=1) * sin_ref[rows, :]).astype(BF16)

    i = pl.program_id(0)
    hb_a = normed(rows_a)
    u_a = _dot(jnp.concatenate([modulated(xp_ref[0]), hb_a], axis=0), wu_ref[...])
    hb_b = normed(rows_b)
    gates_and_keys(rows_a, hb_a)
    u_b = _dot(jnp.concatenate([hb_b, modulated(xn_ref[0])], axis=0), wu_ref[...])
    keep_prev = (i > 0).astype(F32)
    keep_next = (i < pl.num_programs(0) - 1).astype(F32)

    def short_conv(cols):
        ext = jnp.concatenate([u_a[:PROJ_HALO, cols] * keep_prev, u_a[PROJ_HALO:, cols], u_b[:half, cols],
                               u_b[half:, cols] * keep_next], axis=0)
        n = ext.shape[0]
        cw = cw_ref[:, cols]
        prev = pltpu.roll(ext, 1, axis=0)[PROJ_HALO:PROJ_HALO + tm]
        nxt = pltpu.roll(ext, n - 1, axis=0)[PROJ_HALO:PROJ_HALO + tm]
        u_ref[0, :, cols] = (prev * cw[0:1] + ext[PROJ_HALO:PROJ_HALO + tm] * cw[1:2] + nxt * cw[2:3]
                             + cb_ref[:, cols]).astype(BF16)

    third = u_a.shape[1] // 3
    short_conv(slice(0, third))
    gates_and_keys(rows_b, hb_b)
    short_conv(slice(third, 2 * third))
    hb = jnp.concatenate([hb_a, hb_b], axis=0)
    qv = _dot_nt(wqt_ref[...], hb)
    partner = jnp.concatenate(
        [qv[r + ROT:r + 2 * ROT] if part == 0 else qv[r:r + ROT]
         for r in range(0, qv.shape[0], 2 * ROT) for part in (0, 1)], axis=0)
    qt_ref[0] = (qv * cost_ref[...] + partner * sint_ref[...]).astype(BF16)
    short_conv(slice(2 * third, 3 * third))
    vt_ref[0] = _dot_nt(wvt_ref[...], hb).astype(BF16)


def _proj(x, g, sh, sc, cos, sin, cost, sint, conv_w, conv_b, wu, wqt, wk, wvt, wg):
    b, l, d = x.shape
    tm = 512
    per_halo = tm // PROJ_HALO
    n_halo = l // PROJ_HALO
    tile = lambda i, bi: (bi, i, 0)
    tile_t = lambda i, bi: (bi, 0, i)
    per_b = lambda i, bi: (bi, 0, 0)
    before = lambda i, bi: (bi, jnp.maximum(i * per_halo - 1, 0), 0)
    after = lambda i, bi: (bi, jnp.minimum((i + 1) * per_halo, n_halo - 1), 0)
    bf = lambda shape: jax.ShapeDtypeStruct(shape, BF16)
    n_u, n_g = wu.shape[1], wg.shape[1]
    return pl.pallas_call(
        _proj_body,
        grid=(l // tm, b),
        in_specs=[
            pl.BlockSpec((1, tm, d), tile),
            pl.BlockSpec((1, PROJ_HALO, d), before), pl.BlockSpec((1, PROJ_HALO, d), after),
            _const_spec((1, d)),
            pl.BlockSpec((1, 1, d), per_b), pl.BlockSpec((1, 1, d), per_b),
            pl.BlockSpec((tm, D_ATTN), lambda i, bi: (i, 0)), pl.BlockSpec((tm, D_ATTN), lambda i, bi: (i, 0)),
            pl.BlockSpec((D_ATTN, tm), lambda i, bi: (0, i)), pl.BlockSpec((D_ATTN, tm), lambda i, bi: (0, i)),
        ] + [_const_spec(w.shape) for w in (conv_w, conv_b, wu, wqt, wk, wvt, wg)],
        out_specs=[pl.BlockSpec((1, tm, n_u), tile), pl.BlockSpec((1, D_ATTN, tm), tile_t),
                   pl.BlockSpec((1, tm, D_ATTN), tile), pl.BlockSpec((1, D_ATTN, tm), tile_t),
                   pl.BlockSpec((1, tm, n_g), tile)],
        out_shape=[bf((b, l, n_u)), bf((b, D_ATTN, l)), bf((b, l, D_ATTN)), bf((b, D_ATTN, l)), bf((b, l, n_g))],
        compiler_params=_params("arbitrary", "arbitrary"),
        name="proj",
    )(x, x, x, g, sh, sc, cos, sin, cost, sint, conv_w, conv_b, wu, wqt, wk, wvt, wg)


def _fft_dims(l):
    n = 2 * l
    n1 = n // FFT_N2
    h1 = n1 // 2
    k1 = h1 + 1
    ng = FFT_N2 // FFT_GROUP
    return n, n1, h1, k1, ng


@functools.lru_cache(maxsize=None)
def _fft_constants(l):
    n, n1, h1, k1n, ng = _fft_dims(l)
    g = FFT_GROUP
    ma = np.zeros((ng, k1n, 2, g, h1, g), np.float64)
    for gi in range(ng):
        for j in range(g):
            n2 = gi * g + j
            nn = FFT_N2 * np.arange(h1)[None, :] + n2
            theta = 2.0 * np.pi * ((np.arange(k1n)[:, None] * nn) % n) / n
            ma[gi, :, 0, j, :, j] = np.cos(theta)
            ma[gi, :, 1, j, :, j] = -np.sin(theta)
    weight = np.full((k1n,), 2.0)
    weight[0] = weight[-1] = 1.0
    ma_fwd = ma.reshape(ng, k1n * 2 * g, h1 * g)
    ma_inv = (ma * (weight / n)[None, :, None, None, None, None]).reshape(ng, k1n * 2 * g, h1 * g)
    ma_inv = np.transpose(ma_inv, (0, 2, 1))
    kpad = -(-ma_inv.shape[2] // LANES) * LANES
    ma_inv = np.pad(ma_inv, ((0, 0), (0, 0), (0, kpad - ma_inv.shape[2])))
    ang = 2.0 * np.pi * ((np.arange(FFT_N2)[:, None] * np.arange(FFT_N2)[None, :]) % FFT_N2) / FFT_N2
    fr, fi = np.cos(ang), -np.sin(ang)
    dft3 = np.stack([fr, fi, fr + fi, fr - fi])
    mb_filt = np.block([[fr, -fi, fr, -fi], [fi, fr, -fi, -fr]])
    return tuple(np.asarray(a, np.float32) for a in (ma_fwd, ma_inv, dft3, mb_filt))


def _fft_stage_a(src_ref, ma_ref, spec_ref, l, unroll=FFT_GROUP_UNROLL):
    _, _, h1, k1n, ng = _fft_dims(l)
    g = FFT_GROUP

    def group(gi, carry):
        off = gi * g
        xg = jnp.concatenate(
            [src_ref[pl.ds(pl.multiple_of(FFT_N2 * i + off, g), g), :] for i in range(h1)], axis=0)
        bg = _dot(ma_ref[gi], xg.astype(BF16)).astype(BF16)
        for c in range(2 * k1n):
            spec_ref[pl.ds(pl.multiple_of(c * FFT_N2 + off, g), g), :] = bg[c * g:(c + 1) * g]
        return carry

    lax.fori_loop(0, ng, group, 0, unroll=unroll)


def _fft_stage_a_inv(spec_ref, ma_inv_ref, emit, l):
    _, _, h1, k1n, ng = _fft_dims(l)
    g = FFT_GROUP
    kpad = ma_inv_ref.shape[2] - 2 * k1n * g

    def group(gi, carry):
        off = gi * g
        chunks = [spec_ref[pl.ds(pl.multiple_of(c * FFT_N2 + off, g), g), :] for c in range(2 * k1n)]
        if kpad:
            chunks.append(jnp.zeros((kpad, spec_ref.shape[1]), BF16))
        yg = _dot(ma_inv_ref[gi], jnp.concatenate(chunks, axis=0))
        for i in range(h1):
            emit(pl.ds(pl.multiple_of(FFT_N2 * i + off, g), g), yg[i * g:(i + 1) * g])
        return carry

    lax.fori_loop(0, ng, group, 0, unroll=FFT_GROUP_UNROLL)


def _filt_body(z_ref, w1_ref, b1_ref, w2_ref, b2_ref, fq_ref, w3f_ref, b3f_ref, w3b_ref, b3b_ref,
               dl_ref, ma_ref, mb_ref, hf_ref, hdn_ref, sig_ref, specf_ref, specb_ref, *, l):
    _, _, _, k1n, _ = _fft_dims(l)

    @pl.when((pl.program_id(0) == 0) & (pl.program_id(1) == 0))
    def _():
        fq = fq_ref[...]
        h1 = jnp.sin(fq * (_dot_x3(z_ref[...], w1_ref[...]) + b1_ref[...]))
        hdn_ref[...] = jnp.sin(fq * (_dot_x3(h1, w2_ref[...]) + b2_ref[...])).astype(BF16)

    hdn = hdn_ref[...]
    cb = specf_ref.shape[1]
    blk = 2 * FFT_N2
    rows = lax.broadcasted_iota(jnp.int32, (l, cb), 0)
    window = jnp.exp(rows.astype(F32) * (-1.0 / (l - 1)) * dl_ref[...])

    sig_ref[...] = (_dot(hdn, w3f_ref[...].astype(BF16)) + b3f_ref[...]) * window
    _fft_stage_a(sig_ref, ma_ref, specf_ref, l, unroll=FILT_GROUP_UNROLL)
    sig_ref[...] = jnp.where(rows > 0, (_dot(hdn, w3b_ref[...].astype(BF16)) + b3b_ref[...]) * window, 0.0)
    _fft_stage_a(sig_ref, ma_ref, specb_ref, l, unroll=FILT_GROUP_UNROLL)

    def blocks(starts):
        x = jnp.concatenate(
            [jnp.concatenate([specf_ref[pl.ds(st, blk), :], specb_ref[pl.ds(st, blk), :]], axis=0)
             for st in starts], axis=1)
        s = _dot(mb_ref[...], x).astype(BF16)
        for n, st in enumerate(starts):
            hf_ref[0, pl.ds(st, blk), :] = s[:, n * cb:(n + 1) * cb]

    def per_pair(i, carry):
        st = pl.multiple_of(i * 2 * blk, blk)
        blocks([st, st + blk])
        return carry

    lax.fori_loop(0, k1n // 2, per_pair, 0)
    if k1n % 2:
        blocks([(k1n - 1) * blk])


def _filt(zfeat, w1, b1, w2, b2, fq, w3, b3, deltas, ma_fwd, mb_filt, order, l):
    _, _, _, k1n, _ = _fft_dims(l)
    cb = HY_CB
    nblk = D_HYENA // cb
    srows = k1n * 2 * FFT_N2
    fcol = lambda o, c: (0, (2 * o) * nblk + c)
    bcol = lambda o, c: (0, (2 * o + 1) * nblk + c)
    return pl.pallas_call(
        functools.partial(_filt_body, l=l),
        grid=(order, nblk),
        in_specs=[
            _const_spec(zfeat.shape), _const_spec(w1.shape), _const_spec(b1.shape),
            _const_spec(w2.shape), _const_spec(b2.shape), _const_spec(fq.shape),
            pl.BlockSpec((w3.shape[0], cb), fcol), pl.BlockSpec((1, cb), fcol),
            pl.BlockSpec((w3.shape[0], cb), bcol), pl.BlockSpec((1, cb), bcol),
            pl.BlockSpec((1, cb), lambda o, c: (0, c)),
            _const_spec(ma_fwd.shape), _const_spec(mb_filt.shape),
        ],
        out_specs=pl.BlockSpec((1, srows, cb), lambda o, c: (o, 0, c)),
        out_shape=jax.ShapeDtypeStruct((order, srows, D_HYENA), BF16),
        scratch_shapes=[
            pltpu.VMEM((l, w2.shape[1]), BF16),
            pltpu.VMEM((l, cb), F32),
            pltpu.VMEM((srows, cb), BF16),
            pltpu.VMEM((srows, cb), BF16),
        ],
        compiler_params=_params("arbitrary", "arbitrary"),
        name="filt",
    )(zfeat, w1, b1, w2, b2, fq, w3, b3, w3, b3, deltas, ma_fwd, mb_filt)


HY_ROWS = 512


def _hyena_body(u0_ref, u1_ref, u2_ref, db_ref, hf_ref,
                ma_ref, mai_ref, dft_ref, o_ref, z_ref, y_ref, spec_ref, *, l, order):
    _, _, _, k1n, _ = _fft_dims(l)
    blk = 2 * FFT_N2
    gates = (u1_ref, u2_ref)

    for r0 in range(0, l, HY_ROWS):
        z_ref[r0:r0 + HY_ROWS, :] = u0_ref[0, r0:r0 + HY_ROWS, :].astype(F32)

    for o in range(order):
        _fft_stage_a(z_ref, ma_ref, spec_ref, l)

        cb = spec_ref.shape[1]
        groups = [list(range(i, min(i + 2, k1n))) for i in range(0, k1n, 2)]

        def stage_b(ks):
            br = jnp.concatenate([spec_ref[k * blk:k * blk + FFT_N2, :] for k in ks], axis=1)
            bi = jnp.concatenate([spec_ref[k * blk + FFT_N2:(k + 1) * blk, :] for k in ks], axis=1)
            k1 = _dot(dft_ref[2], br)
            return k1 - _dot(dft_ref[1], br + bi), k1 + _dot(dft_ref[0], bi - br)

        def filter_and_invert(ks, s, o=o):
            sr, si = s
            hr = jnp.concatenate([hf_ref[o, k * blk:k * blk + FFT_N2, :] for k in ks], axis=1).astype(F32)
            hi = jnp.concatenate([hf_ref[o, k * blk + FFT_N2:(k + 1) * blk, :] for k in ks], axis=1).astype(F32)
            yr = sr * hr - si * hi
            yi = sr * hi + si * hr
            k1 = _dot(dft_ref[3], yr.astype(BF16))
            cr = (k1 + _dot(dft_ref[1], (yr + yi).astype(BF16))).astype(BF16)
            ci = (k1 + _dot(dft_ref[0], (yi - yr).astype(BF16))).astype(BF16)
            for n, k in enumerate(ks):
                spec_ref[k * blk:k * blk + FFT_N2, :] = cr[:, n * cb:(n + 1) * cb]
                spec_ref[k * blk + FFT_N2:(k + 1) * blk, :] = ci[:, n * cb:(n + 1) * cb]

        s_next = stage_b(groups[0])
        for gi, ks in enumerate(groups):
            s_cur = s_next
            if gi + 1 < len(groups):
                s_next = stage_b(groups[gi + 1])
            filter_and_invert(ks, s_cur)

        def store_y(rows, y):
            y_ref[rows, :] = y

        _fft_stage_a_inv(spec_ref, mai_ref, store_y, l)

        dbias = db_ref[o:o + 1, :]
        for r0 in range(0, l, HY_ROWS):
            rows = slice(r0, r0 + HY_ROWS)
            znew = gates[o][0, rows, :].astype(F32) * (y_ref[rows, :] + z_ref[rows, :] * dbias)
            if o + 1 < order:
                z_ref[rows, :] = znew
            else:
                o_ref[0, rows, :] = znew.astype(BF16)


def _hyena(u, dbias, hf, consts, order):
    b, l, _ = u.shape
    _, _, _, k1n, _ = _fft_dims(l)
    cb = HY_CB
    nblk = D_HYENA // cb
    srows = k1n * 2 * FFT_N2
    ma_fwd, ma_inv, dft3 = consts
    upart = lambda p: pl.BlockSpec((1, l, cb), lambda c, bi: (bi, 0, p * nblk + c))
    return pl.pallas_call(
        functools.partial(_hyena_body, l=l, order=order),
        grid=(nblk, b),
        in_specs=[
            upart(0), upart(1), upart(2),
            pl.BlockSpec((order, cb), lambda c, bi: (0, c)),
            pl.BlockSpec((order, srows, cb), lambda c, bi: (0, 0, c), pipeline_mode=pl.Buffered(1)),
            _const_spec(ma_fwd.shape), _const_spec(ma_inv.shape),
            _const_spec(dft3.shape),
        ],
        out_specs=pl.BlockSpec((1, l, cb), lambda c, bi: (bi, 0, c)),
        out_shape=jax.ShapeDtypeStruct((b, l, D_HYENA), BF16),
        scratch_shapes=[
            pltpu.VMEM((l, cb), F32),
            pltpu.VMEM((l, cb), F32),
            pltpu.VMEM((srows, cb), BF16),
        ],
        compiler_params=_params("arbitrary", "arbitrary"),
        name="hyena",
    )(u, u, u, dbias, hf, ma_fwd, ma_inv, dft3)


ATT_TQ = MXU_DIM
ATT_KEY_CHUNK = 2 * MXU_DIM
ATT_BLOCKS = 8


def _attn_body(qc_ref, qn_ref, k_ref, kn_ref, kc_ref, kcn_ref, vt_ref, vct_ref, lam_ref, sg_ref, o_ref,
               sa_ref, sb_ref, ma_ref, mb_ref):
    tq = ATT_TQ
    lat = k_ref.shape[1]
    lv = lam_ref[...]
    lam = (jnp.exp(jnp.sum(lv[0:1] * lv[1:2], axis=-1, keepdims=True))
           - jnp.exp(jnp.sum(lv[2:3] * lv[3:4], axis=-1, keepdims=True)) + LAM_INIT)

    n_ctx = kc_ref.shape[1]
    chunks = [(False, r, r, ATT_KEY_CHUNK) for r in range(0, lat, ATT_KEY_CHUNK)] + [(True, 0, lat, n_ctx)]

    def stage(qt, keys_ref, ctx_keys_ref, s_new, m_new, s_old, m_old, hook=None):
        m_run = acc = denom = None
        feat = lax.broadcasted_iota(jnp.int32, qt.shape, 0)
        qz = jnp.concatenate([jnp.where(feat < HEAD_DIM, qt, jnp.zeros_like(qt)),
                              jnp.where(feat >= HEAD_DIM, qt, jnp.zeros_like(qt))], axis=1)
        if s_old is not None:
            m_prev = m_old[...]
        for n, (is_ctx, src, dst, rows) in enumerate(chunks):
            keys = ctx_keys_ref[0, src:src + rows, :] if is_ctx else keys_ref[0, src:src + rows, :]
            s = _dot(keys, qz)
            s_new[dst:dst + rows, :] = s
            cmax = jnp.max(s, axis=0, keepdims=True)
            m_run = cmax if m_run is None else jnp.maximum(m_run, cmax)
            if s_old is not None:
                p = jnp.exp2(s_old[dst:dst + rows, :] - m_prev)
                psum = jnp.sum(p, axis=0, keepdims=True)
                vals = vct_ref[0, :, src:src + rows] if is_ctx else vt_ref[0, :, src:src + rows]
                part = _dot(vals, p.astype(BF16))
                denom = psum if denom is None else denom + psum
                acc = part if acc is None else acc + part
            if n == 0 and hook is not None:
                hook()
        m_new[...] = m_run
        return acc, denom

    def finish(acc, denom):
        r = 1.0 / denom
        ot = acc[:, :tq] * r[:, :tq] - acc[:, tq:] * (lam * r[:, tq:])
        return (_rms(ot.T, sg_ref[...]) * (1.0 - LAM_INIT)).astype(BF16)

    @pl.when(pl.program_id(0) == 0)
    def _():
        stage(qc_ref[0, :, :tq], k_ref, kc_ref, sa_ref, ma_ref, None, None)

    bufs = ((sa_ref, ma_ref), (sb_ref, mb_ref))
    pending = None
    for n in range(ATT_BLOCKS):
        if n + 1 < ATT_BLOCKS:
            new = (qc_ref[0, :, (n + 1) * tq:(n + 2) * tq], k_ref, kc_ref)
        else:
            new = (qn_ref[0, :, :tq], kn_ref, kcn_ref)
        raw = stage(*new, *bufs[(n + 1) % 2], *bufs[n % 2], hook=pending)

        def pending(raw=raw, n=n):
            o_ref[0, n * tq:(n + 1) * tq, :] = finish(*raw)

    pending()


def _attn(qt, k, vt, kc, vct, lamv, subln_g):
    b, l, _ = k.shape
    c = kc.shape[1]
    dv = 2 * HEAD_DIM
    tq = ATT_TQ
    tstep = ATT_BLOCKS * tq
    assert l % tstep == 0 and ATT_BLOCKS % 2 == 0
    pairs = l // tstep
    steps = b * N_HEADS * pairs

    def split(j):
        return j // (N_HEADS * pairs), (j // pairs) % N_HEADS, j % pairs

    def cur(fn):
        return lambda j: fn(*split(j))

    def nxt(fn):
        return lambda j: fn(*split(jnp.minimum(j + 1, steps - 1)))

    q_map = lambda bi, h, i: (bi, h, i)
    k_map = lambda bi, h, i: (bi, 0, h)
    v_map = lambda bi, h, i: (bi, h, 0)
    return pl.pallas_call(
        _attn_body,
        grid=(steps,),
        in_specs=[
            pl.BlockSpec((1, dv, tstep), cur(q_map)), pl.BlockSpec((1, dv, tstep), nxt(q_map)),
            pl.BlockSpec((1, l, dv), cur(k_map)), pl.BlockSpec((1, l, dv), nxt(k_map)),
            pl.BlockSpec((1, c, dv), cur(k_map)), pl.BlockSpec((1, c, dv), nxt(k_map)),
            pl.BlockSpec((1, dv, l), cur(v_map)), pl.BlockSpec((1, dv, c), cur(v_map)),
            _const_spec(lamv.shape), _const_spec(subln_g.shape),
        ],
        out_specs=pl.BlockSpec((1, tstep, dv), cur(lambda bi, h, i: (bi, i, h))),
        out_shape=jax.ShapeDtypeStruct((b, l, N_HEADS * dv), BF16),
        scratch_shapes=[pltpu.VMEM((l + c, 2 * tq), F32), pltpu.VMEM((l + c, 2 * tq), F32),
                        pltpu.VMEM((1, 2 * tq), F32), pltpu.VMEM((1, 2 * tq), F32)],
        compiler_params=_params("arbitrary"),
        name="attn",
    )(qt, qt, k, k, kc, kc, vt, vct, lamv, subln_g)


def _post_body(x_ref, zh_ref, oa_ref, gate_ref, g1_ref, sh2_ref, sc2_ref, g2_ref,
               gpost_ref, gfpre_ref, gfpost_ref,
               whu_ref, wau_ref, wo_ref, wg_ref, wu_ref, wd_ref, o_ref, *, ff_chunks):
    d = x_ref.shape[2]
    half = x_ref.shape[1] // 2
    rows_a, rows_b = slice(0, half), slice(half, 2 * half)

    def lifts(rows):
        return _dot(zh_ref[0, rows, :], whu_ref[...]), _dot(oa_ref[0, rows, :], wau_ref[...])

    def out_proj(rows, y_hy, y_att):
        mixed = gate_ref[0, rows, :d].astype(F32) * y_hy + gate_ref[0, rows, d:].astype(F32) * y_att
        return _dot(mixed.astype(BF16), wo_ref[...])

    def residual_and_norm(rows, mixed):
        x1 = x_ref[0, rows, :] + g1_ref[0] * _rms(mixed, gpost_ref[...])
        return x1, (_rms(x1, gfpre_ref[...]) * (1.0 + sc2_ref[0]) + sh2_ref[0]).astype(BF16)

    def ffn_chunk(hf, f, chunk):
        c0, cw = chunk
        a = _dot(hf, wg_ref[:, c0:c0 + cw])
        t = (a * jax.nn.sigmoid(a) * _dot(hf, wu_ref[:, c0:c0 + cw])).astype(BF16)
        part = _dot(t, wd_ref[c0:c0 + cw, :])
        return part if f is None else f + part

    def finish(rows, x1, f):
        o_ref[0, rows, :] = x1 + g2_ref[0] * _rms(f, gfpost_ref[...])

    mixed_a = out_proj(rows_a, *lifts(rows_a))
    lifted_b = lifts(rows_b)
    x1_a, hf_a = residual_and_norm(rows_a, mixed_a)
    mixed_b = out_proj(rows_b, *lifted_b)
    f_a = ffn_chunk(hf_a, None, ff_chunks[0])
    x1_b, hf_b = residual_and_norm(rows_b, mixed_b)
    for chunk in ff_chunks[1:]:
        f_a = ffn_chunk(hf_a, f_a, chunk)
    f_b = ffn_chunk(hf_b, None, ff_chunks[0])
    finish(rows_a, x1_a, f_a)
    for chunk in ff_chunks[1:]:
        f_b = ffn_chunk(hf_b, f_b, chunk)
    finish(rows_b, x1_b, f_b)


def _post(x, zh, oa, gate, g1, sh2, sc2, g2, gpost, gfpre, gfpost, whu, wau, wo, wg, wu, wd):
    b, l, d = x.shape
    dff = wg.shape[1]
    tm = 512
    ff_chunk = 4 * MXU_DIM
    ff_chunks = tuple((c0, min(ff_chunk, dff - c0)) for c0 in range(0, dff, ff_chunk))
    tile = lambda bi, i: (bi, i, 0)
    per_b = lambda bi, i: (bi, 0, 0)
    return pl.pallas_call(
        functools.partial(_post_body, ff_chunks=ff_chunks),
        grid=(b, l // tm),
        in_specs=[
            pl.BlockSpec((1, tm, d), tile),
            pl.BlockSpec((1, tm, zh.shape[2]), tile),
            pl.BlockSpec((1, tm, oa.shape[2]), tile),
            pl.BlockSpec((1, tm, gate.shape[2]), tile),
            pl.BlockSpec((1, 1, d), per_b), pl.BlockSpec((1, 1, d), per_b),
            pl.BlockSpec((1, 1, d), per_b), pl.BlockSpec((1, 1, d), per_b),
            _const_spec((1, d)), _const_spec((1, d)), _const_spec((1, d)),
        ] + [_const_spec(w.shape) for w in (whu, wau, wo, wg, wu, wd)],
        out_specs=pl.BlockSpec((1, tm, d), tile),
        out_shape=jax.ShapeDtypeStruct((b, l, d), F32),
        compiler_params=_params("arbitrary", "arbitrary"),
        name="post",
    )(x, zh, oa, gate, g1, sh2, sc2, g2, gpost, gfpre, gfpost, whu, wau, wo, wg, wu, wd)


@functools.lru_cache(maxsize=None)
def _rope_tables(l):
    quarter = HEAD_DIM // 4
    inv = ROPE_BASE ** (-np.arange(quarter, dtype=np.float64) / quarter)
    t = np.arange(l)
    ang_r = (t // GRID_W)[:, None] * inv[None, :]
    ang_c = (t % GRID_W)[:, None] * inv[None, :]
    ang = np.concatenate([ang_r, ang_r, ang_c, ang_c], axis=1)
    reps = D_ATTN // HEAD_DIM
    cos = np.tile(np.cos(ang), (1, reps)).astype(np.float32)
    second = (np.arange(D_ATTN) % (2 * quarter)) >= quarter
    sin = (np.tile(np.sin(ang), (1, reps)) * np.where(second, 1.0, -1.0)[None, :]).astype(np.float32)
    return cos, sin


@functools.lru_cache(maxsize=None)
def _filter_features(l, emb, kpad):
    bands = (emb - 1) // 2
    t = np.linspace(0.0, 1.0, l, dtype=np.float32).astype(np.float64)[:, None]
    w = (2.0 * math.pi / l) * np.arange(l, dtype=np.float64)[:, None]
    fr = np.linspace(1e-4, bands - 1, bands, dtype=np.float32).astype(np.float64)[None, :]
    z = np.concatenate([t, np.cos(fr * w), -np.sin(fr * w)], axis=-1)
    z = np.pad(z, ((0, 0), (0, kpad - emb))).astype(np.float32)
    deltas = np.abs(np.linspace(math.log(FILTER_TARGET) / SLOW_DECAY_PCT,
                                math.log(FILTER_TARGET) / FAST_DECAY_PCT, D_HYENA, dtype=np.float32))
    return z, deltas[None, :].astype(np.float32)


def _pad_to(a, rows, cols, value=0.0):
    return jnp.pad(a, ((0, rows - a.shape[0]), (0, cols - a.shape[1])), constant_values=value)


def kernel(x, c, ctx, c_ctx, w_ada, b_ada, g_mix_pre, g_mix_post, g_ffn_pre, g_ffn_post, w_in, hy_conv_w, hy_conv_b, hy_f_w1, hy_f_b1, hy_f_w2, hy_f_b2, hy_f_w3, hy_f_b3, hy_f_freq, hy_bias, lambda_q1, lambda_k1, lambda_q2, lambda_k2, att_subln_g, w_hy_up, w_att_up, w_out, w_ffn_gate, w_ffn_up, w_ffn_down):
    b, l, d = x.shape
    order = hy_bias.shape[1]
    off_q = (order + 1) * D_HYENA
    off_k = off_q + D_ATTN
    off_v = off_k + D_ATTN
    off_g = off_v + D_ATTN
    assert w_in.shape[0] == 1, "single-layer block"
    assert l % HY_ROWS == 0 and l % 512 == 0

    pad_rows = -(-(b + 1) // 8) * 8
    cc = jnp.concatenate([c, c_ctx[None, :], jnp.zeros((pad_rows - b - 1, d), F32)], axis=0)
    ada = _ada(cc, w_ada[0], b_ada[0][None, :])
    sh1, sc1, g1, sh2, sc2, g2 = (ada[:b, i * d:(i + 1) * d][:, None, :] for i in range(6))
    csh1, csc1 = ada[b:b + 1, 0:d], ada[b:b + 1, d:2 * d]

    row = lambda v: v[0][None, :]
    w = w_in[0]
    cos, sin = _rope_tables(l)
    q_scale = HEAD_DIM ** -0.5 * math.log2(math.e)
    wq = (w[:, off_q:off_k] * q_scale).astype(BF16)
    wk = w[:, off_k:off_v].astype(BF16)
    wv = w[:, off_v:off_g].astype(BF16)

    kc, vct, wqt, wvt = _ctx_kv(ctx, row(g_mix_pre), csh1, csc1, wk, wq, wv)

    u, qt, k, vt, gate = _proj(x, row(g_mix_pre), sh1, sc1, jnp.asarray(cos), jnp.asarray(sin),
                               jnp.asarray(cos.T), jnp.asarray(sin.T), hy_conv_w[0], row(hy_conv_b),
                               w[:, :off_q].astype(BF16), wqt, wk, wvt, w[:, off_g:].astype(BF16))

    consts = tuple(jnp.asarray(a).astype(BF16) for a in _fft_constants(l))
    zfeat, deltas = _filter_features(l, hy_f_w1.shape[1], LANES)
    hf = _filt(jnp.asarray(zfeat),
               _pad_to(hy_f_w1[0], LANES, LANES), _pad_to(row(hy_f_b1), 1, LANES),
               _pad_to(hy_f_w2[0], LANES, LANES), _pad_to(row(hy_f_b2), 1, LANES),
               _pad_to(row(hy_f_freq), 1, LANES, 1.0),
               _pad_to(hy_f_w3[0], LANES, hy_f_w3.shape[2]), row(hy_f_b3),
               jnp.asarray(deltas), consts[0], consts[3], order, l)

    zh = _hyena(u, hy_bias[0], hf, consts[:3], order)

    lamv = jnp.concatenate([lambda_q1, lambda_k1, lambda_q2, lambda_k2], axis=0)
    oa = _attn(qt, k, vt, kc, vct, lamv, row(att_subln_g))

    return _post(x, zh, oa, gate, g1, sh2, sc2, g2, row(g_mix_post), row(g_ffn_pre), row(g_ffn_post),
                 w_hy_up[0].astype(BF16), w_att_up[0].astype(BF16), w_out[0].astype(BF16),
                 w_ffn_gate[0].astype(BF16), w_ffn_up[0].astype(BF16), w_ffn_down[0].astype(BF16))
```

```python
import functools
import math

import numpy as np
import jax
import jax.numpy as jnp
from jax import lax
from jax.experimental import pallas as pl
from jax.experimental.pallas import tpu as pltpu

F32 = jnp.float32
BF16 = jnp.bfloat16

EPS = 1e-6
GRID_W = 64
HEAD_DIM = 64
N_HEADS = 4
D_HYENA = 512
D_ATTN = N_HEADS * 2 * HEAD_DIM
ROPE_BASE = 10000.0
FILTER_TARGET = 1e-2
FAST_DECAY_PCT = 0.3
SLOW_DECAY_PCT = 1.5
LAM_INIT = 0.8 - 0.6 * math.exp(0.0)

LANES = 128
MXU_DIM = 256
VMEM_LIMIT_BYTES = 56 * 1024 * 1024

FFT_N2 = 256
FFT_GROUP = 16
HY_CB = 256
FFT_GROUP_UNROLL = 16


def _dot(a, b):
    return jnp.dot(a, b, preferred_element_type=F32)


def _dot_x3(a, b):
    a_hi = a.astype(BF16)
    a_lo = (a - a_hi.astype(F32)).astype(BF16)
    b_hi = b.astype(BF16)
    b_lo = (b - b_hi.astype(F32)).astype(BF16)
    return _dot(a_hi, b_hi) + _dot(a_lo, b_hi) + _dot(a_hi, b_lo)


def _dot_nt(a, b):
    return lax.dot_general(a, b, (((1,), (1,)), ((), ())), preferred_element_type=F32)


def _rms(x, g):
    return x * lax.rsqrt(jnp.mean(x * x, axis=-1, keepdims=True) + EPS) * g


def _const_spec(shape):
    zeros = (0,) * len(shape)
    return pl.BlockSpec(shape, lambda *_: zeros, pipeline_mode=pl.Buffered(1))


def _params(*semantics):
    return pltpu.CompilerParams(dimension_semantics=semantics, vmem_limit_bytes=VMEM_LIMIT_BYTES)


def _ada_body(c_ref, w_ref, b_ref, o_ref):
    c = c_ref[...]
    s = c * jax.nn.sigmoid(c)
    w = w_ref[...]
    s_hi = s.astype(BF16)
    s_lo = (s - s_hi.astype(F32)).astype(BF16)
    w_hi = w.astype(BF16)
    w_lo = (w - w_hi.astype(F32)).astype(BF16)
    rows = s.shape[0]
    both = _dot(jnp.concatenate([s_hi, s_lo], axis=0), w_hi)
    o_ref[...] = both[:rows] + both[rows:] + _dot(s_hi, w_lo) + b_ref[...]


def _ada(cc, w_ada, b_ada):
    rows, d = cc.shape
    n = w_ada.shape[1]
    tn = 1024
    return pl.pallas_call(
        _ada_body,
        grid=(n // tn,),
        in_specs=[
            pl.BlockSpec((rows, d), lambda j: (0, 0)),
            pl.BlockSpec((d, tn), lambda j: (0, j)),
            pl.BlockSpec((1, tn), lambda j: (0, j)),
        ],
        out_specs=pl.BlockSpec((rows, tn), lambda j: (0, j)),
        out_shape=jax.ShapeDtypeStruct((rows, n), F32),
        compiler_params=_params("arbitrary"),
        name="ada",
    )(cc, w_ada, b_ada)


def _ctx_kv_body(x_ref, g_ref, sh_ref, sc_ref, wk_ref, wq_ref, wv_ref, k_ref, vt_ref, wqt_ref, wvt_ref):
    @pl.when(pl.program_id(0) == 0)
    def _():
        wqt_ref[...] = wq_ref[...].astype(F32).T.astype(BF16)
        wvt_ref[...] = wv_ref[...].astype(F32).T.astype(BF16)

    h = _rms(x_ref[0], g_ref[...]) * (1.0 + sc_ref[...]) + sh_ref[...]
    hb = h.astype(BF16)
    k_ref[0] = _dot(hb, wk_ref[...]).astype(BF16)
    vt_ref[0] = _dot_nt(wvt_ref[...], hb).astype(BF16)


def _ctx_kv(ctx, g, sh, sc, wk, wq, wv):
    b, n_ctx, d = ctx.shape
    n = wk.shape[1]
    bf = lambda shape: jax.ShapeDtypeStruct(shape, BF16)
    whole_t = pl.BlockSpec((n, d), lambda i: (0, 0))
    return pl.pallas_call(
        _ctx_kv_body,
        grid=(b,),
        in_specs=[
            pl.BlockSpec((1, n_ctx, d), lambda i: (i, 0, 0)),
            _const_spec((1, d)), _const_spec((1, d)), _const_spec((1, d)),
            _const_spec((d, n)), _const_spec((d, n)), _const_spec((d, n)),
        ],
        out_specs=[pl.BlockSpec((1, n_ctx, n), lambda i: (i, 0, 0)),
                   pl.BlockSpec((1, n, n_ctx), lambda i: (i, 0, 0)), whole_t, whole_t],
        out_shape=[bf((b, n_ctx, n)), bf((b, n, n_ctx)), bf((n, d)), bf((n, d))],
        compiler_params=_params("arbitrary"),
        name="ctx_kv",
    )(ctx, g, sh, sc, wk, wq, wv)


ROT = HEAD_DIM // 4
PROJ_HALO = 16


def _proj_body(x_ref, xp_ref, xn_ref, g_ref, sh_ref, sc_ref, cos_ref, sin_ref, cost_ref, sint_ref,
               cw_ref, cb_ref, wu_ref, wqt_ref, wk_ref, wvt_ref, wg_ref,
               u_ref, qt_ref, k_ref, vt_ref, gate_ref):
    tm = x_ref.shape[1]
    half = tm // 2
    rows_a, rows_b = slice(0, half), slice(half, 2 * half)

    def modulated(x):
        return (_rms(x, g_ref[...]) * (1.0 + sc_ref[0]) + sh_ref[0]).astype(BF16)

    def normed(rows):
        return modulated(x_ref[0, rows, :])

    def gates_and_keys(rows, hb):
        gate_ref[0, rows, :] = jax.nn.sigmoid(_dot(hb, wg_ref[...])).astype(BF16)
        kv = _dot(hb, wk_ref[...])
        lane = lax.broadcasted_iota(jnp.int32, (kv.shape[0], LANES), 1)
        first = (lane % (2 * ROT)) < ROT
        slabs = []
        for c0 in range(0, kv.shape[1], LANES):
            slab = kv[:, c0:c0 + LANES]
            slabs.append(jnp.where(first, pltpu.roll(slab, LANES - ROT, axis=1), pltpu.roll(slab, ROT, axis=1)))
        k_ref[0, rows, :] = (kv * cos_ref[rows, :]
                             + jnp.concatenate(slabs, axis=1) * sin_ref[rows, :]).astype(BF16)

    i = pl.program_id(0)
    hb_a = normed(rows_a)
    u_a = _dot(jnp.concatenate([modulated(xp_ref[0]), hb_a], axis=0), wu_ref[...])
    hb_b = normed(rows_b)
    gates_and_keys(rows_a, hb_a)
    u_b = _dot(jnp.concatenate([hb_b, modulated(xn_ref[0])], axis=0), wu_ref[...])
    keep_prev = (i > 0).astype(F32)
    keep_next = (i < pl.num_programs(0) - 1).astype(F32)

    def short_conv(cols):
        ext = jnp.concatenate([u_a[:PROJ_HALO, cols] * keep_prev, u_a[PROJ_HALO:, cols], u_b[:half, cols],
                               u_b[half:, cols] * keep_next], axis=0)
        n = ext.shape[0]
        cw = cw_ref[:, cols]
        prev = pltpu.roll(ext, 1, axis=0)[PROJ_HALO:PROJ_HALO + tm]
        nxt = pltpu.roll(ext, n - 1, axis=0)[PROJ_HALO:PROJ_HALO + tm]
        u_ref[0, :, cols] = (prev * cw[0:1] + ext[PROJ_HALO:PROJ_HALO + tm] * cw[1:2] + nxt * cw[2:3]
                             + cb_ref[:, cols]).astype(BF16)

    third = u_a.shape[1] // 3
    short_conv(slice(0, third))
    gates_and_keys(rows_b, hb_b)
    short_conv(slice(third, 2 * third))
    hb = jnp.concatenate([hb_a, hb_b], axis=0)
    qv = _dot_nt(wqt_ref[...], hb)
    partner = jnp.concatenate(
        [qv[r + ROT:r + 2 * ROT] if part == 0 else qv[r:r + ROT]
         for r in range(0, qv.shape[0], 2 * ROT) for part in (0, 1)], axis=0)
    qt_ref[0] = (qv * cost_ref[...] + partner * sint_ref[...]).astype(BF16)
    short_conv(slice(2 * third, 3 * third))
    vt_ref[0] = _dot_nt(wvt_ref[...], hb).astype(BF16)


def _proj(x, g, sh, sc, cos, sin, cost, sint, conv_w, conv_b, wu, wqt, wk, wvt, wg):
    b, l, d = x.shape
    tm = 512
    per_halo = tm // PROJ_HALO
    n_halo = l // PROJ_HALO
    tile = lambda i, bi: (bi, i, 0)
    tile_t = lambda i, bi: (bi, 0, i)
    per_b = lambda i, bi: (bi, 0, 0)
    before = lambda i, bi: (bi, jnp.maximum(i * per_halo - 1, 0), 0)
    after = lambda i, bi: (bi, jnp.minimum((i + 1) * per_halo, n_halo - 1), 0)
    bf = lambda shape: jax.ShapeDtypeStruct(shape, BF16)
    n_u, n_g = wu.shape[1], wg.shape[1]
    return pl.pallas_call(
        _proj_body,
        grid=(l // tm, b),
        in_specs=[
            pl.BlockSpec((1, tm, d), tile),
            pl.BlockSpec((1, PROJ_HALO, d), before), pl.BlockSpec((1, PROJ_HALO, d), after),
            _const_spec((1, d)),
            pl.BlockSpec((1, 1, d), per_b), pl.BlockSpec((1, 1, d), per_b),
            pl.BlockSpec((tm, D_ATTN), lambda i, bi: (i, 0)), pl.BlockSpec((tm, D_ATTN), lambda i, bi: (i, 0)),
            pl.BlockSpec((D_ATTN, tm), lambda i, bi: (0, i)), pl.BlockSpec((D_ATTN, tm), lambda i, bi: (0, i)),
        ] + [_const_spec(w.shape) for w in (conv_w, conv_b, wu, wqt, wk, wvt, wg)],
        out_specs=[pl.BlockSpec((1, tm, n_u), tile), pl.BlockSpec((1, D_ATTN, tm), tile_t),
                   pl.BlockSpec((1, tm, D_ATTN), tile), pl.BlockSpec((1, D_ATTN, tm), tile_t),
                   pl.BlockSpec((1, tm, n_g), tile)],
        out_shape=[bf((b, l, n_u)), bf((b, D_ATTN, l)), bf((b, l, D_ATTN)), bf((b, D_ATTN, l)), bf((b, l, n_g))],
        compiler_params=_params("arbitrary", "arbitrary"),
        name="proj",
    )(x, x, x, g, sh, sc, cos, sin, cost, sint, conv_w, conv_b, wu, wqt, wk, wvt, wg)


def _fft_dims(l):
    n = 2 * l
    n1 = n // FFT_N2
    h1 = n1 // 2
    k1 = h1 + 1
    ng = FFT_N2 // FFT_GROUP
    return n, n1, h1, k1, ng


@functools.lru_cache(maxsize=None)
def _fft_constants(l):
    n, n1, h1, k1n, ng = _fft_dims(l)
    g = FFT_GROUP
    ma = np.zeros((ng, k1n, 2, g, h1, g), np.float64)
    for gi in range(ng):
        for j in range(g):
            n2 = gi * g + j
            nn = FFT_N2 * np.arange(h1)[None, :] + n2
            theta = 2.0 * np.pi * ((np.arange(k1n)[:, None] * nn) % n) / n
            ma[gi, :, 0, j, :, j] = np.cos(theta)
            ma[gi, :, 1, j, :, j] = -np.sin(theta)
    weight = np.full((k1n,), 2.0)
    weight[0] = weight[-1] = 1.0
    ma_fwd = ma.reshape(ng, k1n * 2 * g, h1 * g)
    ma_inv = (ma * (weight / n)[None, :, None, None, None, None]).reshape(ng, k1n * 2 * g, h1 * g)
    ma_inv = np.transpose(ma_inv, (0, 2, 1))
    kpad = -(-ma_inv.shape[2] // LANES) * LANES
    ma_inv = np.pad(ma_inv, ((0, 0), (0, 0), (0, kpad - ma_inv.shape[2])))
    ang = 2.0 * np.pi * ((np.arange(FFT_N2)[:, None] * np.arange(FFT_N2)[None, :]) % FFT_N2) / FFT_N2
    fr, fi = np.cos(ang), -np.sin(ang)
    dft3 = np.stack([fr, fi, fr + fi, fr - fi])
    mb_filt = np.block([[fr, -fi, fr, -fi], [fi, fr, -fi, -fr]])
    return tuple(np.asarray(a, np.float32) for a in (ma_fwd, ma_inv, dft3, mb_filt))


def _fft_stage_a(src_ref, ma_ref, spec_ref, l):
    _, _, h1, k1n, ng = _fft_dims(l)
    g = FFT_GROUP

    def group(gi, carry):
        off = gi * g
        xg = jnp.concatenate(
            [src_ref[pl.ds(pl.multiple_of(FFT_N2 * i + off, g), g), :] for i in range(h1)], axis=0)
        bg = _dot(ma_ref[gi], xg.astype(BF16)).astype(BF16)
        for c in range(2 * k1n):
            spec_ref[pl.ds(pl.multiple_of(c * FFT_N2 + off, g), g), :] = bg[c * g:(c + 1) * g]
        return carry

    lax.fori_loop(0, ng, group, 0, unroll=FFT_GROUP_UNROLL)


def _fft_stage_a_inv(spec_ref, ma_inv_ref, emit, l):
    _, _, h1, k1n, ng = _fft_dims(l)
    g = FFT_GROUP
    kpad = ma_inv_ref.shape[2] - 2 * k1n * g

    def group(gi, carry):
        off = gi * g
        chunks = [spec_ref[pl.ds(pl.multiple_of(c * FFT_N2 + off, g), g), :] for c in range(2 * k1n)]
        if kpad:
            chunks.append(jnp.zeros((kpad, spec_ref.shape[1]), BF16))
        yg = _dot(ma_inv_ref[gi], jnp.concatenate(chunks, axis=0))
        for i in range(h1):
            emit(pl.ds(pl.multiple_of(FFT_N2 * i + off, g), g), yg[i * g:(i + 1) * g])
        return carry

    lax.fori_loop(0, ng, group, 0, unroll=FFT_GROUP_UNROLL)


def _filt_body(z_ref, w1_ref, b1_ref, w2_ref, b2_ref, fq_ref, w3f_ref, b3f_ref, w3b_ref, b3b_ref,
               dl_ref, ma_ref, mb_ref, hf_ref, hdn_ref, sig_ref, specf_ref, specb_ref, *, l):
    _, _, _, k1n, _ = _fft_dims(l)

    @pl.when((pl.program_id(0) == 0) & (pl.program_id(1) == 0))
    def _():
        fq = fq_ref[...]
        h1 = jnp.sin(fq * (_dot_x3(z_ref[...], w1_ref[...]) + b1_ref[...]))
        hdn_ref[...] = jnp.sin(fq * (_dot_x3(h1, w2_ref[...]) + b2_ref[...])).astype(BF16)

    hdn = hdn_ref[...]
    cb = specf_ref.shape[1]
    blk = 2 * FFT_N2
    rows = lax.broadcasted_iota(jnp.int32, (l, cb), 0)
    window = jnp.exp(rows.astype(F32) * (-1.0 / (l - 1)) * dl_ref[...])

    sig_ref[...] = (_dot(hdn, w3f_ref[...].astype(BF16)) + b3f_ref[...]) * window
    _fft_stage_a(sig_ref, ma_ref, specf_ref, l)
    sig_ref[...] = jnp.where(rows > 0, (_dot(hdn, w3b_ref[...].astype(BF16)) + b3b_ref[...]) * window, 0.0)
    _fft_stage_a(sig_ref, ma_ref, specb_ref, l)

    def blocks(starts):
        x = jnp.concatenate(
            [jnp.concatenate([specf_ref[pl.ds(st, blk), :], specb_ref[pl.ds(st, blk), :]], axis=0)
             for st in starts], axis=1)
        s = _dot(mb_ref[...], x).astype(BF16)
        for n, st in enumerate(starts):
            hf_ref[0, pl.ds(st, blk), :] = s[:, n * cb:(n + 1) * cb]

    def per_pair(i, carry):
        st = pl.multiple_of(i * 2 * blk, blk)
        blocks([st, st + blk])
        return carry

    lax.fori_loop(0, k1n // 2, per_pair, 0)
    if k1n % 2:
        blocks([(k1n - 1) * blk])


def _filt(zfeat, w1, b1, w2, b2, fq, w3, b3, deltas, ma_fwd, mb_filt, order, l):
    _, _, _, k1n, _ = _fft_dims(l)
    cb = HY_CB
    nblk = D_HYENA // cb
    srows = k1n * 2 * FFT_N2
    fcol = lambda o, c: (0, (2 * o) * nblk + c)
    bcol = lambda o, c: (0, (2 * o + 1) * nblk + c)
    return pl.pallas_call(
        functools.partial(_filt_body, l=l),
        grid=(order, nblk),
        in_specs=[
            _const_spec(zfeat.shape), _const_spec(w1.shape), _const_spec(b1.shape),
            _const_spec(w2.shape), _const_spec(b2.shape), _const_spec(fq.shape),
            pl.BlockSpec((w3.shape[0], cb), fcol), pl.BlockSpec((1, cb), fcol),
            pl.BlockSpec((w3.shape[0], cb), bcol), pl.BlockSpec((1, cb), bcol),
            pl.BlockSpec((1, cb), lambda o, c: (0, c)),
            _const_spec(ma_fwd.shape), _const_spec(mb_filt.shape),
        ],
        out_specs=pl.BlockSpec((1, srows, cb), lambda o, c: (o, 0, c)),
        out_shape=jax.ShapeDtypeStruct((order, srows, D_HYENA), BF16),
        scratch_shapes=[
            pltpu.VMEM((l, w2.shape[1]), BF16),
            pltpu.VMEM((l, cb), F32),
            pltpu.VMEM((srows, cb), BF16),
            pltpu.VMEM((srows, cb), BF16),
        ],
        compiler_params=_params("arbitrary", "arbitrary"),
        name="filt",
    )(zfeat, w1, b1, w2, b2, fq, w3, b3, w3, b3, deltas, ma_fwd, mb_filt)


HY_ROWS = 512


def _hyena_body(u0_ref, u1_ref, u2_ref, db_ref, hf_ref,
                ma_ref, mai_ref, dft_ref, o_ref, z_ref, y_ref, spec_ref, *, l, order):
    _, _, _, k1n, _ = _fft_dims(l)
    blk = 2 * FFT_N2
    gates = (u1_ref, u2_ref)

    for r0 in range(0, l, HY_ROWS):
        z_ref[r0:r0 + HY_ROWS, :] = u0_ref[0, r0:r0 + HY_ROWS, :].astype(F32)

    for o in range(order):
        _fft_stage_a(z_ref, ma_ref, spec_ref, l)

        cb = spec_ref.shape[1]
        groups = [list(range(i, min(i + 2, k1n))) for i in range(0, k1n, 2)]

        def stage_b(ks):
            br = jnp.concatenate([spec_ref[k * blk:k * blk + FFT_N2, :] for k in ks], axis=1)
            bi = jnp.concatenate([spec_ref[k * blk + FFT_N2:(k + 1) * blk, :] for k in ks], axis=1)
            k1 = _dot(dft_ref[2], br)
            return k1 - _dot(dft_ref[1], br + bi), k1 + _dot(dft_ref[0], bi - br)

        def filter_and_invert(ks, s, o=o):
            sr, si = s
            hr = jnp.concatenate([hf_ref[o, k * blk:k * blk + FFT_N2, :] for k in ks], axis=1).astype(F32)
            hi = jnp.concatenate([hf_ref[o, k * blk + FFT_N2:(k + 1) * blk, :] for k in ks], axis=1).astype(F32)
            yr = sr * hr - si * hi
            yi = sr * hi + si * hr
            k1 = _dot(dft_ref[3], yr.astype(BF16))
            cr = (k1 + _dot(dft_ref[1], (yr + yi).astype(BF16))).astype(BF16)
            ci = (k1 + _dot(dft_ref[0], (yi - yr).astype(BF16))).astype(BF16)
            for n, k in enumerate(ks):
                spec_ref[k * blk:k * blk + FFT_N2, :] = cr[:, n * cb:(n + 1) * cb]
                spec_ref[k * blk + FFT_N2:(k + 1) * blk, :] = ci[:, n * cb:(n + 1) * cb]

        s_next = stage_b(groups[0])
        for gi, ks in enumerate(groups):
            s_cur = s_next
            if gi + 1 < len(groups):
                s_next = stage_b(groups[gi + 1])
            filter_and_invert(ks, s_cur)

        def store_y(rows, y):
            y_ref[rows, :] = y

        _fft_stage_a_inv(spec_ref, mai_ref, store_y, l)

        dbias = db_ref[o:o + 1, :]
        for r0 in range(0, l, HY_ROWS):
            rows = slice(r0, r0 + HY_ROWS)
            znew = gates[o][0, rows, :].astype(F32) * (y_ref[rows, :] + z_ref[rows, :] * dbias)
            if o + 1 < order:
                z_ref[rows, :] = znew
            else:
                o_ref[0, rows, :] = znew.astype(BF16)


def _hyena(u, dbias, hf, consts, order):
    b, l, _ = u.shape
    _, _, _, k1n, _ = _fft_dims(l)
    cb = HY_CB
    nblk = D_HYENA // cb
    srows = k1n * 2 * FFT_N2
    ma_fwd, ma_inv, dft3 = consts
    upart = lambda p: pl.BlockSpec((1, l, cb), lambda c, bi: (bi, 0, p * nblk + c))
    return pl.pallas_call(
        functools.partial(_hyena_body, l=l, order=order),
        grid=(nblk, b),
        in_specs=[
            upart(0), upart(1), upart(2),
            pl.BlockSpec((order, cb), lambda c, bi: (0, c)),
            pl.BlockSpec((order, srows, cb), lambda c, bi: (0, 0, c), pipeline_mode=pl.Buffered(1)),
            _const_spec(ma_fwd.shape), _const_spec(ma_inv.shape),
            _const_spec(dft3.shape),
        ],
        out_specs=pl.BlockSpec((1, l, cb), lambda c, bi: (bi, 0, c)),
        out_shape=jax.ShapeDtypeStruct((b, l, D_HYENA), BF16),
        scratch_shapes=[
            pltpu.VMEM((l, cb), F32),
            pltpu.VMEM((l, cb), F32),
            pltpu.VMEM((srows, cb), BF16),
        ],
        compiler_params=_params("arbitrary", "arbitrary"),
        name="hyena",
    )(u, u, u, dbias, hf, ma_fwd, ma_inv, dft3)


ATT_TQ = MXU_DIM
ATT_KEY_CHUNK = 2 * MXU_DIM
ATT_SCORE_ROWS = MXU_DIM
ATT_BLOCKS = 8


def _attn_body(qc_ref, qn_ref, k_ref, kn_ref, kc_ref, kcn_ref, vt_ref, vct_ref, lam_ref, sg_ref, o_ref,
               sa_ref, sb_ref, ma_ref, mb_ref):
    tq = ATT_TQ
    lat = k_ref.shape[1]
    lv = lam_ref[...]
    lam = (jnp.exp(jnp.sum(lv[0:1] * lv[1:2], axis=-1, keepdims=True))
           - jnp.exp(jnp.sum(lv[2:3] * lv[3:4], axis=-1, keepdims=True)) + LAM_INIT)

    n_ctx = kc_ref.shape[1]
    chunks = [(False, r, r, ATT_KEY_CHUNK) for r in range(0, lat, ATT_KEY_CHUNK)] + [(True, 0, lat, n_ctx)]

    def stage(qt, keys_ref, ctx_keys_ref, s_new, m_new, s_old, m_old, hook=None):
        if hook is not None:
            hook()
        feat = lax.broadcasted_iota(jnp.int32, qt.shape, 0)
        qz = jnp.concatenate([jnp.where(feat < HEAD_DIM, qt, jnp.zeros_like(qt)),
                              jnp.where(feat >= HEAD_DIM, qt, jnp.zeros_like(qt))], axis=1)
        has_old = s_old is not None
        if has_old:
            m_prev = m_old[...]

        def chunk(keys, vals, rows, carry):
            m_run, denom, acc = carry
            s = _dot(keys, qz)
            s_new[rows, :] = s
            m_run = jnp.maximum(m_run, jnp.max(s, axis=0, keepdims=True))
            if has_old:
                p = jnp.exp2(s_old[rows, :] - m_prev)
                denom = denom + jnp.sum(p, axis=0, keepdims=True)
                acc = acc + _dot(vals, p.astype(BF16))
            return m_run, denom, acc

        def latent(c, carry):
            rows = pl.ds(pl.multiple_of(c * ATT_KEY_CHUNK, ATT_KEY_CHUNK), ATT_KEY_CHUNK)
            return chunk(keys_ref[0, rows, :], vt_ref[0, :, rows], rows, carry)

        init = (jnp.full((1, 2 * tq), -jnp.inf, F32), jnp.zeros((1, 2 * tq), F32),
                jnp.zeros((2 * HEAD_DIM, 2 * tq), F32))
        carry = lax.fori_loop(0, lat // ATT_KEY_CHUNK, latent, init)
        m_run, denom, acc = chunk(ctx_keys_ref[0], vct_ref[0], slice(lat, lat + n_ctx), carry)
        m_new[...] = m_run
        return acc, denom

    def finish(acc, denom):
        r = 1.0 / denom
        ot = acc[:, :tq] * r[:, :tq] - acc[:, tq:] * (lam * r[:, tq:])
        return (_rms(ot.T, sg_ref[...]) * (1.0 - LAM_INIT)).astype(BF16)

    @pl.when(pl.program_id(0) == 0)
    def _():
        stage(qc_ref[0, :, :tq], k_ref, kc_ref, sa_ref, ma_ref, None, None)

    bufs = ((sa_ref, ma_ref), (sb_ref, mb_ref))
    pending = None
    for n in range(ATT_BLOCKS):
        if n + 1 < ATT_BLOCKS:
            new = (qc_ref[0, :, (n + 1) * tq:(n + 2) * tq], k_ref, kc_ref)
        else:
            new = (qn_ref[0, :, :tq], kn_ref, kcn_ref)
        raw = stage(*new, *bufs[(n + 1) % 2], *bufs[n % 2], hook=pending)

        def pending(raw=raw, n=n):
            o_ref[0, n * tq:(n + 1) * tq, :] = finish(*raw)

    pending()


def _attn(qt, k, vt, kc, vct, lamv, subln_g):
    b, l, _ = k.shape
    c = kc.shape[1]
    dv = 2 * HEAD_DIM
    tq = ATT_TQ
    tstep = ATT_BLOCKS * tq
    assert l % tstep == 0 and ATT_BLOCKS % 2 == 0
    pairs = l // tstep
    steps = b * N_HEADS * pairs

    def split(j):
        return j // (N_HEADS * pairs), (j // pairs) % N_HEADS, j % pairs

    def cur(fn):
        return lambda j: fn(*split(j))

    def nxt(fn):
        return lambda j: fn(*split(jnp.minimum(j + 1, steps - 1)))

    q_map = lambda bi, h, i: (bi, h, i)
    k_map = lambda bi, h, i: (bi, 0, h)
    v_map = lambda bi, h, i: (bi, h, 0)
    return pl.pallas_call(
        _attn_body,
        grid=(steps,),
        in_specs=[
            pl.BlockSpec((1, dv, tstep), cur(q_map)), pl.BlockSpec((1, dv, tstep), nxt(q_map)),
            pl.BlockSpec((1, l, dv), cur(k_map)), pl.BlockSpec((1, l, dv), nxt(k_map)),
            pl.BlockSpec((1, c, dv), cur(k_map)), pl.BlockSpec((1, c, dv), nxt(k_map)),
            pl.BlockSpec((1, dv, l), cur(v_map)), pl.BlockSpec((1, dv, c), cur(v_map)),
            _const_spec(lamv.shape), _const_spec(subln_g.shape),
        ],
        out_specs=pl.BlockSpec((1, tstep, dv), cur(lambda bi, h, i: (bi, i, h))),
        out_shape=jax.ShapeDtypeStruct((b, l, N_HEADS * dv), BF16),
        scratch_shapes=[pltpu.VMEM((l + c, 2 * tq), F32), pltpu.VMEM((l + c, 2 * tq), F32),
                        pltpu.VMEM((1, 2 * tq), F32), pltpu.VMEM((1, 2 * tq), F32)],
        compiler_params=_params("arbitrary"),
        name="attn",
    )(qt, qt, k, k, kc, kc, vt, vct, lamv, subln_g)


def _post_body(x_ref, zh_ref, oa_ref, gate_ref, g1_ref, sh2_ref, sc2_ref, g2_ref,
               gpost_ref, gfpre_ref, gfpost_ref,
               whu_ref, wau_ref, wo_ref, wg_ref, wu_ref, wd_ref, o_ref, *, ff_chunks):
    d = x_ref.shape[2]
    half = x_ref.shape[1] // 2
    rows_a, rows_b = slice(0, half), slice(half, 2 * half)

    def lifts(rows):
        return _dot(zh_ref[0, rows, :], whu_ref[...]), _dot(oa_ref[0, rows, :], wau_ref[...])

    def out_proj(rows, y_hy, y_att):
        mixed = gate_ref[0, rows, :d].astype(F32) * y_hy + gate_ref[0, rows, d:].astype(F32) * y_att
        return _dot(mixed.astype(BF16), wo_ref[...])

    def residual_and_norm(rows, mixed):
        x1 = x_ref[0, rows, :] + g1_ref[0] * _rms(mixed, gpost_ref[...])
        return x1, (_rms(x1, gfpre_ref[...]) * (1.0 + sc2_ref[0]) + sh2_ref[0]).astype(BF16)

    def ffn_chunk(hf, f, chunk):
        c0, cw = chunk
        a = _dot(hf, wg_ref[:, c0:c0 + cw])
        t = (a * jax.nn.sigmoid(a) * _dot(hf, wu_ref[:, c0:c0 + cw])).astype(BF16)
        part = _dot(t, wd_ref[c0:c0 + cw, :])
        return part if f is None else f + part

    def finish(rows, x1, f):
        o_ref[0, rows, :] = x1 + g2_ref[0] * _rms(f, gfpost_ref[...])

    mixed_a = out_proj(rows_a, *lifts(rows_a))
    lifted_b = lifts(rows_b)
    x1_a, hf_a = residual_and_norm(rows_a, mixed_a)
    mixed_b = out_proj(rows_b, *lifted_b)
    f_a = ffn_chunk(hf_a, None, ff_chunks[0])
    x1_b, hf_b = residual_and_norm(rows_b, mixed_b)
    for chunk in ff_chunks[1:]:
        f_a = ffn_chunk(hf_a, f_a, chunk)
    f_b = ffn_chunk(hf_b, None, ff_chunks[0])
    finish(rows_a, x1_a, f_a)
    for chunk in ff_chunks[1:]:
        f_b = ffn_chunk(hf_b, f_b, chunk)
    finish(rows_b, x1_b, f_b)


def _post(x, zh, oa, gate, g1, sh2, sc2, g2, gpost, gfpre, gfpost, whu, wau, wo, wg, wu, wd):
    b, l, d = x.shape
    dff = wg.shape[1]
    tm = 512
    ff_chunk = 4 * MXU_DIM
    ff_chunks = tuple((c0, min(ff_chunk, dff - c0)) for c0 in range(0, dff, ff_chunk))
    tile = lambda bi, i: (bi, i, 0)
    per_b = lambda bi, i: (bi, 0, 0)
    return pl.pallas_call(
        functools.partial(_post_body, ff_chunks=ff_chunks),
        grid=(b, l // tm),
        in_specs=[
            pl.BlockSpec((1, tm, d), tile),
            pl.BlockSpec((1, tm, zh.shape[2]), tile),
            pl.BlockSpec((1, tm, oa.shape[2]), tile),
            pl.BlockSpec((1, tm, gate.shape[2]), tile),
            pl.BlockSpec((1, 1, d), per_b), pl.BlockSpec((1, 1, d), per_b),
            pl.BlockSpec((1, 1, d), per_b), pl.BlockSpec((1, 1, d), per_b),
            _const_spec((1, d)), _const_spec((1, d)), _const_spec((1, d)),
        ] + [_const_spec(w.shape) for w in (whu, wau, wo, wg, wu, wd)],
        out_specs=pl.BlockSpec((1, tm, d), tile),
        out_shape=jax.ShapeDtypeStruct((b, l, d), F32),
        compiler_params=_params("arbitrary", "arbitrary"),
        name="post",
    )(x, zh, oa, gate, g1, sh2, sc2, g2, gpost, gfpre, gfpost, whu, wau, wo, wg, wu, wd)


@functools.lru_cache(maxsize=None)
def _rope_tables(l):
    quarter = HEAD_DIM // 4
    inv = ROPE_BASE ** (-np.arange(quarter, dtype=np.float64) / quarter)
    t = np.arange(l)
    ang_r = (t // GRID_W)[:, None] * inv[None, :]
    ang_c = (t % GRID_W)[:, None] * inv[None, :]
    ang = np.concatenate([ang_r, ang_r, ang_c, ang_c], axis=1)
    reps = D_ATTN // HEAD_DIM
    cos = np.tile(np.cos(ang), (1, reps)).astype(np.float32)
    second = (np.arange(D_ATTN) % (2 * quarter)) >= quarter
    sin = (np.tile(np.sin(ang), (1, reps)) * np.where(second, 1.0, -1.0)[None, :]).astype(np.float32)
    return cos, sin


@functools.lru_cache(maxsize=None)
def _filter_features(l, emb, kpad):
    bands = (emb - 1) // 2
    t = np.linspace(0.0, 1.0, l, dtype=np.float32).astype(np.float64)[:, None]
    w = (2.0 * math.pi / l) * np.arange(l, dtype=np.float64)[:, None]
    fr = np.linspace(1e-4, bands - 1, bands, dtype=np.float32).astype(np.float64)[None, :]
    z = np.concatenate([t, np.cos(fr * w), -np.sin(fr * w)], axis=-1)
    z = np.pad(z, ((0, 0), (0, kpad - emb))).astype(np.float32)
    deltas = np.abs(np.linspace(math.log(FILTER_TARGET) / SLOW_DECAY_PCT,
                                math.log(FILTER_TARGET) / FAST_DECAY_PCT, D_HYENA, dtype=np.float32))
    return z, deltas[None, :].astype(np.float32)


def _pad_to(a, rows, cols, value=0.0):
    return jnp.pad(a, ((0, rows - a.shape[0]), (0, cols - a.shape[1])), constant_values=value)


def kernel(x, c, ctx, c_ctx, w_ada, b_ada, g_mix_pre, g_mix_post, g_ffn_pre, g_ffn_post, w_in, hy_conv_w, hy_conv_b, hy_f_w1, hy_f_b1, hy_f_w2, hy_f_b2, hy_f_w3, hy_f_b3, hy_f_freq, hy_bias, lambda_q1, lambda_k1, lambda_q2, lambda_k2, att_subln_g, w_hy_up, w_att_up, w_out, w_ffn_gate, w_ffn_up, w_ffn_down):
    b, l, d = x.shape
    order = hy_bias.shape[1]
    off_q = (order + 1) * D_HYENA
    off_k = off_q + D_ATTN
    off_v = off_k + D_ATTN
    off_g = off_v + D_ATTN
    assert w_in.shape[0] == 1, "single-layer block"
    assert l % HY_ROWS == 0 and l % 512 == 0

    pad_rows = -(-(b + 1) // 8) * 8
    cc = jnp.concatenate([c, c_ctx[None, :], jnp.zeros((pad_rows - b - 1, d), F32)], axis=0)
    ada = _ada(cc, w_ada[0], b_ada[0][None, :])
    sh1, sc1, g1, sh2, sc2, g2 = (ada[:b, i * d:(i + 1) * d][:, None, :] for i in range(6))
    csh1, csc1 = ada[b:b + 1, 0:d], ada[b:b + 1, d:2 * d]

    row = lambda v: v[0][None, :]
    w = w_in[0]
    cos, sin = _rope_tables(l)
    q_scale = HEAD_DIM ** -0.5 * math.log2(math.e)
    wq = (w[:, off_q:off_k] * q_scale).astype(BF16)
    wk = w[:, off_k:off_v].astype(BF16)
    wv = w[:, off_v:off_g].astype(BF16)

    kc, vct, wqt, wvt = _ctx_kv(ctx, row(g_mix_pre), csh1, csc1, wk, wq, wv)

    u, qt, k, vt, gate = _proj(x, row(g_mix_pre), sh1, sc1, jnp.asarray(cos), jnp.asarray(sin),
                               jnp.asarray(cos.T), jnp.asarray(sin.T), hy_conv_w[0], row(hy_conv_b),
                               w[:, :off_q].astype(BF16), wqt, wk, wvt, w[:, off_g:].astype(BF16))

    consts = tuple(jnp.asarray(a).astype(BF16) for a in _fft_constants(l))
    zfeat, deltas = _filter_features(l, hy_f_w1.shape[1], LANES)
    hf = _filt(jnp.asarray(zfeat),
               _pad_to(hy_f_w1[0], LANES, LANES), _pad_to(row(hy_f_b1), 1, LANES),
               _pad_to(hy_f_w2[0], LANES, LANES), _pad_to(row(hy_f_b2), 1, LANES),
               _pad_to(row(hy_f_freq), 1, LANES, 1.0),
               _pad_to(hy_f_w3[0], LANES, hy_f_w3.shape[2]), row(hy_f_b3),
               jnp.asarray(deltas), consts[0], consts[3], order, l)

    zh = _hyena(u, hy_bias[0], hf, consts[:3], order)

    lamv = jnp.concatenate([lambda_q1, lambda_k1, lambda_q2, lambda_k2], axis=0)
    oa = _attn(qt, k, vt, kc, vct, lamv, row(att_subln_g))

    return _post(x, zh, oa, gate, g1, sh2, sc2, g2, row(g_mix_post), row(g_ffn_pre), row(g_ffn_post),
                 w_hy_up[0].astype(BF16), w_att_up[0].astype(BF16), w_out[0].astype(BF16),
                 w_ffn_gate[0].astype(BF16), w_ffn_up[0].astype(BF16), w_ffn_down[0].astype(BF16))
```
